```python
import jax, jax.numpy as jnp
from jax import lax
import numpy as np

D_MODEL = 2048
BATCH = 4
SEQ = 4096
DEPTH = 2
DEC_BATCH = 4
DEC_SEQ = 2048
PAST_LEN = 128

D_MIX = D_MODEL
D_A = D_MIX // 2
D_B = D_MIX - D_A
N_HEADS_A = 4
HEAD_DIM_A = D_A // N_HEADS_A
N_GATES = 4 * N_HEADS_A
D_IN_EVEN = 4 * D_A + D_B + N_GATES
CHUNK = 128
POOL_WINDOWS = (2, 4, 8, 16)
N_POOL_GROUPS = len(POOL_WINDOWS)
POOL_GROUP_DIM = D_B // N_POOL_GROUPS
N_FOURIER_GROUPS = 4
FOURIER_GROUP_DIM = D_MODEL // N_FOURIER_GROUPS
PEER_HEADS = 8
PEER_NKEYS = 128
PEER_N_EXPERTS = PEER_NKEYS * PEER_NKEYS
PEER_QDIM = 256
PEER_HALF = PEER_QDIM // 2
PEER_TOPK = 16
PEER_TOKEN_BLOCK = 128
N_EVEN = (DEPTH + 1) // 2
N_ODD = DEPTH // 2
EPS = 1e-6

kernel_name = 'hybrid_mlstm_pool_fnet_peer_encoder'


def rmsnorm(x, g):
    xf = x.astype(jnp.float32)
    y = xf * lax.rsqrt(jnp.mean(xf * xf, axis=-1, keepdims=True) + EPS)
    return y * g.astype(jnp.float32)


def mlstm_direction(q, k, v, li, lf):
    B, H, S, Dh = q.shape
    nc = S // CHUNK

    def to_chunks(a):
        return jnp.moveaxis(a.reshape(B, H, nc, CHUNK, *a.shape[3:]), 2, 0)

    xs = tuple(to_chunks(a) for a in (q, k, v, li, lf))
    lower = jnp.tril(jnp.ones((CHUNK, CHUNK), dtype=bool))

    def step(carry, inp):
        C, n, m = carry
        qb, kb, vb, lib, lfb = inp
        b = jnp.cumsum(lfb, axis=-1)
        logw = b[..., :, None] - b[..., None, :] + lib[..., None, :]
        logw = jnp.where(lower, logw, -jnp.inf)
        inter = b + m[..., None]
        m_t = jnp.maximum(inter, jnp.max(logw, axis=-1))
        w = jnp.exp(logw - m_t[..., None])
        a_inter = jnp.exp(inter - m_t)
        s = jnp.einsum('bhtd,bhsd->bhts', qb, kb) * w
        num = jnp.einsum('bhts,bhse->bhte', s, vb) + a_inter[..., None] * jnp.einsum('bhtd,bhde->bhte', qb, C)
        den = jnp.sum(s, axis=-1) + a_inter * jnp.einsum('bhtd,bhd->bht', qb, n)
        h = num / jnp.maximum(jnp.abs(den), jnp.exp(-m_t))[..., None]
        b_last = b[..., -1]
        logu = b_last[..., None] - b + lib
        m_new = jnp.maximum(b_last + m, jnp.max(logu, axis=-1))
        u = jnp.exp(logu - m_new[..., None])
        decay = jnp.exp(b_last + m - m_new)
        C_new = decay[..., None, None] * C + jnp.einsum('bhs,bhsd,bhse->bhde', u, kb, vb)
        n_new = decay[..., None] * n + jnp.einsum('bhs,bhsd->bhd', u, kb)
        return (C_new, n_new, m_new), h

    init = (jnp.zeros((B, H, Dh, Dh), jnp.float32), jnp.zeros((B, H, Dh), jnp.float32), jnp.zeros((B, H), jnp.float32))
    _, hs = lax.scan(step, init, xs)
    return jnp.moveaxis(hs, 0, 2).reshape(B, H, S, Dh)


def multiscale_pool(u):
    B, S, _ = u.shape
    ug = u.reshape(B, S, N_POOL_GROUPS, POOL_GROUP_DIM)
    cs = jnp.concatenate([jnp.zeros_like(ug[:, :1]), jnp.cumsum(ug, axis=1)], axis=1)
    pos = jnp.arange(S)
    outs = []
    for gi, w in enumerate(POOL_WINDOWS):
        lo = jnp.clip(pos - w // 2, 0, S)
        hi = jnp.clip(pos + (w - w // 2), 0, S)
        csg = cs[:, :, gi]
        mean = (csg[:, hi] - csg[:, lo]) / (hi - lo).astype(jnp.float32)[None, :, None]
        outs.append(mean - ug[:, :, gi])
    return jnp.stack(outs, axis=2)


def even_mixer(xn, w_in, gate_bias, pool_w, pool_scale, w_out):
    B, S, _ = xn.shape
    z = jnp.einsum('bsd,de->bse', xn, w_in).astype(jnp.float32)
    q, k, v, o, u, g = jnp.split(z, [D_A, 2 * D_A, 3 * D_A, 4 * D_A, 4 * D_A + D_B], axis=-1)

    def heads(a):
        return a.reshape(B, S, N_HEADS_A, HEAD_DIM_A).transpose(0, 2, 1, 3)

    q = heads(q) * (HEAD_DIM_A ** -0.5)
    k = heads(k)
    v = heads(v)
    g = (g + gate_bias.astype(jnp.float32)).reshape(B, S, 4, N_HEADS_A).transpose(2, 0, 3, 1)
    li_f, lf_f = g[0], jax.nn.log_sigmoid(g[1])
    li_b, lf_b = g[2], jax.nn.log_sigmoid(g[3])
    h_f = mlstm_direction(q, k, v, li_f, lf_f)
    fl = lambda a: jnp.flip(a, axis=2)
    h_b = fl(mlstm_direction(fl(q), fl(k), fl(v), fl(li_b), fl(lf_b)))
    h_a = (h_f + h_b).transpose(0, 2, 1, 3).reshape(B, S, D_A) * jax.nn.sigmoid(o)
    pooled = multiscale_pool(u)
    y_b = jnp.einsum('bsgc,gcd->bsgd', pooled, pool_w).reshape(B, S, D_B) * pool_scale
    return jnp.einsum('bse,ed->bsd', jnp.concatenate([h_a, y_b], axis=-1), w_out)


def fourier_mixer(xn, w_c):
    B, S, D = xn.shape
    xg = xn.astype(jnp.float32).reshape(B, S, N_FOURIER_GROUPS, FOURIER_GROUP_DIM)
    f = jnp.fft.fft2(xg, axes=(1, 3), norm='ortho').real
    return jnp.einsum('bsd,de->bse', f.reshape(B, S, D), w_c)


def peer(xn, wq, k1, k2, u_tab, v_tab):
    B, S, D = xn.shape
    T = B * S
    xt = xn.reshape(T, D)
    q = jnp.einsum('td,de->te', xt, wq).astype(jnp.float32).reshape(T, PEER_HEADS, 2, PEER_HALF)
    s1 = jnp.einsum('thc,kc->thk', q[:, :, 0], k1.astype(jnp.float32))
    s2 = jnp.einsum('thc,kc->thk', q[:, :, 1], k2.astype(jnp.float32))
    v1, i1 = lax.top_k(s1, PEER_TOPK)
    v2, i2 = lax.top_k(s2, PEER_TOPK)
    cand = (v1[..., :, None] + v2[..., None, :]).reshape(T, PEER_HEADS, PEER_TOPK * PEER_TOPK)
    cand_id = (i1[..., :, None] * PEER_NKEYS + i2[..., None, :]).reshape(T, PEER_HEADS, PEER_TOPK * PEER_TOPK)
    top_s, top_pos = lax.top_k(cand, PEER_TOPK)
    expert = jnp.take_along_axis(cand_id, top_pos, axis=-1)
    gate = jax.nn.softmax(top_s, axis=-1)
    E = PEER_HEADS * PEER_TOPK
    nb = T // PEER_TOKEN_BLOCK

    def block(args):
        xb, eb, gb = args
        act = jax.nn.gelu(jnp.einsum('td,ted->te', xb, u_tab[eb]).astype(jnp.float32), approximate=False)
        return jnp.einsum('te,ted->td', gb * act, v_tab[eb])

    out = lax.map(block, (xt.reshape(nb, PEER_TOKEN_BLOCK, D), expert.reshape(nb, PEER_TOKEN_BLOCK, E), gate.reshape(nb, PEER_TOKEN_BLOCK, E)))
    return out.reshape(B, S, D)


def trunk(x, even_mix_norm, even_w_in, even_gate_bias, even_pool_w, even_pool_scale, even_w_out, odd_mix_norm, odd_fourier_w, ffn_norm, peer_wq, peer_k1, peer_k2, peer_u, peer_v, final_norm):
    for layer in range(DEPTH):
        j = layer // 2
        if layer % 2 == 0:
            xn = rmsnorm(x, even_mix_norm[j])
            x = x + even_mixer(xn, even_w_in[j], even_gate_bias[j], even_pool_w[j], even_pool_scale[j], even_w_out[j]).astype(x.dtype)
        else:
            xn = rmsnorm(x, odd_mix_norm[j])
            x = x + fourier_mixer(xn, odd_fourier_w[j]).astype(x.dtype)
        xn = rmsnorm(x, ffn_norm[layer])
        x = x + peer(xn, peer_wq[layer], peer_k1[layer], peer_k2[layer], peer_u[layer], peer_v[layer]).astype(x.dtype)
    return rmsnorm(x, final_norm).astype(x.dtype)


def setup_inputs(seed: int = 0) -> dict:
    key = jax.random.key(seed)
    ks = jax.random.split(key, 20)
    nrm = lambda k, shape, scale: jax.random.normal(k, shape, jnp.float32) * scale
    gain = lambda k, shape: 1.0 + 0.02 * jax.random.normal(k, shape, jnp.float32)
    gate_base = jnp.repeat(jnp.array([0.0, 3.0, 0.0, 3.0], jnp.float32), N_HEADS_A)
    return {
        'x_prompt': nrm(ks[0], (BATCH, SEQ, D_MODEL), 1.0),
        'x_sample': nrm(ks[1], (DEC_BATCH, DEC_SEQ, D_MODEL), 1.0),
        'even_mix_norm': gain(ks[2], (N_EVEN, D_MODEL)),
        'even_w_in': nrm(ks[3], (N_EVEN, D_MODEL, D_IN_EVEN), D_MODEL ** -0.5),
        'even_gate_bias': gate_base[None, :] + nrm(ks[4], (N_EVEN, N_GATES), 0.1),
        'even_pool_w': nrm(ks[5], (N_EVEN, N_POOL_GROUPS, POOL_GROUP_DIM, POOL_GROUP_DIM), POOL_GROUP_DIM ** -0.5),
        'even_pool_scale': gain(ks[6], (N_EVEN, D_B)),
        'even_w_out': nrm(ks[7], (N_EVEN, D_MIX, D_MODEL), D_MIX ** -0.5),
        'odd_mix_norm': gain(ks[8], (N_ODD, D_MODEL)),
        'odd_fourier_w': nrm(ks[9], (N_ODD, D_MODEL, D_MODEL), D_MODEL ** -0.5),
        'ffn_norm': gain(ks[10], (DEPTH, D_MODEL)),
        'peer_wq': nrm(ks[11], (DEPTH, D_MODEL, PEER_HEADS * PEER_QDIM), D_MODEL ** -0.5),
        'peer_k1': nrm(ks[12], (DEPTH, PEER_NKEYS, PEER_HALF), PEER_HALF ** -0.5),
        'peer_k2': nrm(ks[13], (DEPTH, PEER_NKEYS, PEER_HALF), PEER_HALF ** -0.5),
        'peer_u': nrm(ks[14], (DEPTH, PEER_N_EXPERTS, D_MODEL), D_MODEL ** -0.5),
        'peer_v': nrm(ks[15], (DEPTH, PEER_N_EXPERTS, D_MODEL), PEER_HEADS ** -0.5),
        'final_norm': gain(ks[16], (D_MODEL,)),
    }


def reference(x_prompt, x_sample, even_mix_norm, even_w_in, even_gate_bias, even_pool_w, even_pool_scale, even_w_out, odd_mix_norm, odd_fourier_w, ffn_norm, peer_wq, peer_k1, peer_k2, peer_u, peer_v, final_norm):
    y_prompt = trunk(x_prompt, even_mix_norm, even_w_in, even_gate_bias, even_pool_w, even_pool_scale, even_w_out, odd_mix_norm, odd_fourier_w, ffn_norm, peer_wq, peer_k1, peer_k2, peer_u, peer_v, final_norm)
    y_sample = trunk(x_sample, even_mix_norm, even_w_in, even_gate_bias, even_pool_w, even_pool_scale, even_w_out, odd_mix_norm, odd_fourier_w, ffn_norm, peer_wq, peer_k1, peer_k2, peer_u, peer_v, final_norm)
    return (y_prompt, y_sample)
```

```python
import functools
import math

import jax
import jax.numpy as jnp
from jax import lax
from jax.experimental import pallas as pl
from jax.experimental.pallas import tpu as pltpu

F32 = jnp.float32
BF16 = jnp.bfloat16
I32 = jnp.int32
HI = lax.Precision.HIGHEST

D_MODEL = 2048
EPS = 1e-6
N_HEADS = 4
HEAD_DIM = 256
D_A = N_HEADS * HEAD_DIM
D_B = D_MODEL - D_A
N_GATES = 4 * N_HEADS
CHUNK = 128
N_POOL_GROUPS = 4
POOL_GROUP_DIM = D_B // N_POOL_GROUPS
N_FOURIER_GROUPS = 4
FOURIER_GROUP_DIM = D_MODEL // N_FOURIER_GROUPS
PEER_HEADS = 8
PEER_NKEYS = 128
PEER_HALF = 128
PEER_TOPK = 16
N_EXPERTS = PEER_NKEYS * PEER_NKEYS

LANES = 128
SUBLANES = 8
ROW_TILES = D_MODEL // LANES
VMEM_LIMIT = 56 * 1024 * 1024

PEER_TOKEN_BLOCK = 1024
PEER_EXPERT_CHUNK = 256
PEER_WINDOW = 1024
PEER_UNROLL = 4


def _cparams(n_axes):
    return pltpu.CompilerParams(dimension_semantics=("arbitrary",) * n_axes,
                                vmem_limit_bytes=VMEM_LIMIT)


def _rms(x, g):
    ms = jnp.mean(x * x, axis=-1, keepdims=True)
    return (x * lax.rsqrt(ms + EPS)) * g


def _norm_kernel(a_ref, g_ref, xn_ref):
    xn_ref[...] = _rms(a_ref[...], g_ref[...])


def _addnorm_kernel(a_ref, b_ref, g_ref, x_ref, xn_ref):
    x = a_ref[...] + b_ref[...]
    x_ref[...] = x
    xn_ref[...] = _rms(x, g_ref[...])


def rmsnorm(a, g, tm=256):
    t, d = a.shape
    tm = min(tm, t)
    row = pl.BlockSpec((tm, d), lambda i: (i, 0))
    return pl.pallas_call(
        _norm_kernel,
        grid=(t // tm,),
        in_specs=[row, pl.BlockSpec((1, d), lambda i: (0, 0))],
        out_specs=row,
        out_shape=jax.ShapeDtypeStruct((t, d), F32),
        compiler_params=_cparams(1),
        name="rmsnorm",
    )(a, g.reshape(1, d))


def add_rmsnorm(a, b, g, tm=256):
    t, d = a.shape
    tm = min(tm, t)
    row = pl.BlockSpec((tm, d), lambda i: (i, 0))
    return pl.pallas_call(
        _addnorm_kernel,
        grid=(t // tm,),
        in_specs=[row, row, pl.BlockSpec((1, d), lambda i: (0, 0))],
        out_specs=[row, row],
        out_shape=[jax.ShapeDtypeStruct((t, d), F32)] * 2,
        compiler_params=_cparams(1),
        name="add_rmsnorm",
    )(a, b, g.reshape(1, d))


def _mm_kernel(a_ref, w_ref, o_ref):
    o_ref[...] = jnp.dot(a_ref[...].astype(BF16), w_ref[...],
                         preferred_element_type=F32).astype(o_ref.dtype)


def _mm_res_kernel(a_ref, w_ref, r_ref, o_ref):
    acc = jnp.dot(a_ref[...].astype(BF16), w_ref[...], preferred_element_type=F32)
    o_ref[...] = (acc + r_ref[...]).astype(o_ref.dtype)


def _mm_hi_kernel(a_ref, w_ref, o_ref):
    o_ref[...] = jnp.dot(a_ref[...], w_ref[...], precision=HI,
                         preferred_element_type=F32)


def matmul(a, w, res=None, tm=512, tn=512, out_dtype=F32, name="matmul"):
    m, k = a.shape
    n = w.shape[1]
    tm = min(tm, m)
    tn = min(tn, n)
    in_specs = [pl.BlockSpec((tm, k), lambda i, j: (i, 0)),
                pl.BlockSpec((k, tn), lambda i, j: (0, j))]
    args = [a, w]
    body = _mm_kernel
    if res is not None:
        in_specs.append(pl.BlockSpec((tm, tn), lambda i, j: (i, j)))
        args.append(res)
        body = _mm_res_kernel
    return pl.pallas_call(
        body,
        grid=(m // tm, n // tn),
        in_specs=in_specs,
        out_specs=pl.BlockSpec((tm, tn), lambda i, j: (i, j)),
        out_shape=jax.ShapeDtypeStruct((m, n), out_dtype),
        compiler_params=_cparams(2),
        name=name,
    )(*args)


def matmul_hi(a, w, tm=512):
    m, k = a.shape
    n = w.shape[1]
    tm = min(tm, m)
    return pl.pallas_call(
        _mm_hi_kernel,
        grid=(m // tm,),
        in_specs=[pl.BlockSpec((tm, k), lambda i: (i, 0)),
                  pl.BlockSpec((k, n), lambda i: (0, 0))],
        out_specs=pl.BlockSpec((tm, n), lambda i: (i, 0)),
        out_shape=jax.ShapeDtypeStruct((m, n), F32),
        compiler_params=_cparams(1),
        name="matmul_hi",
    )(a, w)


def _log_sigmoid(x):
    return jnp.minimum(x, 0.0) - jnp.log(1.0 + jnp.exp(-jnp.abs(x)))


def _mlstm_kernel(q_ref, k_ref, v_ref, gc_ref, gr_ref, bc_ref, br_ref, h_ref,
                  c_ref, n_ref, m_ref):
    d = pl.program_id(0)
    hh = pl.program_id(2)
    c = pl.program_id(3)
    L = CHUNK

    @pl.when(c == 0)
    def _():
        c_ref[...] = jnp.zeros_like(c_ref)
        n_ref[...] = jnp.zeros_like(n_ref)
        m_ref[...] = jnp.zeros_like(m_ref)

    q = q_ref[...] * (HEAD_DIM ** -0.5)
    k = k_ref[...]
    v = v_ref[...]
    gcol = gc_ref[...] + bc_ref[...]
    grow = gr_ref[...] + br_ref[...]
    li_idx = 2 * d * N_HEADS + hh
    lf_idx = li_idx + N_HEADS
    lane = lax.broadcasted_iota(I32, gcol.shape, 1)
    sub = lax.broadcasted_iota(I32, grow.shape, 0)
    li_col = jnp.sum(jnp.where(lane == li_idx, gcol, 0.0), axis=1, keepdims=True)
    gf_col = jnp.sum(jnp.where(lane == lf_idx, gcol, 0.0), axis=1, keepdims=True)
    li_row = jnp.sum(jnp.where(sub == li_idx, grow, 0.0), axis=0, keepdims=True)
    gf_row = jnp.sum(jnp.where(sub == lf_idx, grow, 0.0), axis=0, keepdims=True)
    lf_col = _log_sigmoid(gf_col)
    lf_row = _log_sigmoid(gf_row)

    r = lax.broadcasted_iota(I32, (L, L), 0)
    cc = lax.broadcasted_iota(I32, (L, L), 1)
    sign = 1 - 2 * d
    mask = (cc - r) * sign <= 0
    mask_t = (r - cc) * sign <= 0
    b_col = jnp.sum(jnp.where(mask, lf_row, 0.0), axis=1, keepdims=True)
    b_row = jnp.sum(jnp.where(mask_t, lf_col, 0.0), axis=0, keepdims=True)
    logw = jnp.where(mask, b_col - b_row + li_row, -jnp.inf)

    m_prev = m_ref[:, 0:1]
    inter = b_col + m_prev
    m_t = jnp.maximum(inter, jnp.max(logw, axis=1, keepdims=True))
    w = jnp.exp(logw - m_t)
    a_inter = jnp.exp(inter - m_t)
    s = lax.dot_general(q, k, (((1,), (1,)), ((), ())), precision=HI,
                        preferred_element_type=F32) * w
    cm = c_ref[...]
    num = (jnp.dot(s, v, precision=HI, preferred_element_type=F32)
           + a_inter * jnp.dot(q, cm, precision=HI, preferred_element_type=F32))
    den = (jnp.sum(s, axis=1, keepdims=True)
           + a_inter * jnp.sum(q * n_ref[...], axis=1, keepdims=True))
    h_ref[...] = num / jnp.maximum(jnp.abs(den), jnp.exp(-m_t))

    b_last = jnp.sum(lf_row, axis=1, keepdims=True)
    logu = b_last - b_col + li_col
    m_new = jnp.maximum(b_last + m_prev, jnp.max(logu, axis=0, keepdims=True))
    u = jnp.exp(logu - m_new)
    decay = jnp.exp(b_last + m_prev - m_new)
    uk = u * k
    c_ref[...] = decay * cm + lax.dot_general(
        uk, v, (((0,), (0,)), ((), ())), precision=HI, preferred_element_type=F32)
    n_ref[...] = decay * n_ref[...] + jnp.sum(uk, axis=0, keepdims=True)
    m_ref[...] = jnp.broadcast_to(m_new, m_ref.shape)


def mlstm(z, gates, gates_t, bias_row, bias_col, base_tok, batch, seq):
    nc = seq // CHUNK
    base = base_tok // CHUNK

    def blk(dd, b, c):
        return base + b * nc + c + dd * (nc - 1 - 2 * c)

    def sec(which):
        return pl.BlockSpec((CHUNK, HEAD_DIM),
                            lambda dd, b, h, c: (blk(dd, b, c), which * N_HEADS + h))

    return pl.pallas_call(
        _mlstm_kernel,
        grid=(2, batch, N_HEADS, nc),
        in_specs=[sec(0), sec(1), sec(2),
                  pl.BlockSpec((CHUNK, LANES), lambda dd, b, h, c: (blk(dd, b, c), 0)),
                  pl.BlockSpec((N_GATES, CHUNK), lambda dd, b, h, c: (0, blk(dd, b, c))),
                  pl.BlockSpec((1, LANES), lambda dd, b, h, c: (0, 0)),
                  pl.BlockSpec((N_GATES, 1), lambda dd, b, h, c: (0, 0))],
        out_specs=pl.BlockSpec((None, CHUNK, HEAD_DIM),
                               lambda dd, b, h, c: (dd, blk(dd, b, c) - base, h)),
        out_shape=jax.ShapeDtypeStruct((2, batch * seq, D_A), F32),
        scratch_shapes=[pltpu.VMEM((HEAD_DIM, HEAD_DIM), F32),
                        pltpu.VMEM((1, HEAD_DIM), F32),
                        pltpu.VMEM((1, LANES), F32)],
        compiler_params=_cparams(4),
        name="mlstm",
    )(z, z, z, gates, gates_t, bias_row, bias_col)


def _pool_kernel(prev_ref, mid_ref, next_ref, pw_ref, sc_ref, o_ref, *, seq):
    i = pl.program_id(1)
    g = pl.program_id(2)
    half = jnp.left_shift(1, g)
    T = CHUNK
    p = i * T + lax.broadcasted_iota(I32, (T, T), 0)
    lo = jnp.maximum(p - half, 0)
    hi = jnp.minimum(p + half, seq)
    inv = 1.0 / (hi - lo).astype(F32)
    acc = jnp.zeros((T, POOL_GROUP_DIM), F32)
    for off, ref in ((-T, prev_ref), (0, mid_ref), (T, next_ref)):
        pj = i * T + off + lax.broadcasted_iota(I32, (T, T), 1)
        inside = (pj >= lo) & (pj < hi)
        a = jnp.where(inside, inv, 0.0)
        if off == 0:
            a = a - jnp.where(pj == p, 1.0, 0.0)
        acc = acc + jnp.dot(a, ref[...], precision=HI, preferred_element_type=F32)
    y = jnp.dot(acc.astype(BF16), pw_ref[...], preferred_element_type=F32)
    o_ref[...] = y * sc_ref[...]


def pool(z, pool_w, pool_scale, col0, base_tok, batch, seq):
    nt = seq // CHUNK
    base = base_tok // CHUNK
    cb = col0 // POOL_GROUP_DIM

    def spec(shift):
        return pl.BlockSpec(
            (CHUNK, POOL_GROUP_DIM),
            lambda b, i, g: (base + b * nt + jnp.clip(i + shift, 0, nt - 1), cb + g))

    return pl.pallas_call(
        functools.partial(_pool_kernel, seq=seq),
        grid=(batch, nt, N_POOL_GROUPS),
        in_specs=[spec(-1), spec(0), spec(1),
                  pl.BlockSpec((None, POOL_GROUP_DIM, POOL_GROUP_DIM), lambda b, i, g: (g, 0, 0)),
                  pl.BlockSpec((1, POOL_GROUP_DIM), lambda b, i, g: (0, g))],
        out_specs=pl.BlockSpec((CHUNK, POOL_GROUP_DIM), lambda b, i, g: (b * nt + i, g)),
        out_shape=jax.ShapeDtypeStruct((batch * seq, D_B), F32),
        compiler_params=_cparams(3),
        name="pool",
    )(z, z, z, pool_w, pool_scale.reshape(1, D_B))


def _outproj_kernel(hf_ref, hb_ref, o_ref, yb_ref, wa_ref, wb_ref, x_ref, out_ref):
    ha = (hf_ref[...] + hb_ref[...]) * jax.nn.sigmoid(o_ref[...])
    acc = jnp.dot(ha.astype(BF16), wa_ref[...], preferred_element_type=F32)
    acc = acc + jnp.dot(yb_ref[...].astype(BF16), wb_ref[...], preferred_element_type=F32)
    out_ref[...] = acc + x_ref[...]


def outproj(h2, z, yb, w_out, x, tm=512, tn=512):
    t = x.shape[0]
    tm = min(tm, t)
    ocb = 3 * D_A // D_A
    return pl.pallas_call(
        _outproj_kernel,
        grid=(t // tm, D_MODEL // tn),
        in_specs=[pl.BlockSpec((None, tm, D_A), lambda i, j: (0, i, 0)),
                  pl.BlockSpec((None, tm, D_A), lambda i, j: (1, i, 0)),
                  pl.BlockSpec((tm, D_A), lambda i, j: (i, ocb)),
                  pl.BlockSpec((tm, D_B), lambda i, j: (i, 0)),
                  pl.BlockSpec((D_A, tn), lambda i, j: (0, j)),
                  pl.BlockSpec((D_B, tn), lambda i, j: (D_A // D_B, j)),
                  pl.BlockSpec((tm, tn), lambda i, j: (i, j))],
        out_specs=pl.BlockSpec((tm, tn), lambda i, j: (i, j)),
        out_shape=jax.ShapeDtypeStruct((t, D_MODEL), F32),
        compiler_params=_cparams(2),
        name="outproj",
    )(h2, h2, z, yb, w_out, w_out, x)


def _dft_kernel(c_ref, s_ref, *, n):
    tr = c_ref.shape[0]
    r = pl.program_id(0) * tr + lax.broadcasted_iota(I32, (tr, n), 0)
    c = lax.broadcasted_iota(I32, (tr, n), 1)
    ang = ((r * c) & (n - 1)).astype(F32) * (2.0 * math.pi / n)
    scale = n ** -0.5
    c_ref[...] = (jnp.cos(ang) * scale).astype(BF16)
    s_ref[...] = (jnp.sin(ang) * scale).astype(BF16)


def dft_tables(n, tr=256):
    assert n & (n - 1) == 0
    blk = pl.BlockSpec((tr, n), lambda i: (i, 0))
    return pl.pallas_call(
        functools.partial(_dft_kernel, n=n),
        grid=(n // tr,),
        out_specs=[blk, blk],
        out_shape=[jax.ShapeDtypeStruct((n, n), BF16)] * 2,
        compiler_params=_cparams(1),
        name="dft_tables",
    )()


def _fourier_ch_kernel(x_ref, cd_ref, sd_ref, yc_ref, ys_ref):
    xb = x_ref[...].astype(BF16)
    yc_ref[...] = jnp.dot(xb, cd_ref[...], preferred_element_type=F32).astype(BF16)
    ys_ref[...] = jnp.dot(xb, sd_ref[...], preferred_element_type=F32).astype(BF16)


def fourier_channels(xn, cd, sd, tm=512):
    t = xn.shape[0]
    tm = min(tm, t)
    gd = FOURIER_GROUP_DIM
    tile = pl.BlockSpec((tm, gd), lambda i, g: (i, g))
    tab = pl.BlockSpec((gd, gd), lambda i, g: (0, 0))
    return pl.pallas_call(
        _fourier_ch_kernel,
        grid=(t // tm, N_FOURIER_GROUPS),
        in_specs=[tile, tab, tab],
        out_specs=[tile, tile],
        out_shape=[jax.ShapeDtypeStruct((t, D_MODEL), BF16)] * 2,
        compiler_params=_cparams(2),
        name="fourier_channels",
    )(xn, cd, sd)


def _fourier_seq_kernel(cs_ref, ss_ref, yc_ref, ys_ref, o_ref, acc_ref):
    kk = pl.program_id(3)

    @pl.when(kk == 0)
    def _():
        acc_ref[...] = jnp.zeros_like(acc_ref)

    acc_ref[...] += (jnp.dot(cs_ref[...], yc_ref[...], preferred_element_type=F32)
                     - jnp.dot(ss_ref[...], ys_ref[...], preferred_element_type=F32))

    @pl.when(kk == pl.num_programs(3) - 1)
    def _():
        o_ref[...] = acc_ref[...].astype(o_ref.dtype)


def fourier_seq(cs, ss, yc, ys, base_tok, batch, seq, tm=1024, tn=1024, tk=512):
    nk = seq // tk
    base = base_tok // tk
    a_spec = pl.BlockSpec((tm, tk), lambda b, i, j, k: (i, k))
    y_spec = pl.BlockSpec((tk, tn), lambda b, i, j, k: (base + b * nk + k, j))
    return pl.pallas_call(
        _fourier_seq_kernel,
        grid=(batch, seq // tm, D_MODEL // tn, nk),
        in_specs=[a_spec, a_spec, y_spec, y_spec],
        out_specs=pl.BlockSpec((tm, tn), lambda b, i, j, k: (b * (seq // tm) + i, j)),
        out_shape=jax.ShapeDtypeStruct((batch * seq, D_MODEL), BF16),
        scratch_shapes=[pltpu.VMEM((tm, tn), F32)],
        compiler_params=_cparams(4),
        name="fourier_seq",
    )(cs, ss, yc, ys)


def _top16_rows(s, row_key):
    n = s.shape[1]
    out_iota = lax.broadcasted_iota(I32, (PEER_TOPK, n), 0)
    vals = jnp.zeros((PEER_TOPK, n), F32)
    keys = jnp.zeros((PEER_TOPK, n), I32)
    big = jnp.int32(1 << 30)
    for rnk in range(PEER_TOPK):
        mx = jnp.max(s, axis=0, keepdims=True)
        kmin = jnp.min(jnp.where(s == mx, row_key, big), axis=0, keepdims=True)
        vals = jnp.where(out_iota == rnk, mx, vals)
        keys = jnp.where(out_iota == rnk, kmin, keys)
        s = jnp.where(row_key == kmin, -jnp.inf, s)
    return vals, keys


def _peer_topk_kernel(q_ref, k1_ref, k2_ref, e_ref, g_ref):
    tt = q_ref.shape[0]
    k1 = k1_ref[...]
    k2 = k2_ref[...]
    key_iota = lax.broadcasted_iota(I32, (PEER_NKEYS, tt), 0)
    nt = (((1,), (1,)), ((), ()))
    blocks = [(a, 16 if a == 0 else 8) for a in range(8)]
    for h in range(PEER_HEADS):
        q1 = q_ref[:, h * 2 * PEER_HALF: h * 2 * PEER_HALF + PEER_HALF]
        q2 = q_ref[:, h * 2 * PEER_HALF + PEER_HALF: (h + 1) * 2 * PEER_HALF]
        s1 = lax.dot_general(k1, q1, nt, precision=HI, preferred_element_type=F32)
        s2 = lax.dot_general(k2, q2, nt, precision=HI, preferred_element_type=F32)
        v1, i1 = _top16_rows(s1, key_iota)
        v2, i2 = _top16_rows(s2, key_iota)
        cands, ids, orders = [], [], []
        for a, rows in blocks:
            bi = lax.broadcasted_iota(I32, (rows, tt), 0)
            ok = (bi + 1) * (a + 1) <= PEER_TOPK
            cands.append(jnp.where(ok, v1[a:a + 1, :] + v2[0:rows, :], -jnp.inf))
            ids.append(i1[a:a + 1, :] * PEER_NKEYS + i2[0:rows, :])
            orders.append(a * PEER_TOPK + bi)
        ai = lax.broadcasted_iota(I32, (8, tt), 0) + 8
        cands.append(v1[8:16, :] + v2[0:1, :])
        ids.append(i1[8:16, :] * PEER_NKEYS + i2[0:1, :])
        orders.append(ai * PEER_TOPK)
        cand = jnp.concatenate(cands, axis=0)
        cid = jnp.concatenate(ids, axis=0)
        order = jnp.concatenate(orders, axis=0)
        ts, tpos = _top16_rows(cand, order)
        experts = jnp.zeros((PEER_TOPK, tt), I32)
        out_iota = lax.broadcasted_iota(I32, (PEER_TOPK, tt), 0)
        for rnk in range(PEER_TOPK):
            sel = jnp.max(jnp.where(order == tpos[rnk:rnk + 1, :], cid, -1), axis=0, keepdims=True)
            experts = jnp.where(out_iota == rnk, sel, experts)
        ex = jnp.exp(ts - ts[0:1, :])
        gate = ex / jnp.sum(ex, axis=0, keepdims=True)
        e_ref[h * PEER_TOPK:(h + 1) * PEER_TOPK, :] = experts
        g_ref[h * PEER_TOPK:(h + 1) * PEER_TOPK, :] = gate


def peer_topk(q, k1, k2, tt=256):
    t = q.shape[0]
    tt = min(tt, t)
    ne = PEER_HEADS * PEER_TOPK
    keys = pl.BlockSpec((PEER_NKEYS, PEER_HALF), lambda i: (0, 0))
    out = pl.BlockSpec((ne, tt), lambda i: (0, i))
    return pl.pallas_call(
        _peer_topk_kernel,
        grid=(t // tt,),
        in_specs=[pl.BlockSpec((tt, q.shape[1]), lambda i: (i, 0)), keys, keys],
        out_specs=[out, out],
        out_shape=[jax.ShapeDtypeStruct((ne, t), I32), jax.ShapeDtypeStruct((ne, t), F32)],
        compiler_params=_cparams(1),
        name="peer_topk",
    )(q, k1, k2)


def _gelu_exact(x):
    return 0.5 * x * (1.0 + lax.erf(x * (2.0 ** -0.5)))


def _peer_mix_kernel(lo_ref, hi_ref, win_ref, sb_ref, ch_ref, first_ref,
                     key_ref, gate_ref, xn_ref, u_ref, v_ref, out_ref,
                     part_ref, coef_ref, *, tb, ec, window):
    i = pl.program_id(0)
    tb_bits = tb.bit_length() - 1

    @pl.when(i == 0)
    def _():
        part_ref[...] = jnp.zeros_like(part_ref)

    @pl.when(first_ref[i] == 1)
    def _():
        out_ref[...] = jnp.zeros_like(out_ref)

    base = win_ref[i] * window
    p_lo = lo_ref[i] - base
    p_hi = hi_ref[i] - base
    U = PEER_UNROLL
    g_lo = p_lo // U
    g_hi = (p_hi + U - 1) // U

    def ids(p):
        key = key_ref[0, 0, p]
        return key & (tb - 1), (key >> tb_bits) & (ec - 1)

    def dots(g, carry):
        for j in range(U):
            p = g * U + j
            t, e = ids(p)
            prod = xn_ref[t] * u_ref[e]
            part_ref[pl.ds(pl.multiple_of(p * SUBLANES, SUBLANES), SUBLANES), :] = (
                prod[:SUBLANES] + prod[SUBLANES:])
        return carry

    lax.fori_loop(g_lo, g_hi, dots, 0)

    ones = jnp.ones((LANES, LANES), BF16)

    def reduce_rows(cb, carry):
        r0 = pl.multiple_of(cb * LANES, LANES)
        a = part_ref[pl.ds(r0 * SUBLANES, LANES, stride=SUBLANES), :]
        for s_ in range(1, SUBLANES):
            a = a + part_ref[pl.ds(r0 * SUBLANES + s_, LANES, stride=SUBLANES), :]
        a_hi = a.astype(BF16)
        a_lo = (a - a_hi.astype(F32)).astype(BF16)
        a_lo2 = (a - a_hi.astype(F32) - a_lo.astype(F32)).astype(BF16)
        act = (jnp.dot(a_hi, ones, preferred_element_type=F32)
               + jnp.dot(a_lo, ones, preferred_element_type=F32)
               + jnp.dot(a_lo2, ones, preferred_element_type=F32))
        coef_ref[pl.ds(r0, LANES), :] = _gelu_exact(act)
        return carry

    lax.fori_loop(p_lo // LANES, (p_hi + LANES - 1) // LANES, reduce_rows, 0)

    def mix(g, carry):
        for j in range(U):
            p = g * U + j
            t, e = ids(p)
            valid = (p >= p_lo) & (p < p_hi)
            gt = jnp.where(valid, gate_ref[0, 0, p], 0.0)
            cf = coef_ref[pl.ds(p, 1), :] * gt
            out_ref[t] = out_ref[t] + cf * v_ref[e]
        return carry

    lax.fori_loop(g_lo, g_hi, mix, 0)


def peer_mix(xn, expert_t, gate_t, u_tab, v_tab):
    t = xn.shape[0]
    tb = min(PEER_TOKEN_BLOCK, t)
    ec = PEER_EXPERT_CHUNK
    window = PEER_WINDOW
    ne = expert_t.shape[0]
    nsb = t // tb
    nch = N_EXPERTS // ec
    nbins = nsb * nch
    npairs = ne * t
    nwin = npairs // window
    tb_bits = tb.bit_length() - 1

    tok = lax.broadcasted_iota(I32, (ne, t), 1)
    key = (((tok // tb) * N_EXPERTS + expert_t) << tb_bits) | (tok % tb)
    skey, sgate = lax.sort((key.reshape(-1), gate_t.reshape(-1)), num_keys=1)
    bin_shift = tb_bits + (ec.bit_length() - 1)
    bin_start = jnp.searchsorted(skey, jnp.arange(nbins, dtype=I32) << bin_shift).astype(I32)
    bp = jnp.sort(jnp.concatenate([jnp.arange(nwin, dtype=I32) * window, bin_start]))
    lo = bp
    hi = jnp.concatenate([bp[1:], jnp.full((1,), npairs, I32)])
    win = jnp.minimum(lo // window, nwin - 1)
    sbin = jnp.clip(jnp.searchsorted(bin_start, lo, side="right").astype(I32) - 1, 0, nbins - 1)
    sb = sbin // nch
    ch = sbin % nch
    first = jnp.concatenate([jnp.ones((1,), I32), (sb[1:] != sb[:-1]).astype(I32)])
    nsteps = nwin + nbins

    rows = lambda a: a.reshape(a.shape[0], ROW_TILES, LANES)
    meta = pl.BlockSpec((1, 1, window), lambda i, lo, hi, win, sb, ch, fs: (win[i], 0, 0),
                        memory_space=pltpu.SMEM)
    tok_rows = pl.BlockSpec((tb, ROW_TILES, LANES), lambda i, lo, hi, win, sb, ch, fs: (sb[i], 0, 0))
    tab_rows = pl.BlockSpec((ec, ROW_TILES, LANES), lambda i, lo, hi, win, sb, ch, fs: (ch[i], 0, 0))
    out = pl.pallas_call(
        functools.partial(_peer_mix_kernel, tb=tb, ec=ec, window=window),
        grid_spec=pltpu.PrefetchScalarGridSpec(
            num_scalar_prefetch=6,
            grid=(nsteps,),
            in_specs=[meta, meta, tok_rows, tab_rows, tab_rows],
            out_specs=tok_rows,
            scratch_shapes=[pltpu.VMEM((window * SUBLANES, LANES), F32),
                            pltpu.VMEM((window, LANES), F32)]),
        out_shape=jax.ShapeDtypeStruct((t, ROW_TILES, LANES), F32),
        compiler_params=_cparams(1),
        name="peer_mix",
    )(lo, hi, win, sb, ch, first,
      skey.reshape(nwin, 1, window), sgate.reshape(nwin, 1, window),
      rows(xn), rows(u_tab), rows(v_tab))
    return out.reshape(t, D_MODEL)


def peer_layer(xn, wq, k1, k2, u_tab, v_tab):
    q = matmul(xn, wq.astype(BF16), name="peer_query")
    expert_t, gate_t = peer_topk(q, k1, k2)
    return peer_mix(xn, expert_t, gate_t, u_tab, v_tab)


def _trunk(x, parts, even_mix_norm, even_w_in, even_gate_bias, even_pool_w, even_pool_scale,
           even_w_out, odd_mix_norm, odd_fourier_w, ffn_norm, peer_wq, peer_k1, peer_k2,
           peer_u, peer_v, final_norm):
    cat = lambda xs: xs[0] if len(xs) == 1 else jnp.concatenate(xs, axis=0)

    xn = rmsnorm(x, even_mix_norm[0])
    n_main = 4 * D_A + D_B
    w_in = even_w_in[0]
    z = matmul(xn, w_in[:, :n_main].astype(BF16), name="in_proj")
    w_gate = jnp.pad(w_in[:, n_main:], ((0, 0), (0, LANES - N_GATES)))
    gates = matmul_hi(xn, w_gate)
    gates_t = gates[:, :N_GATES].T
    bias = even_gate_bias[0].astype(F32)
    bias_row = jnp.pad(bias, (0, LANES - N_GATES)).reshape(1, LANES)
    bias_col = bias.reshape(N_GATES, 1)
    pw = even_pool_w[0].astype(BF16)
    hs, ybs = [], []
    for base, batch, seq in parts:
        hs.append(mlstm(z, gates, gates_t, bias_row, bias_col, base, batch, seq))
        ybs.append(pool(z, pw, even_pool_scale[0], 4 * D_A, base, batch, seq))
    h2 = hs[0] if len(hs) == 1 else jnp.concatenate(hs, axis=1)
    x = outproj(h2, z, cat(ybs), even_w_out[0].astype(BF16), x)

    xn = rmsnorm(x, ffn_norm[0])
    y = peer_layer(xn, peer_wq[0], peer_k1[0], peer_k2[0], peer_u[0], peer_v[0])

    x, xn = add_rmsnorm(x, y, odd_mix_norm[0])
    cd, sd = dft_tables(FOURIER_GROUP_DIM)
    yc, ys = fourier_channels(xn, cd, sd)
    fs = []
    for base, batch, seq in parts:
        cs, ss = dft_tables(seq)
        fs.append(fourier_seq(cs, ss, yc, ys, base, batch, seq,
                              tm=min(1024, seq), tk=min(512, seq)))
    x = matmul(cat(fs), odd_fourier_w[0].astype(BF16), res=x, name="fourier_out")

    xn = rmsnorm(x, ffn_norm[1])
    y = peer_layer(xn, peer_wq[1], peer_k1[1], peer_k2[1], peer_u[1], peer_v[1])
    _, out = add_rmsnorm(x, y, final_norm)
    return out


def kernel(x_prompt, x_sample, even_mix_norm, even_w_in, even_gate_bias, even_pool_w,
           even_pool_scale, even_w_out, odd_mix_norm, odd_fourier_w, ffn_norm, peer_wq,
           peer_k1, peer_k2, peer_u, peer_v, final_norm):
    bp, sp, d = x_prompt.shape
    bs, ss, _ = x_sample.shape
    tp = bp * sp
    x = jnp.concatenate([x_prompt.reshape(tp, d), x_sample.reshape(bs * ss, d)], axis=0)
    parts = ((0, bp, sp), (tp, bs, ss))
    y = _trunk(x, parts, even_mix_norm, even_w_in, even_gate_bias, even_pool_w,
               even_pool_scale, even_w_out, odd_mix_norm, odd_fourier_w, ffn_norm,
               peer_wq, peer_k1, peer_k2, peer_u, peer_v, final_norm)
    return y[:tp].reshape(bp, sp, d), y[tp:].reshape(bs, ss, d)
```

```python
import functools
import math

import jax
import jax.numpy as jnp
from jax import lax
from jax.experimental import pallas as pl
from jax.experimental.pallas import tpu as pltpu

F32 = jnp.float32
BF16 = jnp.bfloat16
I32 = jnp.int32
HI = lax.Precision.HIGHEST

D_MODEL = 2048
EPS = 1e-6
N_HEADS = 4
HEAD_DIM = 256
D_A = N_HEADS * HEAD_DIM
D_B = D_MODEL - D_A
N_GATES = 4 * N_HEADS
CHUNK = 128
N_POOL_GROUPS = 4
POOL_GROUP_DIM = D_B // N_POOL_GROUPS
N_FOURIER_GROUPS = 4
FOURIER_GROUP_DIM = D_MODEL // N_FOURIER_GROUPS
PEER_HEADS = 8
PEER_NKEYS = 128
PEER_HALF = 128
PEER_TOPK = 16
N_EXPERTS = PEER_NKEYS * PEER_NKEYS

LANES = 128
SUBLANES = 8
ROW_TILES = D_MODEL // LANES
VMEM_LIMIT = 56 * 1024 * 1024

PEER_TOKEN_BLOCK = 1024
PEER_EXPERT_CHUNK = 256
PEER_WINDOW = 1024
PEER_UNROLL = 16
PEER_FLAG_PAD = 8


def _cparams(n_axes):
    return pltpu.CompilerParams(dimension_semantics=("arbitrary",) * n_axes,
                                vmem_limit_bytes=VMEM_LIMIT)


def _rms(x, g):
    ms = jnp.mean(x * x, axis=-1, keepdims=True)
    return (x * lax.rsqrt(ms + EPS)) * g


def _norm_kernel(a_ref, g_ref, xn_ref):
    xn_ref[...] = _rms(a_ref[...], g_ref[...])


def _addnorm_kernel(a_ref, b_ref, g_ref, x_ref, xn_ref):
    x = a_ref[...] + b_ref[...]
    x_ref[...] = x
    xn_ref[...] = _rms(x, g_ref[...])


def rmsnorm(a, g, tm=256):
    t, d = a.shape
    tm = min(tm, t)
    row = pl.BlockSpec((tm, d), lambda i: (i, 0))
    return pl.pallas_call(
        _norm_kernel,
        grid=(t // tm,),
        in_specs=[row, pl.BlockSpec((1, d), lambda i: (0, 0))],
        out_specs=row,
        out_shape=jax.ShapeDtypeStruct((t, d), F32),
        compiler_params=_cparams(1),
        name="rmsnorm",
    )(a, g.reshape(1, d))


def add_rmsnorm(a, b, g, tm=256):
    t, d = a.shape
    tm = min(tm, t)
    row = pl.BlockSpec((tm, d), lambda i: (i, 0))
    return pl.pallas_call(
        _addnorm_kernel,
        grid=(t // tm,),
        in_specs=[row, row, pl.BlockSpec((1, d), lambda i: (0, 0))],
        out_specs=[row, row],
        out_shape=[jax.ShapeDtypeStruct((t, d), F32)] * 2,
        compiler_params=_cparams(1),
        name="add_rmsnorm",
    )(a, b, g.reshape(1, d))


def _mm_kernel(a_ref, w_ref, o_ref):
    o_ref[...] = jnp.dot(a_ref[...].astype(BF16), w_ref[...],
                         preferred_element_type=F32).astype(o_ref.dtype)


def _mm_res_kernel(a_ref, w_ref, r_ref, o_ref):
    acc = jnp.dot(a_ref[...].astype(BF16), w_ref[...], preferred_element_type=F32)
    o_ref[...] = (acc + r_ref[...]).astype(o_ref.dtype)


def _mm_hi_kernel(a_ref, w_ref, o_ref):
    o_ref[...] = jnp.dot(a_ref[...], w_ref[...], precision=HI,
                         preferred_element_type=F32)


def matmul(a, w, res=None, tm=512, tn=512, out_dtype=F32, name="matmul"):
    m, k = a.shape
    n = w.shape[1]
    tm = min(tm, m)
    tn = min(tn, n)
    in_specs = [pl.BlockSpec((tm, k), lambda i, j: (i, 0)),
                pl.BlockSpec((k, tn), lambda i, j: (0, j))]
    args = [a, w]
    body = _mm_kernel
    if res is not None:
        in_specs.append(pl.BlockSpec((tm, tn), lambda i, j: (i, j)))
        args.append(res)
        body = _mm_res_kernel
    return pl.pallas_call(
        body,
        grid=(m // tm, n // tn),
        in_specs=in_specs,
        out_specs=pl.BlockSpec((tm, tn), lambda i, j: (i, j)),
        out_shape=jax.ShapeDtypeStruct((m, n), out_dtype),
        compiler_params=_cparams(2),
        name=name,
    )(*args)


def matmul_hi(a, w, tm=512):
    m, k = a.shape
    n = w.shape[1]
    tm = min(tm, m)
    return pl.pallas_call(
        _mm_hi_kernel,
        grid=(m // tm,),
        in_specs=[pl.BlockSpec((tm, k), lambda i: (i, 0)),
                  pl.BlockSpec((k, n), lambda i: (0, 0))],
        out_specs=pl.BlockSpec((tm, n), lambda i: (i, 0)),
        out_shape=jax.ShapeDtypeStruct((m, n), F32),
        compiler_params=_cparams(1),
        name="matmul_hi",
    )(a, w)


def _log_sigmoid(x):
    return jnp.minimum(x, 0.0) - jnp.log(1.0 + jnp.exp(-jnp.abs(x)))


def _mlstm_kernel(q_ref, k_ref, v_ref, gc_ref, gr_ref, bc_ref, br_ref, h_ref,
                  c_ref, n_ref, m_ref):
    d = pl.program_id(0)
    hh = pl.program_id(2)
    c = pl.program_id(3)
    L = CHUNK

    @pl.when(c == 0)
    def _():
        c_ref[...] = jnp.zeros_like(c_ref)
        n_ref[...] = jnp.zeros_like(n_ref)
        m_ref[...] = jnp.zeros_like(m_ref)

    q = q_ref[...] * (HEAD_DIM ** -0.5)
    k = k_ref[...]
    v = v_ref[...]
    gcol = gc_ref[...] + bc_ref[...]
    grow = gr_ref[...] + br_ref[...]
    li_idx = 2 * d * N_HEADS + hh
    lf_idx = li_idx + N_HEADS
    lane = lax.broadcasted_iota(I32, gcol.shape, 1)
    sub = lax.broadcasted_iota(I32, grow.shape, 0)
    li_col = jnp.sum(jnp.where(lane == li_idx, gcol, 0.0), axis=1, keepdims=True)
    gf_col = jnp.sum(jnp.where(lane == lf_idx, gcol, 0.0), axis=1, keepdims=True)
    li_row = jnp.sum(jnp.where(sub == li_idx, grow, 0.0), axis=0, keepdims=True)
    gf_row = jnp.sum(jnp.where(sub == lf_idx, grow, 0.0), axis=0, keepdims=True)
    lf_col = _log_sigmoid(gf_col)
    lf_row = _log_sigmoid(gf_row)

    r = lax.broadcasted_iota(I32, (L, L), 0)
    cc = lax.broadcasted_iota(I32, (L, L), 1)
    sign = 1 - 2 * d
    mask = (cc - r) * sign <= 0
    mask_t = (r - cc) * sign <= 0
    b_col = jnp.sum(jnp.where(mask, lf_row, 0.0), axis=1, keepdims=True)
    b_row = jnp.sum(jnp.where(mask_t, lf_col, 0.0), axis=0, keepdims=True)
    logw = jnp.where(mask, b_col - b_row + li_row, -jnp.inf)

    m_prev = m_ref[:, 0:1]
    inter = b_col + m_prev
    m_t = jnp.maximum(inter, jnp.max(logw, axis=1, keepdims=True))
    w = jnp.exp(logw - m_t)
    a_inter = jnp.exp(inter - m_t)
    s = lax.dot_general(q, k, (((1,), (1,)), ((), ())), precision=HI,
                        preferred_element_type=F32) * w
    cm = c_ref[...]
    num = (jnp.dot(s, v, precision=HI, preferred_element_type=F32)
           + a_inter * jnp.dot(q, cm, precision=HI, preferred_element_type=F32))
    den = (jnp.sum(s, axis=1, keepdims=True)
           + a_inter * jnp.sum(q * n_ref[...], axis=1, keepdims=True))
    h_ref[...] = num / jnp.maximum(jnp.abs(den), jnp.exp(-m_t))

    b_last = jnp.sum(lf_row, axis=1, keepdims=True)
    logu = b_last - b_col + li_col
    m_new = jnp.maximum(b_last + m_prev, jnp.max(logu, axis=0, keepdims=True))
    u = jnp.exp(logu - m_new)
    decay = jnp.exp(b_last + m_prev - m_new)
    uk = u * k
    c_ref[...] = decay * cm + lax.dot_general(
        uk, v, (((0,), (0,)), ((), ())), precision=HI, preferred_element_type=F32)
    n_ref[...] = decay * n_ref[...] + jnp.sum(uk, axis=0, keepdims=True)
    m_ref[...] = jnp.broadcast_to(m_new, m_ref.shape)


def mlstm(z, gates, gates_t, bias_row, bias_col, base_tok, batch, seq):
    nc = seq // CHUNK
    base = base_tok // CHUNK

    def blk(dd, b, c):
        return base + b * nc + c + dd * (nc - 1 - 2 * c)

    def sec(which):
        return pl.BlockSpec((CHUNK, HEAD_DIM),
                            lambda dd, b, h, c: (blk(dd, b, c), which * N_HEADS + h))

    return pl.pallas_call(
        _mlstm_kernel,
        grid=(2, batch, N_HEADS, nc),
        in_specs=[sec(0), sec(1), sec(2),
                  pl.BlockSpec((CHUNK, LANES), lambda dd, b, h, c: (blk(dd, b, c), 0)),
                  pl.BlockSpec((N_GATES, CHUNK), lambda dd, b, h, c: (0, blk(dd, b, c))),
                  pl.BlockSpec((1, LANES), lambda dd, b, h, c: (0, 0)),
                  pl.BlockSpec((N_GATES, 1), lambda dd, b, h, c: (0, 0))],
        out_specs=pl.BlockSpec((None, CHUNK, HEAD_DIM),
                               lambda dd, b, h, c: (dd, blk(dd, b, c) - base, h)),
        out_shape=jax.ShapeDtypeStruct((2, batch * seq, D_A), F32),
        scratch_shapes=[pltpu.VMEM((HEAD_DIM, HEAD_DIM), F32),
                        pltpu.VMEM((1, HEAD_DIM), F32),
                        pltpu.VMEM((1, LANES), F32)],
        compiler_params=_cparams(4),
        name="mlstm",
    )(z, z, z, gates, gates_t, bias_row, bias_col)


def _pool_kernel(prev_ref, mid_ref, next_ref, pw_ref, sc_ref, o_ref, *, seq):
    i = pl.program_id(1)
    g = pl.program_id(2)
    half = jnp.left_shift(1, g)
    T = CHUNK
    p = i * T + lax.broadcasted_iota(I32, (T, T), 0)
    lo = jnp.maximum(p - half, 0)
    hi = jnp.minimum(p + half, seq)
    inv = 1.0 / (hi - lo).astype(F32)
    acc = jnp.zeros((T, POOL_GROUP_DIM), F32)
    for off, ref in ((-T, prev_ref), (0, mid_ref), (T, next_ref)):
        pj = i * T + off + lax.broadcasted_iota(I32, (T, T), 1)
        inside = (pj >= lo) & (pj < hi)
        a = jnp.where(inside, inv, 0.0)
        if off == 0:
            a = a - jnp.where(pj == p, 1.0, 0.0)
        acc = acc + jnp.dot(a, ref[...], precision=HI, preferred_element_type=F32)
    y = jnp.dot(acc.astype(BF16), pw_ref[...], preferred_element_type=F32)
    o_ref[...] = y * sc_ref[...]


def pool(z, pool_w, pool_scale, col0, base_tok, batch, seq):
    nt = seq // CHUNK
    base = base_tok // CHUNK
    cb = col0 // POOL_GROUP_DIM

    def spec(shift):
        return pl.BlockSpec(
            (CHUNK, POOL_GROUP_DIM),
            lambda b, i, g: (base + b * nt + jnp.clip(i + shift, 0, nt - 1), cb + g))

    return pl.pallas_call(
        functools.partial(_pool_kernel, seq=seq),
        grid=(batch, nt, N_POOL_GROUPS),
        in_specs=[spec(-1), spec(0), spec(1),
                  pl.BlockSpec((None, POOL_GROUP_DIM, POOL_GROUP_DIM), lambda b, i, g: (g, 0, 0)),
                  pl.BlockSpec((1, POOL_GROUP_DIM), lambda b, i, g: (0, g))],
        out_specs=pl.BlockSpec((CHUNK, POOL_GROUP_DIM), lambda b, i, g: (b * nt + i, g)),
        out_shape=jax.ShapeDtypeStruct((batch * seq, D_B), F32),
        compiler_params=_cparams(3),
        name="pool",
    )(z, z, z, pool_w, pool_scale.reshape(1, D_B))


def _outproj_kernel(hf_ref, hb_ref, o_ref, yb_ref, wa_ref, wb_ref, x_ref, out_ref):
    ha = (hf_ref[...] + hb_ref[...]) * jax.nn.sigmoid(o_ref[...])
    acc = jnp.dot(ha.astype(BF16), wa_ref[...], preferred_element_type=F32)
    acc = acc + jnp.dot(yb_ref[...].astype(BF16), wb_ref[...], preferred_element_type=F32)
    out_ref[...] = acc + x_ref[...]


def outproj(h2, z, yb, w_out, x, tm=512, tn=512):
    t = x.shape[0]
    tm = min(tm, t)
    ocb = 3 * D_A // D_A
    return pl.pallas_call(
        _outproj_kernel,
        grid=(t // tm, D_MODEL // tn),
        in_specs=[pl.BlockSpec((None, tm, D_A), lambda i, j: (0, i, 0)),
                  pl.BlockSpec((None, tm, D_A), lambda i, j: (1, i, 0)),
                  pl.BlockSpec((tm, D_A), lambda i, j: (i, ocb)),
                  pl.BlockSpec((tm, D_B), lambda i, j: (i, 0)),
                  pl.BlockSpec((D_A, tn), lambda i, j: (0, j)),
                  pl.BlockSpec((D_B, tn), lambda i, j: (D_A // D_B, j)),
                  pl.BlockSpec((tm, tn), lambda i, j: (i, j))],
        out_specs=pl.BlockSpec((tm, tn), lambda i, j: (i, j)),
        out_shape=jax.ShapeDtypeStruct((t, D_MODEL), F32),
        compiler_params=_cparams(2),
        name="outproj",
    )(h2, h2, z, yb, w_out, w_out, x)


def _dft_kernel(c_ref, s_ref, *, n):
    tr = c_ref.shape[0]
    r = pl.program_id(0) * tr + lax.broadcasted_iota(I32, (tr, n), 0)
    c = lax.broadcasted_iota(I32, (tr, n), 1)
    ang = ((r * c) & (n - 1)).astype(F32) * (2.0 * math.pi / n)
    scale = n ** -0.5
    c_ref[...] = (jnp.cos(ang) * scale).astype(BF16)
    s_ref[...] = (jnp.sin(ang) * scale).astype(BF16)


def dft_tables(n, tr=256):
    assert n & (n - 1) == 0
    blk = pl.BlockSpec((tr, n), lambda i: (i, 0))
    return pl.pallas_call(
        functools.partial(_dft_kernel, n=n),
        grid=(n // tr,),
        out_specs=[blk, blk],
        out_shape=[jax.ShapeDtypeStruct((n, n), BF16)] * 2,
        compiler_params=_cparams(1),
        name="dft_tables",
    )()


def _fourier_ch_kernel(x_ref, cd_ref, sd_ref, yc_ref, ys_ref):
    xb = x_ref[...].astype(BF16)
    yc_ref[...] = jnp.dot(xb, cd_ref[...], preferred_element_type=F32).astype(BF16)
    ys_ref[...] = jnp.dot(xb, sd_ref[...], preferred_element_type=F32).astype(BF16)


def fourier_channels(xn, cd, sd, tm=512):
    t = xn.shape[0]
    tm = min(tm, t)
    gd = FOURIER_GROUP_DIM
    tile = pl.BlockSpec((tm, gd), lambda i, g: (i, g))
    tab = pl.BlockSpec((gd, gd), lambda i, g: (0, 0))
    return pl.pallas_call(
        _fourier_ch_kernel,
        grid=(t // tm, N_FOURIER_GROUPS),
        in_specs=[tile, tab, tab],
        out_specs=[tile, tile],
        out_shape=[jax.ShapeDtypeStruct((t, D_MODEL), BF16)] * 2,
        compiler_params=_cparams(2),
        name="fourier_channels",
    )(xn, cd, sd)


def _fourier_seq_kernel(cs_ref, ss_ref, yc_ref, ys_ref, o_ref, acc_ref):
    kk = pl.program_id(3)

    @pl.when(kk == 0)
    def _():
        acc_ref[...] = jnp.zeros_like(acc_ref)

    acc_ref[...] += (jnp.dot(cs_ref[...], yc_ref[...], preferred_element_type=F32)
                     - jnp.dot(ss_ref[...], ys_ref[...], preferred_element_type=F32))

    @pl.when(kk == pl.num_programs(3) - 1)
    def _():
        o_ref[...] = acc_ref[...].astype(o_ref.dtype)


def fourier_seq(cs, ss, yc, ys, base_tok, batch, seq, tm=1024, tn=1024, tk=512):
    nk = seq // tk
    base = base_tok // tk
    a_spec = pl.BlockSpec((tm, tk), lambda b, i, j, k: (i, k))
    y_spec = pl.BlockSpec((tk, tn), lambda b, i, j, k: (base + b * nk + k, j))
    return pl.pallas_call(
        _fourier_seq_kernel,
        grid=(batch, seq // tm, D_MODEL // tn, nk),
        in_specs=[a_spec, a_spec, y_spec, y_spec],
        out_specs=pl.BlockSpec((tm, tn), lambda b, i, j, k: (b * (seq // tm) + i, j)),
        out_shape=jax.ShapeDtypeStruct((batch * seq, D_MODEL), BF16),
        scratch_shapes=[pltpu.VMEM((tm, tn), F32)],
        compiler_params=_cparams(4),
        name="fourier_seq",
    )(cs, ss, yc, ys)


def _top16_rows(s, row_key):
    n = s.shape[1]
    out_iota = lax.broadcasted_iota(I32, (PEER_TOPK, n), 0)
    vals = jnp.zeros((PEER_TOPK, n), F32)
    keys = jnp.zeros((PEER_TOPK, n), I32)
    big = jnp.int32(1 << 30)
    for rnk in range(PEER_TOPK):
        mx = jnp.max(s, axis=0, keepdims=True)
        kmin = jnp.min(jnp.where(s == mx, row_key, big), axis=0, keepdims=True)
        vals = jnp.where(out_iota == rnk, mx, vals)
        keys = jnp.where(out_iota == rnk, kmin, keys)
        s = jnp.where(row_key == kmin, -jnp.inf, s)
    return vals, keys


def _peer_topk_kernel(q_ref, k1_ref, k2_ref, e_ref, g_ref):
    tt = q_ref.shape[0]
    k1 = k1_ref[...]
    k2 = k2_ref[...]
    key_iota = lax.broadcasted_iota(I32, (PEER_NKEYS, tt), 0)
    nt = (((1,), (1,)), ((), ()))
    blocks = [(a, 16 if a == 0 else 8) for a in range(8)]
    for h in range(PEER_HEADS):
        q1 = q_ref[:, h * 2 * PEER_HALF: h * 2 * PEER_HALF + PEER_HALF]
        q2 = q_ref[:, h * 2 * PEER_HALF + PEER_HALF: (h + 1) * 2 * PEER_HALF]
        s1 = lax.dot_general(k1, q1, nt, precision=HI, preferred_element_type=F32)
        s2 = lax.dot_general(k2, q2, nt, precision=HI, preferred_element_type=F32)
        v1, i1 = _top16_rows(s1, key_iota)
        v2, i2 = _top16_rows(s2, key_iota)
        cands, ids, orders = [], [], []
        for a, rows in blocks:
            bi = lax.broadcasted_iota(I32, (rows, tt), 0)
            ok = (bi + 1) * (a + 1) <= PEER_TOPK
            cands.append(jnp.where(ok, v1[a:a + 1, :] + v2[0:rows, :], -jnp.inf))
            ids.append(i1[a:a + 1, :] * PEER_NKEYS + i2[0:rows, :])
            orders.append(a * PEER_TOPK + bi)
        ai = lax.broadcasted_iota(I32, (8, tt), 0) + 8
        cands.append(v1[8:16, :] + v2[0:1, :])
        ids.append(i1[8:16, :] * PEER_NKEYS + i2[0:1, :])
        orders.append(ai * PEER_TOPK)
        cand = jnp.concatenate(cands, axis=0)
        cid = jnp.concatenate(ids, axis=0)
        order = jnp.concatenate(orders, axis=0)
        ts, tpos = _top16_rows(cand, order)
        experts = jnp.zeros((PEER_TOPK, tt), I32)
        out_iota = lax.broadcasted_iota(I32, (PEER_TOPK, tt), 0)
        for rnk in range(PEER_TOPK):
            sel = jnp.max(jnp.where(order == tpos[rnk:rnk + 1, :], cid, -1), axis=0, keepdims=True)
            experts = jnp.where(out_iota == rnk, sel, experts)
        ex = jnp.exp(ts - ts[0:1, :])
        gate = ex / jnp.sum(ex, axis=0, keepdims=True)
        e_ref[h * PEER_TOPK:(h + 1) * PEER_TOPK, :] = experts
        g_ref[h * PEER_TOPK:(h + 1) * PEER_TOPK, :] = gate


def peer_topk(q, k1, k2, tt=256):
    t = q.shape[0]
    tt = min(tt, t)
    ne = PEER_HEADS * PEER_TOPK
    keys = pl.BlockSpec((PEER_NKEYS, PEER_HALF), lambda i: (0, 0))
    out = pl.BlockSpec((ne, tt), lambda i: (0, i))
    return pl.pallas_call(
        _peer_topk_kernel,
        grid=(t // tt,),
        in_specs=[pl.BlockSpec((tt, q.shape[1]), lambda i: (i, 0)), keys, keys],
        out_specs=[out, out],
        out_shape=[jax.ShapeDtypeStruct((ne, t), I32), jax.ShapeDtypeStruct((ne, t), F32)],
        compiler_params=_cparams(1),
        name="peer_topk",
    )(q, k1, k2)


def _gelu_exact(x):
    return 0.5 * x * (1.0 + lax.erf(x * (2.0 ** -0.5)))


def _peer_mix_kernel(lo_ref, hi_ref, win_ref, sb_ref, ch_ref, first_ref,
                     trow_ref, erow_ref, flag_ref, gate_ref, xn_ref, u_ref, v_ref, out_ref,
                     part_ref, coef_ref, *, window):
    i = pl.program_id(0)
    U = PEER_UNROLL
    RT = ROW_TILES

    @pl.when(i == 0)
    def _():
        part_ref[...] = jnp.zeros_like(part_ref)

    @pl.when(first_ref[i] == 1)
    def _():
        out_ref[...] = jnp.zeros_like(out_ref)

    base = win_ref[i] * window
    p_lo = lo_ref[i] - base
    p_hi = hi_ref[i] - base

    def rows(ref, r):
        return ref[pl.ds(pl.multiple_of(r, RT), RT), :]

    sub = lax.broadcasted_iota(I32, (SUBLANES, LANES), 0)
    low4 = sub < 4
    low2 = (sub & 3) < 2
    even = (sub & 1) == 0

    def rot(x, k):
        return pltpu.roll(x, k, 0)

    def dots8(g):
        p = []
        for j in range(SUBLANES):
            q = g * SUBLANES + j
            prod = rows(xn_ref, trow_ref[0, 0, q]) * rows(u_ref, erow_ref[0, 0, q])
            p.append(prod[:SUBLANES] + prod[SUBLANES:])
        q4 = [jnp.where(low4, p[j] + rot(p[j], 4), p[j + 4] + rot(p[j + 4], 4)) for j in range(4)]
        r2 = [jnp.where(low2, q4[j] + rot(q4[j], 6), q4[j + 2] + rot(q4[j + 2], 2)) for j in range(2)]
        part_ref[pl.ds(pl.multiple_of(g * SUBLANES, SUBLANES), SUBLANES), :] = jnp.where(
            even, r2[0] + rot(r2[0], 7), r2[1] + rot(r2[1], 1))

    def dots(g2, carry):
        dots8(2 * g2)
        dots8(2 * g2 + 1)
        return carry

    lax.fori_loop(p_lo // (2 * SUBLANES), (p_hi + 2 * SUBLANES - 1) // (2 * SUBLANES), dots, 0)

    ones_sq = jnp.ones((LANES, LANES), BF16)
    eye = (lax.broadcasted_iota(I32, (LANES, LANES), 0)
           == lax.broadcasted_iota(I32, (LANES, LANES), 1))

    def reduce_chunk(cb):
        r0 = pl.multiple_of(cb * LANES, LANES)
        tot = jnp.dot(part_ref[pl.ds(r0, LANES), :].astype(BF16), ones_sq,
                      preferred_element_type=F32)
        gdiag = jnp.where(eye, gate_ref[0, pl.ds(cb, 1), :], 0.0).astype(BF16)
        gcol = jnp.dot(gdiag, ones_sq, preferred_element_type=F32)
        coef_ref[pl.ds(r0, LANES), :] = _gelu_exact(tot) * gcol

    def reduce_rows(k, carry):
        reduce_chunk(2 * k)
        reduce_chunk(2 * k + 1)
        return carry

    c_lo = p_lo // LANES
    c_hi = (p_hi + LANES - 1) // LANES
    lax.fori_loop(c_lo // 2, (c_hi + 1) // 2, reduce_rows, 0)

    def contrib(p):
        return coef_ref[pl.ds(p, 1), :] * rows(v_ref, erow_ref[0, 0, p])

    def mix_one(p, carry):
        tr = pl.multiple_of(trow_ref[0, 0, p], RT)
        out_ref[pl.ds(tr, RT), :] = out_ref[pl.ds(tr, RT), :] + contrib(p)
        return carry

    a_lo = jnp.minimum((p_lo + U - 1) // U * U, p_hi)
    a_hi = jnp.maximum(p_hi // U * U, a_lo)
    lax.fori_loop(p_lo, a_lo, mix_one, 0)

    def mix_group(g, flag):
        nxt = flag_ref[0, 0, g + 1]

        def fast():
            trs = [pl.multiple_of(trow_ref[0, 0, g * U + j], RT) for j in range(U)]
            new = [out_ref[pl.ds(trs[j], RT), :] + contrib(g * U + j) for j in range(U)]
            for j in range(U):
                out_ref[pl.ds(trs[j], RT), :] = new[j]

        def slow():
            for j in range(U):
                mix_one(g * U + j, 0)

        lax.cond(flag == 0, fast, slow)
        return nxt

    g0 = a_lo // U
    lax.fori_loop(g0, a_hi // U, mix_group, flag_ref[0, 0, g0])
    lax.fori_loop(a_hi, p_hi, mix_one, 0)


def peer_route(expert_t, gate_t, tb, ec, window):
    ne, t = expert_t.shape
    nsb = t // tb
    nch = N_EXPERTS // ec
    nbins = nsb * nch
    npairs = ne * t
    nwin = npairs // window
    tb_bits = tb.bit_length() - 1
    tok = lax.broadcasted_iota(I32, (ne, t), 1)
    key = (((tok // tb) * N_EXPERTS + expert_t) << tb_bits) | (tok % tb)
    skey, sgate = lax.sort((key.reshape(-1), gate_t.reshape(-1)), num_keys=1)
    bin_shift = tb_bits + (ec.bit_length() - 1)
    bin_start = jnp.searchsorted(skey, jnp.arange(nbins, dtype=I32) << bin_shift).astype(I32)
    bp = jnp.sort(jnp.concatenate([jnp.arange(nwin, dtype=I32) * window, bin_start]))
    lo = bp
    hi = jnp.concatenate([bp[1:], jnp.full((1,), npairs, I32)])
    win = jnp.minimum(lo // window, nwin - 1)
    sbin = jnp.clip(jnp.searchsorted(bin_start, lo, side="right").astype(I32) - 1, 0, nbins - 1)
    sb = sbin // nch
    ch = sbin % nch
    first = jnp.concatenate([jnp.ones((1,), I32), (sb[1:] != sb[:-1]).astype(I32)])
    tl = skey & (tb - 1)
    trow = tl * ROW_TILES
    erow = ((skey >> tb_bits) & (ec - 1)) * ROW_TILES
    tg = tl.reshape(-1, PEER_UNROLL)
    same = tg[:, :, None] == tg[:, None, :]
    flag = (jnp.sum(same, axis=(1, 2)) > PEER_UNROLL).astype(I32)
    return (lo, hi, win, sb, ch, first), trow, erow, flag, sgate


def peer_mix_call(steps, trow, erow, flag, sgate, xn2, u2, v2, tb, ec, window):
    nrows = xn2.shape[0]
    npairs = trow.shape[0]
    nwin = npairs // window
    nsteps = steps[0].shape[0]
    ngrp = window // PEER_UNROLL + PEER_FLAG_PAD
    smem = lambda n: pl.BlockSpec((1, 1, n), lambda i, lo, hi, win, sb, ch, fs: (win[i], 0, 0),
                                  memory_space=pltpu.SMEM)
    gate_spec = pl.BlockSpec((1, window // LANES, LANES),
                             lambda i, lo, hi, win, sb, ch, fs: (win[i], 0, 0))
    tok_rows = pl.BlockSpec((tb * ROW_TILES, LANES), lambda i, lo, hi, win, sb, ch, fs: (sb[i], 0))
    tab_rows = pl.BlockSpec((ec * ROW_TILES, LANES), lambda i, lo, hi, win, sb, ch, fs: (ch[i], 0))
    flags = jnp.pad(flag.reshape(nwin, 1, -1), ((0, 0), (0, 0), (0, PEER_FLAG_PAD)))
    return pl.pallas_call(
        functools.partial(_peer_mix_kernel, window=window),
        grid_spec=pltpu.PrefetchScalarGridSpec(
            num_scalar_prefetch=6,
            grid=(nsteps,),
            in_specs=[smem(window), smem(window), smem(ngrp), gate_spec, tok_rows, tab_rows, tab_rows],
            out_specs=tok_rows,
            scratch_shapes=[pltpu.VMEM((window, LANES), F32),
                            pltpu.VMEM((window, LANES), F32)]),
        out_shape=jax.ShapeDtypeStruct((nrows, LANES), F32),
        compiler_params=_cparams(1),
        name="peer_mix",
    )(*steps,
      trow.reshape(nwin, 1, window), erow.reshape(nwin, 1, window), flags,
      sgate.reshape(nwin, window // LANES, LANES), xn2, u2, v2)


def peer_mix(xn, expert_t, gate_t, u_tab, v_tab):
    t = xn.shape[0]
    tb = min(PEER_TOKEN_BLOCK, t)
    ec = PEER_EXPERT_CHUNK
    window = PEER_WINDOW
    steps, trow, erow, flag, sgate = peer_route(expert_t, gate_t, tb, ec, window)
    rows2 = lambda a: a.reshape(a.shape[0] * ROW_TILES, LANES)
    out = peer_mix_call(steps, trow, erow, flag, sgate, rows2(xn), rows2(u_tab), rows2(v_tab),
                        tb, ec, window)
    return out.reshape(t, D_MODEL)


def peer_layer(xn, wq, k1, k2, u_tab, v_tab):
    q = matmul(xn, wq.astype(BF16), name="peer_query")
    expert_t, gate_t = peer_topk(q, k1, k2)
    return peer_mix(xn, expert_t, gate_t, u_tab, v_tab)


def _trunk(x, parts, even_mix_norm, even_w_in, even_gate_bias, even_pool_w, even_pool_scale,
           even_w_out, odd_mix_norm, odd_fourier_w, ffn_norm, peer_wq, peer_k1, peer_k2,
           peer_u, peer_v, final_norm):
    cat = lambda xs: xs[0] if len(xs) == 1 else jnp.concatenate(xs, axis=0)

    xn = rmsnorm(x, even_mix_norm[0])
    n_main = 4 * D_A + D_B
    w_in = even_w_in[0]
    z = matmul(xn, w_in[:, :n_main].astype(BF16), name="in_proj")
    w_gate = jnp.pad(w_in[:, n_main:], ((0, 0), (0, LANES - N_GATES)))
    gates = matmul_hi(xn, w_gate)
    gates_t = gates[:, :N_GATES].T
    bias = even_gate_bias[0].astype(F32)
    bias_row = jnp.pad(bias, (0, LANES - N_GATES)).reshape(1, LANES)
    bias_col = bias.reshape(N_GATES, 1)
    pw = even_pool_w[0].astype(BF16)
    hs, ybs = [], []
    for base, batch, seq in parts:
        hs.append(mlstm(z, gates, gates_t, bias_row, bias_col, base, batch, seq))
        ybs.append(pool(z, pw, even_pool_scale[0], 4 * D_A, base, batch, seq))
    h2 = hs[0] if len(hs) == 1 else jnp.concatenate(hs, axis=1)
    x = outproj(h2, z, cat(ybs), even_w_out[0].astype(BF16), x)

    xn = rmsnorm(x, ffn_norm[0])
    y = peer_layer(xn, peer_wq[0], peer_k1[0], peer_k2[0], peer_u[0], peer_v[0])

    x, xn = add_rmsnorm(x, y, odd_mix_norm[0])
    cd, sd = dft_tables(FOURIER_GROUP_DIM)
    yc, ys = fourier_channels(xn, cd, sd)
    fs = []
    for base, batch, seq in parts:
        cs, ss = dft_tables(seq)
        fs.append(fourier_seq(cs, ss, yc, ys, base, batch, seq,
                              tm=min(1024, seq), tk=min(512, seq)))
    x = matmul(cat(fs), odd_fourier_w[0].astype(BF16), res=x, name="fourier_out")

    xn = rmsnorm(x, ffn_norm[1])
    y = peer_layer(xn, peer_wq[1], peer_k1[1], peer_k2[1], peer_u[1], peer_v[1])
    _, out = add_rmsnorm(x, y, final_norm)
    return out


def kernel(x_prompt, x_sample, even_mix_norm, even_w_in, even_gate_bias, even_pool_w,
           even_pool_scale, even_w_out, odd_mix_norm, odd_fourier_w, ffn_norm, peer_wq,
           peer_k1, peer_k2, peer_u, peer_v, final_norm):
    bp, sp, d = x_prompt.shape
    bs, ss, _ = x_sample.shape
    tp = bp * sp
    x = jnp.concatenate([x_prompt.reshape(tp, d), x_sample.reshape(bs * ss, d)], axis=0)
    parts = ((0, bp, sp), (tp, bs, ss))
    y = _trunk(x, parts, even_mix_norm, even_w_in, even_gate_bias, even_pool_w,
               even_pool_scale, even_w_out, odd_mix_norm, odd_fourier_w, ffn_norm,
               peer_wq, peer_k1, peer_k2, peer_u, peer_v, final_norm)
    return y[:tp].reshape(bp, sp, d), y[tp:].reshape(bs, ss, d)
```

```python
import functools
import math

import jax
import jax.numpy as jnp
from jax import lax
from jax.experimental import pallas as pl
from jax.experimental.pallas import tpu as pltpu

F32 = jnp.float32
BF16 = jnp.bfloat16
I32 = jnp.int32
HI = lax.Precision.HIGHEST

D_MODEL = 2048
EPS = 1e-6
N_HEADS = 4
HEAD_DIM = 256
D_A = N_HEADS * HEAD_DIM
D_B = D_MODEL - D_A
N_GATES = 4 * N_HEADS
CHUNK = 128
N_POOL_GROUPS = 4
POOL_GROUP_DIM = D_B // N_POOL_GROUPS
N_FOURIER_GROUPS = 4
FOURIER_GROUP_DIM = D_MODEL // N_FOURIER_GROUPS
PEER_HEADS = 8
PEER_NKEYS = 128
PEER_HALF = 128
PEER_TOPK = 16
N_EXPERTS = PEER_NKEYS * PEER_NKEYS

LANES = 128
VMEM_LIMIT = 56 * 1024 * 1024


def _cparams(n_axes):
    return pltpu.CompilerParams(dimension_semantics=("arbitrary",) * n_axes,
                                vmem_limit_bytes=VMEM_LIMIT)


def _rms(x, g):
    ms = jnp.mean(x * x, axis=-1, keepdims=True)
    return (x * lax.rsqrt(ms + EPS)) * g


def _norm_kernel(a_ref, g_ref, xn_ref):
    xn_ref[...] = _rms(a_ref[...], g_ref[...])


def _addnorm_kernel(a_ref, b_ref, g_ref, x_ref, xn_ref):
    x = a_ref[...] + b_ref[...]
    x_ref[...] = x
    xn_ref[...] = _rms(x, g_ref[...])


def rmsnorm(a, g, tm=256):
    t, d = a.shape
    tm = min(tm, t)
    row = pl.BlockSpec((tm, d), lambda i: (i, 0))
    return pl.pallas_call(
        _norm_kernel,
        grid=(t // tm,),
        in_specs=[row, pl.BlockSpec((1, d), lambda i: (0, 0))],
        out_specs=row,
        out_shape=jax.ShapeDtypeStruct((t, d), F32),
        compiler_params=_cparams(1),
        name="rmsnorm",
    )(a, g.reshape(1, d))


def add_rmsnorm(a, b, g, tm=256):
    t, d = a.shape
    tm = min(tm, t)
    row = pl.BlockSpec((tm, d), lambda i: (i, 0))
    return pl.pallas_call(
        _addnorm_kernel,
        grid=(t // tm,),
        in_specs=[row, row, pl.BlockSpec((1, d), lambda i: (0, 0))],
        out_specs=[row, row],
        out_shape=[jax.ShapeDtypeStruct((t, d), F32)] * 2,
        compiler_params=_cparams(1),
        name="add_rmsnorm",
    )(a, b, g.reshape(1, d))


def _mm_kernel(a_ref, w_ref, o_ref):
    o_ref[...] = jnp.dot(a_ref[...].astype(BF16), w_ref[...],
                         preferred_element_type=F32).astype(o_ref.dtype)


def _mm_res_kernel(a_ref, w_ref, r_ref, o_ref):
    acc = jnp.dot(a_ref[...].astype(BF16), w_ref[...], preferred_element_type=F32)
    o_ref[...] = (acc + r_ref[...]).astype(o_ref.dtype)


def _mm_hi_kernel(a_ref, w_ref, o_ref):
    o_ref[...] = jnp.dot(a_ref[...], w_ref[...], precision=HI,
                         preferred_element_type=F32)


def matmul(a, w, res=None, tm=512, tn=512, out_dtype=F32, name="matmul"):
    m, k = a.shape
    n = w.shape[1]
    tm = min(tm, m)
    tn = min(tn, n)
    in_specs = [pl.BlockSpec((tm, k), lambda i, j: (i, 0)),
                pl.BlockSpec((k, tn), lambda i, j: (0, j))]
    args = [a, w]
    body = _mm_kernel
    if res is not None:
        in_specs.append(pl.BlockSpec((tm, tn), lambda i, j: (i, j)))
        args.append(res)
        body = _mm_res_kernel
    return pl.pallas_call(
        body,
        grid=(m // tm, n // tn),
        in_specs=in_specs,
        out_specs=pl.BlockSpec((tm, tn), lambda i, j: (i, j)),
        out_shape=jax.ShapeDtypeStruct((m, n), out_dtype),
        compiler_params=_cparams(2),
        name=name,
    )(*args)


def matmul_hi(a, w, tm=512):
    m, k = a.shape
    n = w.shape[1]
    tm = min(tm, m)
    return pl.pallas_call(
        _mm_hi_kernel,
        grid=(m // tm,),
        in_specs=[pl.BlockSpec((tm, k), lambda i: (i, 0)),
                  pl.BlockSpec((k, n), lambda i: (0, 0))],
        out_specs=pl.BlockSpec((tm, n), lambda i: (i, 0)),
        out_shape=jax.ShapeDtypeStruct((m, n), F32),
        compiler_params=_cparams(1),
        name="matmul_hi",
    )(a, w)


def _log_sigmoid(x):
    return jnp.minimum(x, 0.0) - jnp.log(1.0 + jnp.exp(-jnp.abs(x)))


def _mlstm_kernel(q_ref, k_ref, v_ref, gc_ref, gr_ref, bc_ref, br_ref, h_ref,
                  c_ref, n_ref, m_ref):
    d = pl.program_id(0)
    hh = pl.program_id(2)
    c = pl.program_id(3)
    L = CHUNK

    @pl.when(c == 0)
    def _():
        c_ref[...] = jnp.zeros_like(c_ref)
        n_ref[...] = jnp.zeros_like(n_ref)
        m_ref[...] = jnp.zeros_like(m_ref)

    q = q_ref[...] * (HEAD_DIM ** -0.5)
    k = k_ref[...]
    v = v_ref[...]
    gcol = gc_ref[...] + bc_ref[...]
    grow = gr_ref[...] + br_ref[...]
    li_idx = 2 * d * N_HEADS + hh
    lf_idx = li_idx + N_HEADS
    lane = lax.broadcasted_iota(I32, gcol.shape, 1)
    sub = lax.broadcasted_iota(I32, grow.shape, 0)
    li_col = jnp.sum(jnp.where(lane == li_idx, gcol, 0.0), axis=1, keepdims=True)
    gf_col = jnp.sum(jnp.where(lane == lf_idx, gcol, 0.0), axis=1, keepdims=True)
    li_row = jnp.sum(jnp.where(sub == li_idx, grow, 0.0), axis=0, keepdims=True)
    gf_row = jnp.sum(jnp.where(sub == lf_idx, grow, 0.0), axis=0, keepdims=True)
    lf_col = _log_sigmoid(gf_col)
    lf_row = _log_sigmoid(gf_row)

    r = lax.broadcasted_iota(I32, (L, L), 0)
    cc = lax.broadcasted_iota(I32, (L, L), 1)
    sign = 1 - 2 * d
    mask = (cc - r) * sign <= 0
    mask_t = (r - cc) * sign <= 0
    b_col = jnp.sum(jnp.where(mask, lf_row, 0.0), axis=1, keepdims=True)
    b_row = jnp.sum(jnp.where(mask_t, lf_col, 0.0), axis=0, keepdims=True)
    logw = jnp.where(mask, b_col - b_row + li_row, -jnp.inf)

    m_prev = m_ref[:, 0:1]
    inter = b_col + m_prev
    m_t = jnp.maximum(inter, jnp.max(logw, axis=1, keepdims=True))
    w = jnp.exp(logw - m_t)
    a_inter = jnp.exp(inter - m_t)
    s = lax.dot_general(q, k, (((1,), (1,)), ((), ())), precision=HI,
                        preferred_element_type=F32) * w
    cm = c_ref[...]
    num = (jnp.dot(s, v, precision=HI, preferred_element_type=F32)
           + a_inter * jnp.dot(q, cm, precision=HI, preferred_element_type=F32))
    den = (jnp.sum(s, axis=1, keepdims=True)
           + a_inter * jnp.sum(q * n_ref[...], axis=1, keepdims=True))
    h_ref[...] = num / jnp.maximum(jnp.abs(den), jnp.exp(-m_t))

    b_last = jnp.sum(lf_row, axis=1, keepdims=True)
    logu = b_last - b_col + li_col
    m_new = jnp.maximum(b_last + m_prev, jnp.max(logu, axis=0, keepdims=True))
    u = jnp.exp(logu - m_new)
    decay = jnp.exp(b_last + m_prev - m_new)
    uk = u * k
    c_ref[...] = decay * cm + lax.dot_general(
        uk, v, (((0,), (0,)), ((), ())), precision=HI, preferred_element_type=F32)
    n_ref[...] = decay * n_ref[...] + jnp.sum(uk, axis=0, keepdims=True)
    m_ref[...] = jnp.broadcast_to(m_new, m_ref.shape)


def mlstm(z, gates, gates_t, bias_row, bias_col, base_tok, batch, seq):
    nc = seq // CHUNK
    base = base_tok // CHUNK

    def blk(dd, b, c):
        return base + b * nc + c + dd * (nc - 1 - 2 * c)

    def sec(which):
        return pl.BlockSpec((CHUNK, HEAD_DIM),
                            lambda dd, b, h, c: (blk(dd, b, c), which * N_HEADS + h))

    return pl.pallas_call(
        _mlstm_kernel,
        grid=(2, batch, N_HEADS, nc),
        in_specs=[sec(0), sec(1), sec(2),
                  pl.BlockSpec((CHUNK, LANES), lambda dd, b, h, c: (blk(dd, b, c), 0)),
                  pl.BlockSpec((N_GATES, CHUNK), lambda dd, b, h, c: (0, blk(dd, b, c))),
                  pl.BlockSpec((1, LANES), lambda dd, b, h, c: (0, 0)),
                  pl.BlockSpec((N_GATES, 1), lambda dd, b, h, c: (0, 0))],
        out_specs=pl.BlockSpec((None, CHUNK, HEAD_DIM),
                               lambda dd, b, h, c: (dd, blk(dd, b, c) - base, h)),
        out_shape=jax.ShapeDtypeStruct((2, batch * seq, D_A), F32),
        scratch_shapes=[pltpu.VMEM((HEAD_DIM, HEAD_DIM), F32),
                        pltpu.VMEM((1, HEAD_DIM), F32),
                        pltpu.VMEM((1, LANES), F32)],
        compiler_params=_cparams(4),
        name="mlstm",
    )(z, z, z, gates, gates_t, bias_row, bias_col)


def _pool_kernel(prev_ref, mid_ref, next_ref, pw_ref, sc_ref, o_ref, *, seq):
    i = pl.program_id(1)
    g = pl.program_id(2)
    half = jnp.left_shift(1, g)
    T = CHUNK
    p = i * T + lax.broadcasted_iota(I32, (T, T), 0)
    lo = jnp.maximum(p - half, 0)
    hi = jnp.minimum(p + half, seq)
    inv = 1.0 / (hi - lo).astype(F32)
    acc = jnp.zeros((T, POOL_GROUP_DIM), F32)
    for off, ref in ((-T, prev_ref), (0, mid_ref), (T, next_ref)):
        pj = i * T + off + lax.broadcasted_iota(I32, (T, T), 1)
        inside = (pj >= lo) & (pj < hi)
        a = jnp.where(inside, inv, 0.0)
        if off == 0:
            a = a - jnp.where(pj == p, 1.0, 0.0)
        acc = acc + jnp.dot(a, ref[...], precision=HI, preferred_element_type=F32)
    y = jnp.dot(acc.astype(BF16), pw_ref[...], preferred_element_type=F32)
    o_ref[...] = y * sc_ref[...]


def pool(z, pool_w, pool_scale, col0, base_tok, batch, seq):
    nt = seq // CHUNK
    base = base_tok // CHUNK
    cb = col0 // POOL_GROUP_DIM

    def spec(shift):
        return pl.BlockSpec(
            (CHUNK, POOL_GROUP_DIM),
            lambda b, i, g: (base + b * nt + jnp.clip(i + shift, 0, nt - 1), cb + g))

    return pl.pallas_call(
        functools.partial(_pool_kernel, seq=seq),
        grid=(batch, nt, N_POOL_GROUPS),
        in_specs=[spec(-1), spec(0), spec(1),
                  pl.BlockSpec((None, POOL_GROUP_DIM, POOL_GROUP_DIM), lambda b, i, g: (g, 0, 0)),
                  pl.BlockSpec((1, POOL_GROUP_DIM), lambda b, i, g: (0, g))],
        out_specs=pl.BlockSpec((CHUNK, POOL_GROUP_DIM), lambda b, i, g: (b * nt + i, g)),
        out_shape=jax.ShapeDtypeStruct((batch * seq, D_B), F32),
        compiler_params=_cparams(3),
        name="pool",
    )(z, z, z, pool_w, pool_scale.reshape(1, D_B))


def _outproj_kernel(hf_ref, hb_ref, o_ref, yb_ref, wa_ref, wb_ref, x_ref, out_ref):
    ha = (hf_ref[...] + hb_ref[...]) * jax.nn.sigmoid(o_ref[...])
    acc = jnp.dot(ha.astype(BF16), wa_ref[...], preferred_element_type=F32)
    acc = acc + jnp.dot(yb_ref[...].astype(BF16), wb_ref[...], preferred_element_type=F32)
    out_ref[...] = acc + x_ref[...]


def outproj(h2, z, yb, w_out, x, tm=512, tn=512):
    t = x.shape[0]
    tm = min(tm, t)
    ocb = 3 * D_A // D_A
    return pl.pallas_call(
        _outproj_kernel,
        grid=(t // tm, D_MODEL // tn),
        in_specs=[pl.BlockSpec((None, tm, D_A), lambda i, j: (0, i, 0)),
                  pl.BlockSpec((None, tm, D_A), lambda i, j: (1, i, 0)),
                  pl.BlockSpec((tm, D_A), lambda i, j: (i, ocb)),
                  pl.BlockSpec((tm, D_B), lambda i, j: (i, 0)),
                  pl.BlockSpec((D_A, tn), lambda i, j: (0, j)),
                  pl.BlockSpec((D_B, tn), lambda i, j: (D_A // D_B, j)),
                  pl.BlockSpec((tm, tn), lambda i, j: (i, j))],
        out_specs=pl.BlockSpec((tm, tn), lambda i, j: (i, j)),
        out_shape=jax.ShapeDtypeStruct((t, D_MODEL), F32),
        compiler_params=_cparams(2),
        name="outproj",
    )(h2, h2, z, yb, w_out, w_out, x)


def _dft_kernel(c_ref, s_ref, *, n):
    tr = c_ref.shape[0]
    r = pl.program_id(0) * tr + lax.broadcasted_iota(I32, (tr, n), 0)
    c = lax.broadcasted_iota(I32, (tr, n), 1)
    ang = ((r * c) & (n - 1)).astype(F32) * (2.0 * math.pi / n)
    scale = n ** -0.5
    c_ref[...] = (jnp.cos(ang) * scale).astype(BF16)
    s_ref[...] = (jnp.sin(ang) * scale).astype(BF16)


def dft_tables(n, tr=256):
    assert n & (n - 1) == 0
    blk = pl.BlockSpec((tr, n), lambda i: (i, 0))
    return pl.pallas_call(
        functools.partial(_dft_kernel, n=n),
        grid=(n // tr,),
        out_specs=[blk, blk],
        out_shape=[jax.ShapeDtypeStruct((n, n), BF16)] * 2,
        compiler_params=_cparams(1),
        name="dft_tables",
    )()


def _fourier_ch_kernel(x_ref, cd_ref, sd_ref, yc_ref, ys_ref):
    xb = x_ref[...].astype(BF16)
    yc_ref[...] = jnp.dot(xb, cd_ref[...], preferred_element_type=F32).astype(BF16)
    ys_ref[...] = jnp.dot(xb, sd_ref[...], preferred_element_type=F32).astype(BF16)


def fourier_channels(xn, cd, sd, tm=512):
    t = xn.shape[0]
    tm = min(tm, t)
    gd = FOURIER_GROUP_DIM
    tile = pl.BlockSpec((tm, gd), lambda i, g: (i, g))
    tab = pl.BlockSpec((gd, gd), lambda i, g: (0, 0))
    return pl.pallas_call(
        _fourier_ch_kernel,
        grid=(t // tm, N_FOURIER_GROUPS),
        in_specs=[tile, tab, tab],
        out_specs=[tile, tile],
        out_shape=[jax.ShapeDtypeStruct((t, D_MODEL), BF16)] * 2,
        compiler_params=_cparams(2),
        name="fourier_channels",
    )(xn, cd, sd)


def _fourier_seq_kernel(cs_ref, ss_ref, yc_ref, ys_ref, o_ref, acc_ref):
    kk = pl.program_id(3)

    @pl.when(kk == 0)
    def _():
        acc_ref[...] = jnp.zeros_like(acc_ref)

    acc_ref[...] += (jnp.dot(cs_ref[...], yc_ref[...], preferred_element_type=F32)
                     - jnp.dot(ss_ref[...], ys_ref[...], preferred_element_type=F32))

    @pl.when(kk == pl.num_programs(3) - 1)
    def _():
        o_ref[...] = acc_ref[...].astype(o_ref.dtype)


def fourier_seq(cs, ss, yc, ys, base_tok, batch, seq, tm=1024, tn=1024, tk=512):
    nk = seq // tk
    base = base_tok // tk
    a_spec = pl.BlockSpec((tm, tk), lambda b, i, j, k: (i, k))
    y_spec = pl.BlockSpec((tk, tn), lambda b, i, j, k: (base + b * nk + k, j))
    return pl.pallas_call(
        _fourier_seq_kernel,
        grid=(batch, seq // tm, D_MODEL // tn, nk),
        in_specs=[a_spec, a_spec, y_spec, y_spec],
        out_specs=pl.BlockSpec((tm, tn), lambda b, i, j, k: (b * (seq // tm) + i, j)),
        out_shape=jax.ShapeDtypeStruct((batch * seq, D_MODEL), BF16),
        scratch_shapes=[pltpu.VMEM((tm, tn), F32)],
        compiler_params=_cparams(4),
        name="fourier_seq",
    )(cs, ss, yc, ys)


def _top16_rows(s, row_key):
    n = s.shape[1]
    out_iota = lax.broadcasted_iota(I32, (PEER_TOPK, n), 0)
    vals = jnp.zeros((PEER_TOPK, n), F32)
    keys = jnp.zeros((PEER_TOPK, n), I32)
    big = jnp.int32(1 << 30)
    for rnk in range(PEER_TOPK):
        mx = jnp.max(s, axis=0, keepdims=True)
        kmin = jnp.min(jnp.where(s == mx, row_key, big), axis=0, keepdims=True)
        vals = jnp.where(out_iota == rnk, mx, vals)
        keys = jnp.where(out_iota == rnk, kmin, keys)
        s = jnp.where(row_key == kmin, -jnp.inf, s)
    return vals, keys


def _peer_topk_kernel(q_ref, k1_ref, k2_ref, e_ref, g_ref):
    tt = q_ref.shape[0]
    k1 = k1_ref[...]
    k2 = k2_ref[...]
    key_iota = lax.broadcasted_iota(I32, (PEER_NKEYS, tt), 0)
    nt = (((1,), (1,)), ((), ()))
    blocks = [(a, 16 if a == 0 else 8) for a in range(8)]
    for h in range(PEER_HEADS):
        q1 = q_ref[:, h * 2 * PEER_HALF: h * 2 * PEER_HALF + PEER_HALF]
        q2 = q_ref[:, h * 2 * PEER_HALF + PEER_HALF: (h + 1) * 2 * PEER_HALF]
        s1 = lax.dot_general(k1, q1, nt, precision=HI, preferred_element_type=F32)
        s2 = lax.dot_general(k2, q2, nt, precision=HI, preferred_element_type=F32)
        v1, i1 = _top16_rows(s1, key_iota)
        v2, i2 = _top16_rows(s2, key_iota)
        cands, ids, orders = [], [], []
        for a, rows in blocks:
            bi = lax.broadcasted_iota(I32, (rows, tt), 0)
            ok = (bi + 1) * (a + 1) <= PEER_TOPK
            cands.append(jnp.where(ok, v1[a:a + 1, :] + v2[0:rows, :], -jnp.inf))
            ids.append(i1[a:a + 1, :] * PEER_NKEYS + i2[0:rows, :])
            orders.append(a * PEER_TOPK + bi)
        ai = lax.broadcasted_iota(I32, (8, tt), 0) + 8
        cands.append(v1[8:16, :] + v2[0:1, :])
        ids.append(i1[8:16, :] * PEER_NKEYS + i2[0:1, :])
        orders.append(ai * PEER_TOPK)
        cand = jnp.concatenate(cands, axis=0)
        cid = jnp.concatenate(ids, axis=0)
        order = jnp.concatenate(orders, axis=0)
        ts, tpos = _top16_rows(cand, order)
        experts = jnp.zeros((PEER_TOPK, tt), I32)
        out_iota = lax.broadcasted_iota(I32, (PEER_TOPK, tt), 0)
        for rnk in range(PEER_TOPK):
            sel = jnp.max(jnp.where(order == tpos[rnk:rnk + 1, :], cid, -1), axis=0, keepdims=True)
            experts = jnp.where(out_iota == rnk, sel, experts)
        ex = jnp.exp(ts - ts[0:1, :])
        gate = ex / jnp.sum(ex, axis=0, keepdims=True)
        e_ref[h * PEER_TOPK:(h + 1) * PEER_TOPK, :] = experts
        g_ref[h * PEER_TOPK:(h + 1) * PEER_TOPK, :] = gate


def peer_topk(q, k1, k2, tt=256):
    t = q.shape[0]
    tt = min(tt, t)
    ne = PEER_HEADS * PEER_TOPK
    keys = pl.BlockSpec((PEER_NKEYS, PEER_HALF), lambda i: (0, 0))
    out = pl.BlockSpec((ne, tt), lambda i: (0, i))
    return pl.pallas_call(
        _peer_topk_kernel,
        grid=(t // tt,),
        in_specs=[pl.BlockSpec((tt, q.shape[1]), lambda i: (i, 0)), keys, keys],
        out_specs=[out, out],
        out_shape=[jax.ShapeDtypeStruct((ne, t), I32), jax.ShapeDtypeStruct((ne, t), F32)],
        compiler_params=_cparams(1),
        name="peer_topk",
    )(q, k1, k2)


def _gelu_exact(x):
    return 0.5 * x * (1.0 + lax.erf(x * (2.0 ** -0.5)))


def _peer_dense_kernel(i1_ref, i2_ref, g_ref, xn_ref, ut_ref, v_ref, out_ref, gate_ref, *, tm, eb):
    j = pl.program_id(1)
    nk = PEER_NKEYS

    @pl.when(j == 0)
    def _():
        out_ref[...] = jnp.zeros_like(out_ref)
        key = lax.broadcasted_iota(I32, (nk, nk), 0)
        nt = (((1,), (1,)), ((), ()))

        def build(t, carry):
            a_t = jnp.where(key == i1_ref[pl.ds(t, 1), :], 1.0, 0.0).astype(BF16)
            b_t = jnp.where(key == i2_ref[pl.ds(t, 1), :], g_ref[pl.ds(t, 1), :], 0.0).astype(BF16)
            g_t = lax.dot_general(a_t, b_t, nt, preferred_element_type=F32)
            gate_ref[pl.ds(t, nk, stride=tm), :] = g_t
            return carry

        lax.fori_loop(0, tm, build, 0)

    act = jnp.dot(xn_ref[...], ut_ref[...], preferred_element_type=F32)
    nb = eb // nk
    z = [(_gelu_exact(act[:, b * nk:(b + 1) * nk])
          * gate_ref[pl.ds(pl.multiple_of((j * nb + b) * tm, tm), tm), :]).astype(BF16)
         for b in range(nb)]
    out_ref[...] += jnp.dot(jnp.concatenate(z, axis=1), v_ref[...], preferred_element_type=F32)


def peer_dense(xn, expert_t, gate_t, u_tab, v_tab, tm=256, eb=512):
    t, d = xn.shape
    tm = min(tm, t)
    ns = expert_t.shape[0]
    i1 = (expert_t // PEER_NKEYS).T
    i2 = (expert_t % PEER_NKEYS).T
    g = gate_t.T
    slot = pl.BlockSpec((tm, ns), lambda i, j: (i, 0))
    row = pl.BlockSpec((tm, d), lambda i, j: (i, 0))
    return pl.pallas_call(
        functools.partial(_peer_dense_kernel, tm=tm, eb=eb),
        grid=(t // tm, N_EXPERTS // eb),
        in_specs=[slot, slot, slot, row,
                  pl.BlockSpec((d, eb), lambda i, j: (0, j)),
                  pl.BlockSpec((eb, d), lambda i, j: (j, 0))],
        out_specs=row,
        out_shape=jax.ShapeDtypeStruct((t, d), F32),
        scratch_shapes=[pltpu.VMEM((PEER_NKEYS * tm, PEER_NKEYS), F32)],
        compiler_params=_cparams(2),
        name="peer_dense",
    )(i1, i2, g, xn.astype(BF16), u_tab.astype(BF16).T, v_tab.astype(BF16))


def peer_layer(xn, wq, k1, k2, u_tab, v_tab):
    q = matmul(xn, wq.astype(BF16), name="peer_query")
    expert_t, gate_t = peer_topk(q, k1, k2)
    return peer_dense(xn, expert_t, gate_t, u_tab, v_tab)


def _trunk(x, parts, even_mix_norm, even_w_in, even_gate_bias, even_pool_w, even_pool_scale,
           even_w_out, odd_mix_norm, odd_fourier_w, ffn_norm, peer_wq, peer_k1, peer_k2,
           peer_u, peer_v, final_norm):
    cat = lambda xs: xs[0] if len(xs) == 1 else jnp.concatenate(xs, axis=0)

    xn = rmsnorm(x, even_mix_norm[0])
    n_main = 4 * D_A + D_B
    w_in = even_w_in[0]
    z = matmul(xn, w_in[:, :n_main].astype(BF16), name="in_proj")
    w_gate = jnp.pad(w_in[:, n_main:], ((0, 0), (0, LANES - N_GATES)))
    gates = matmul_hi(xn, w_gate)
    gates_t = gates[:, :N_GATES].T
    bias = even_gate_bias[0].astype(F32)
    bias_row = jnp.pad(bias, (0, LANES - N_GATES)).reshape(1, LANES)
    bias_col = bias.reshape(N_GATES, 1)
    pw = even_pool_w[0].astype(BF16)
    hs, ybs = [], []
    for base, batch, seq in parts:
        hs.append(mlstm(z, gates, gates_t, bias_row, bias_col, base, batch, seq))
        ybs.append(pool(z, pw, even_pool_scale[0], 4 * D_A, base, batch, seq))
    h2 = hs[0] if len(hs) == 1 else jnp.concatenate(hs, axis=1)
    x = outproj(h2, z, cat(ybs), even_w_out[0].astype(BF16), x)

    xn = rmsnorm(x, ffn_norm[0])
    y = peer_layer(xn, peer_wq[0], peer_k1[0], peer_k2[0], peer_u[0], peer_v[0])

    x, xn = add_rmsnorm(x, y, odd_mix_norm[0])
    cd, sd = dft_tables(FOURIER_GROUP_DIM)
    yc, ys = fourier_channels(xn, cd, sd)
    fs = []
    for base, batch, seq in parts:
        cs, ss = dft_tables(seq)
        fs.append(fourier_seq(cs, ss, yc, ys, base, batch, seq,
                              tm=min(1024, seq), tk=min(512, seq)))
    x = matmul(cat(fs), odd_fourier_w[0].astype(BF16), res=x, name="fourier_out")

    xn = rmsnorm(x, ffn_norm[1])
    y = peer_layer(xn, peer_wq[1], peer_k1[1], peer_k2[1], peer_u[1], peer_v[1])
    _, out = add_rmsnorm(x, y, final_norm)
    return out


def kernel(x_prompt, x_sample, even_mix_norm, even_w_in, even_gate_bias, even_pool_w,
           even_pool_scale, even_w_out, odd_mix_norm, odd_fourier_w, ffn_norm, peer_wq,
           peer_k1, peer_k2, peer_u, peer_v, final_norm):
    bp, sp, d = x_prompt.shape
    bs, ss, _ = x_sample.shape
    tp = bp * sp
    x = jnp.concatenate([x_prompt.reshape(tp, d), x_sample.reshape(bs * ss, d)], axis=0)
    parts = ((0, bp, sp), (tp, bs, ss))
    y = _trunk(x, parts, even_mix_norm, even_w_in, even_gate_bias, even_pool_w,
               even_pool_scale, even_w_out, odd_mix_norm, odd_fourier_w, ffn_norm,
               peer_wq, peer_k1, peer_k2, peer_u, peer_v, final_norm)
    return y[:tp].reshape(bp, sp, d), y[tp:].reshape(bs, ss, d)
```

```python
import functools
import math

import jax
import jax.numpy as jnp
from jax import lax
from jax.experimental import pallas as pl
from jax.experimental.pallas import tpu as pltpu

F32 = jnp.float32
BF16 = jnp.bfloat16
I32 = jnp.int32
HI = lax.Precision.HIGHEST

D_MODEL = 2048
EPS = 1e-6
N_HEADS = 4
HEAD_DIM = 256
D_A = N_HEADS * HEAD_DIM
D_B = D_MODEL - D_A
N_GATES = 4 * N_HEADS
CHUNK = 128
N_POOL_GROUPS = 4
POOL_GROUP_DIM = D_B // N_POOL_GROUPS
N_FOURIER_GROUPS = 4
FOURIER_GROUP_DIM = D_MODEL // N_FOURIER_GROUPS
PEER_HEADS = 8
PEER_NKEYS = 128
PEER_HALF = 128
PEER_TOPK = 16
N_EXPERTS = PEER_NKEYS * PEER_NKEYS

LANES = 128
PEER_BUILD_UNROLL = 16
VMEM_LIMIT = 56 * 1024 * 1024


def _cparams(n_axes):
    return pltpu.CompilerParams(dimension_semantics=("arbitrary",) * n_axes,
                                vmem_limit_bytes=VMEM_LIMIT)


def _rms(x, g):
    ms = jnp.mean(x * x, axis=-1, keepdims=True)
    return (x * lax.rsqrt(ms + EPS)) * g


def _norm_kernel(a_ref, g_ref, xn_ref):
    xn_ref[...] = _rms(a_ref[...], g_ref[...])


def _addnorm_kernel(a_ref, b_ref, g_ref, x_ref, xn_ref):
    x = a_ref[...] + b_ref[...]
    x_ref[...] = x
    xn_ref[...] = _rms(x, g_ref[...])


def rmsnorm(a, g, tm=256):
    t, d = a.shape
    tm = min(tm, t)
    row = pl.BlockSpec((tm, d), lambda i: (i, 0))
    return pl.pallas_call(
        _norm_kernel,
        grid=(t // tm,),
        in_specs=[row, pl.BlockSpec((1, d), lambda i: (0, 0))],
        out_specs=row,
        out_shape=jax.ShapeDtypeStruct((t, d), F32),
        compiler_params=_cparams(1),
        name="rmsnorm",
    )(a, g.reshape(1, d))


def add_rmsnorm(a, b, g, tm=256):
    t, d = a.shape
    tm = min(tm, t)
    row = pl.BlockSpec((tm, d), lambda i: (i, 0))
    return pl.pallas_call(
        _addnorm_kernel,
        grid=(t // tm,),
        in_specs=[row, row, pl.BlockSpec((1, d), lambda i: (0, 0))],
        out_specs=[row, row],
        out_shape=[jax.ShapeDtypeStruct((t, d), F32)] * 2,
        compiler_params=_cparams(1),
        name="add_rmsnorm",
    )(a, b, g.reshape(1, d))


def _mm_kernel(a_ref, w_ref, o_ref):
    o_ref[...] = jnp.dot(a_ref[...].astype(BF16), w_ref[...],
                         preferred_element_type=F32).astype(o_ref.dtype)


def _mm_res_kernel(a_ref, w_ref, r_ref, o_ref):
    acc = jnp.dot(a_ref[...].astype(BF16), w_ref[...], preferred_element_type=F32)
    o_ref[...] = (acc + r_ref[...]).astype(o_ref.dtype)


def _mm_hi_kernel(a_ref, w_ref, o_ref):
    o_ref[...] = jnp.dot(a_ref[...], w_ref[...], precision=HI,
                         preferred_element_type=F32)


def matmul(a, w, res=None, tm=512, tn=512, out_dtype=F32, name="matmul"):
    m, k = a.shape
    n = w.shape[1]
    tm = min(tm, m)
    tn = min(tn, n)
    in_specs = [pl.BlockSpec((tm, k), lambda i, j: (i, 0)),
                pl.BlockSpec((k, tn), lambda i, j: (0, j))]
    args = [a, w]
    body = _mm_kernel
    if res is not None:
        in_specs.append(pl.BlockSpec((tm, tn), lambda i, j: (i, j)))
        args.append(res)
        body = _mm_res_kernel
    return pl.pallas_call(
        body,
        grid=(m // tm, n // tn),
        in_specs=in_specs,
        out_specs=pl.BlockSpec((tm, tn), lambda i, j: (i, j)),
        out_shape=jax.ShapeDtypeStruct((m, n), out_dtype),
        compiler_params=_cparams(2),
        name=name,
    )(*args)


def matmul_hi(a, w, tm=512):
    m, k = a.shape
    n = w.shape[1]
    tm = min(tm, m)
    return pl.pallas_call(
        _mm_hi_kernel,
        grid=(m // tm,),
        in_specs=[pl.BlockSpec((tm, k), lambda i: (i, 0)),
                  pl.BlockSpec((k, n), lambda i: (0, 0))],
        out_specs=pl.BlockSpec((tm, n), lambda i: (i, 0)),
        out_shape=jax.ShapeDtypeStruct((m, n), F32),
        compiler_params=_cparams(1),
        name="matmul_hi",
    )(a, w)


def _log_sigmoid(x):
    return jnp.minimum(x, 0.0) - jnp.log(1.0 + jnp.exp(-jnp.abs(x)))


def _mlstm_kernel(q_ref, k_ref, v_ref, gc_ref, gr_ref, bc_ref, br_ref, h_ref,
                  c_ref, n_ref, m_ref):
    d = pl.program_id(0)
    hh = pl.program_id(2)
    c = pl.program_id(3)
    L = CHUNK

    @pl.when(c == 0)
    def _():
        c_ref[...] = jnp.zeros_like(c_ref)
        n_ref[...] = jnp.zeros_like(n_ref)
        m_ref[...] = jnp.zeros_like(m_ref)

    q = q_ref[...] * (HEAD_DIM ** -0.5)
    k = k_ref[...]
    v = v_ref[...]
    gcol = gc_ref[...] + bc_ref[...]
    grow = gr_ref[...] + br_ref[...]
    li_idx = 2 * d * N_HEADS + hh
    lf_idx = li_idx + N_HEADS
    lane = lax.broadcasted_iota(I32, gcol.shape, 1)
    sub = lax.broadcasted_iota(I32, grow.shape, 0)
    li_col = jnp.sum(jnp.where(lane == li_idx, gcol, 0.0), axis=1, keepdims=True)
    gf_col = jnp.sum(jnp.where(lane == lf_idx, gcol, 0.0), axis=1, keepdims=True)
    li_row = jnp.sum(jnp.where(sub == li_idx, grow, 0.0), axis=0, keepdims=True)
    gf_row = jnp.sum(jnp.where(sub == lf_idx, grow, 0.0), axis=0, keepdims=True)
    lf_col = _log_sigmoid(gf_col)
    lf_row = _log_sigmoid(gf_row)

    r = lax.broadcasted_iota(I32, (L, L), 0)
    cc = lax.broadcasted_iota(I32, (L, L), 1)
    sign = 1 - 2 * d
    mask = (cc - r) * sign <= 0
    mask_t = (r - cc) * sign <= 0
    b_col = jnp.sum(jnp.where(mask, lf_row, 0.0), axis=1, keepdims=True)
    b_row = jnp.sum(jnp.where(mask_t, lf_col, 0.0), axis=0, keepdims=True)
    logw = jnp.where(mask, b_col - b_row + li_row, -jnp.inf)

    m_prev = m_ref[:, 0:1]
    inter = b_col + m_prev
    m_t = jnp.maximum(inter, jnp.max(logw, axis=1, keepdims=True))
    w = jnp.exp(logw - m_t)
    a_inter = jnp.exp(inter - m_t)
    s = lax.dot_general(q, k, (((1,), (1,)), ((), ())), precision=HI,
                        preferred_element_type=F32) * w
    cm = c_ref[...]
    num = (jnp.dot(s, v, precision=HI, preferred_element_type=F32)
           + a_inter * jnp.dot(q, cm, precision=HI, preferred_element_type=F32))
    den = (jnp.sum(s, axis=1, keepdims=True)
           + a_inter * jnp.sum(q * n_ref[...], axis=1, keepdims=True))
    h_ref[...] = num / jnp.maximum(jnp.abs(den), jnp.exp(-m_t))

    b_last = jnp.sum(lf_row, axis=1, keepdims=True)
    logu = b_last - b_col + li_col
    m_new = jnp.maximum(b_last + m_prev, jnp.max(logu, axis=0, keepdims=True))
    u = jnp.exp(logu - m_new)
    decay = jnp.exp(b_last + m_prev - m_new)
    uk = u * k
    c_ref[...] = decay * cm + lax.dot_general(
        uk, v, (((0,), (0,)), ((), ())), precision=HI, preferred_element_type=F32)
    n_ref[...] = decay * n_ref[...] + jnp.sum(uk, axis=0, keepdims=True)
    m_ref[...] = jnp.broadcast_to(m_new, m_ref.shape)


def mlstm(z, gates, gates_t, bias_row, bias_col, base_tok, batch, seq):
    nc = seq // CHUNK
    base = base_tok // CHUNK

    def blk(dd, b, c):
        return base + b * nc + c + dd * (nc - 1 - 2 * c)

    def sec(which):
        return pl.BlockSpec((CHUNK, HEAD_DIM),
                            lambda dd, b, h, c: (blk(dd, b, c), which * N_HEADS + h))

    return pl.pallas_call(
        _mlstm_kernel,
        grid=(2, batch, N_HEADS, nc),
        in_specs=[sec(0), sec(1), sec(2),
                  pl.BlockSpec((CHUNK, LANES), lambda dd, b, h, c: (blk(dd, b, c), 0)),
                  pl.BlockSpec((N_GATES, CHUNK), lambda dd, b, h, c: (0, blk(dd, b, c))),
                  pl.BlockSpec((1, LANES), lambda dd, b, h, c: (0, 0)),
                  pl.BlockSpec((N_GATES, 1), lambda dd, b, h, c: (0, 0))],
        out_specs=pl.BlockSpec((None, CHUNK, HEAD_DIM),
                               lambda dd, b, h, c: (dd, blk(dd, b, c) - base, h)),
        out_shape=jax.ShapeDtypeStruct((2, batch * seq, D_A), F32),
        scratch_shapes=[pltpu.VMEM((HEAD_DIM, HEAD_DIM), F32),
                        pltpu.VMEM((1, HEAD_DIM), F32),
                        pltpu.VMEM((1, LANES), F32)],
        compiler_params=_cparams(4),
        name="mlstm",
    )(z, z, z, gates, gates_t, bias_row, bias_col)


def _pool_kernel(prev_ref, mid_ref, next_ref, pw_ref, sc_ref, o_ref, *, seq):
    i = pl.program_id(1)
    g = pl.program_id(2)
    half = jnp.left_shift(1, g)
    T = CHUNK
    p = i * T + lax.broadcasted_iota(I32, (T, T), 0)
    lo = jnp.maximum(p - half, 0)
    hi = jnp.minimum(p + half, seq)
    inv = 1.0 / (hi - lo).astype(F32)
    acc = jnp.zeros((T, POOL_GROUP_DIM), F32)
    for off, ref in ((-T, prev_ref), (0, mid_ref), (T, next_ref)):
        pj = i * T + off + lax.broadcasted_iota(I32, (T, T), 1)
        inside = (pj >= lo) & (pj < hi)
        a = jnp.where(inside, inv, 0.0)
        if off == 0:
            a = a - jnp.where(pj == p, 1.0, 0.0)
        acc = acc + jnp.dot(a, ref[...], precision=HI, preferred_element_type=F32)
    y = jnp.dot(acc.astype(BF16), pw_ref[...], preferred_element_type=F32)
    o_ref[...] = y * sc_ref[...]


def pool(z, pool_w, pool_scale, col0, base_tok, batch, seq):
    nt = seq // CHUNK
    base = base_tok // CHUNK
    cb = col0 // POOL_GROUP_DIM

    def spec(shift):
        return pl.BlockSpec(
            (CHUNK, POOL_GROUP_DIM),
            lambda b, i, g: (base + b * nt + jnp.clip(i + shift, 0, nt - 1), cb + g))

    return pl.pallas_call(
        functools.partial(_pool_kernel, seq=seq),
        grid=(batch, nt, N_POOL_GROUPS),
        in_specs=[spec(-1), spec(0), spec(1),
                  pl.BlockSpec((None, POOL_GROUP_DIM, POOL_GROUP_DIM), lambda b, i, g: (g, 0, 0)),
                  pl.BlockSpec((1, POOL_GROUP_DIM), lambda b, i, g: (0, g))],
        out_specs=pl.BlockSpec((CHUNK, POOL_GROUP_DIM), lambda b, i, g: (b * nt + i, g)),
        out_shape=jax.ShapeDtypeStruct((batch * seq, D_B), F32),
        compiler_params=_cparams(3),
        name="pool",
    )(z, z, z, pool_w, pool_scale.reshape(1, D_B))


def _outproj_kernel(hf_ref, hb_ref, o_ref, yb_ref, wa_ref, wb_ref, x_ref, out_ref):
    ha = (hf_ref[...] + hb_ref[...]) * jax.nn.sigmoid(o_ref[...])
    acc = jnp.dot(ha.astype(BF16), wa_ref[...], preferred_element_type=F32)
    acc = acc + jnp.dot(yb_ref[...].astype(BF16), wb_ref[...], preferred_element_type=F32)
    out_ref[...] = acc + x_ref[...]


def outproj(h2, z, yb, w_out, x, tm=512, tn=512):
    t = x.shape[0]
    tm = min(tm, t)
    ocb = 3 * D_A // D_A
    return pl.pallas_call(
        _outproj_kernel,
        grid=(t // tm, D_MODEL // tn),
        in_specs=[pl.BlockSpec((None, tm, D_A), lambda i, j: (0, i, 0)),
                  pl.BlockSpec((None, tm, D_A), lambda i, j: (1, i, 0)),
                  pl.BlockSpec((tm, D_A), lambda i, j: (i, ocb)),
                  pl.BlockSpec((tm, D_B), lambda i, j: (i, 0)),
                  pl.BlockSpec((D_A, tn), lambda i, j: (0, j)),
                  pl.BlockSpec((D_B, tn), lambda i, j: (D_A // D_B, j)),
                  pl.BlockSpec((tm, tn), lambda i, j: (i, j))],
        out_specs=pl.BlockSpec((tm, tn), lambda i, j: (i, j)),
        out_shape=jax.ShapeDtypeStruct((t, D_MODEL), F32),
        compiler_params=_cparams(2),
        name="outproj",
    )(h2, h2, z, yb, w_out, w_out, x)


def _dft_kernel(c_ref, s_ref, *, n):
    tr = c_ref.shape[0]
    r = pl.program_id(0) * tr + lax.broadcasted_iota(I32, (tr, n), 0)
    c = lax.broadcasted_iota(I32, (tr, n), 1)
    ang = ((r * c) & (n - 1)).astype(F32) * (2.0 * math.pi / n)
    scale = n ** -0.5
    c_ref[...] = (jnp.cos(ang) * scale).astype(BF16)
    s_ref[...] = (jnp.sin(ang) * scale).astype(BF16)


def dft_tables(n, tr=256):
    assert n & (n - 1) == 0
    blk = pl.BlockSpec((tr, n), lambda i: (i, 0))
    return pl.pallas_call(
        functools.partial(_dft_kernel, n=n),
        grid=(n // tr,),
        out_specs=[blk, blk],
        out_shape=[jax.ShapeDtypeStruct((n, n), BF16)] * 2,
        compiler_params=_cparams(1),
        name="dft_tables",
    )()


def _fourier_ch_kernel(x_ref, cd_ref, sd_ref, yc_ref, ys_ref):
    xb = x_ref[...].astype(BF16)
    yc_ref[...] = jnp.dot(xb, cd_ref[...], preferred_element_type=F32).astype(BF16)
    ys_ref[...] = jnp.dot(xb, sd_ref[...], preferred_element_type=F32).astype(BF16)


def fourier_channels(xn, cd, sd, tm=512):
    t = xn.shape[0]
    tm = min(tm, t)
    gd = FOURIER_GROUP_DIM
    tile = pl.BlockSpec((tm, gd), lambda i, g: (i, g))
    tab = pl.BlockSpec((gd, gd), lambda i, g: (0, 0))
    return pl.pallas_call(
        _fourier_ch_kernel,
        grid=(t // tm, N_FOURIER_GROUPS),
        in_specs=[tile, tab, tab],
        out_specs=[tile, tile],
        out_shape=[jax.ShapeDtypeStruct((t, D_MODEL), BF16)] * 2,
        compiler_params=_cparams(2),
        name="fourier_channels",
    )(xn, cd, sd)


def _fourier_seq_kernel(cs_ref, ss_ref, yc_ref, ys_ref, o_ref, acc_ref):
    kk = pl.program_id(3)

    @pl.when(kk == 0)
    def _():
        acc_ref[...] = jnp.zeros_like(acc_ref)

    acc_ref[...] += (jnp.dot(cs_ref[...], yc_ref[...], preferred_element_type=F32)
                     - jnp.dot(ss_ref[...], ys_ref[...], preferred_element_type=F32))

    @pl.when(kk == pl.num_programs(3) - 1)
    def _():
        o_ref[...] = acc_ref[...].astype(o_ref.dtype)


def fourier_seq(cs, ss, yc, ys, base_tok, batch, seq, tm=1024, tn=1024, tk=512):
    nk = seq // tk
    base = base_tok // tk
    a_spec = pl.BlockSpec((tm, tk), lambda b, i, j, k: (i, k))
    y_spec = pl.BlockSpec((tk, tn), lambda b, i, j, k: (base + b * nk + k, j))
    return pl.pallas_call(
        _fourier_seq_kernel,
        grid=(batch, seq // tm, D_MODEL // tn, nk),
        in_specs=[a_spec, a_spec, y_spec, y_spec],
        out_specs=pl.BlockSpec((tm, tn), lambda b, i, j, k: (b * (seq // tm) + i, j)),
        out_shape=jax.ShapeDtypeStruct((batch * seq, D_MODEL), BF16),
        scratch_shapes=[pltpu.VMEM((tm, tn), F32)],
        compiler_params=_cparams(4),
        name="fourier_seq",
    )(cs, ss, yc, ys)


def _top16_rows(s, row_key):
    n = s.shape[1]
    out_iota = lax.broadcasted_iota(I32, (PEER_TOPK, n), 0)
    vals = jnp.zeros((PEER_TOPK, n), F32)
    keys = jnp.zeros((PEER_TOPK, n), I32)
    big = jnp.int32(1 << 30)
    for rnk in range(PEER_TOPK):
        mx = jnp.max(s, axis=0, keepdims=True)
        kmin = jnp.min(jnp.where(s == mx, row_key, big), axis=0, keepdims=True)
        vals = jnp.where(out_iota == rnk, mx, vals)
        keys = jnp.where(out_iota == rnk, kmin, keys)
        s = jnp.where(row_key == kmin, -jnp.inf, s)
    return vals, keys


def _peer_topk_kernel(q_ref, k1_ref, k2_ref, e_ref, g_ref):
    tt = q_ref.shape[0]
    k1 = k1_ref[...]
    k2 = k2_ref[...]
    key_iota = lax.broadcasted_iota(I32, (PEER_NKEYS, tt), 0)
    nt = (((1,), (1,)), ((), ()))
    blocks = [(a, 16 if a == 0 else 8) for a in range(8)]
    for h in range(PEER_HEADS):
        q1 = q_ref[:, h * 2 * PEER_HALF: h * 2 * PEER_HALF + PEER_HALF]
        q2 = q_ref[:, h * 2 * PEER_HALF + PEER_HALF: (h + 1) * 2 * PEER_HALF]
        s1 = lax.dot_general(k1, q1, nt, precision=HI, preferred_element_type=F32)
        s2 = lax.dot_general(k2, q2, nt, precision=HI, preferred_element_type=F32)
        v1, i1 = _top16_rows(s1, key_iota)
        v2, i2 = _top16_rows(s2, key_iota)
        cands, ids, orders = [], [], []
        for a, rows in blocks:
            bi = lax.broadcasted_iota(I32, (rows, tt), 0)
            ok = (bi + 1) * (a + 1) <= PEER_TOPK
            cands.append(jnp.where(ok, v1[a:a + 1, :] + v2[0:rows, :], -jnp.inf))
            ids.append(i1[a:a + 1, :] * PEER_NKEYS + i2[0:rows, :])
            orders.append(a * PEER_TOPK + bi)
        ai = lax.broadcasted_iota(I32, (8, tt), 0) + 8
        cands.append(v1[8:16, :] + v2[0:1, :])
        ids.append(i1[8:16, :] * PEER_NKEYS + i2[0:1, :])
        orders.append(ai * PEER_TOPK)
        cand = jnp.concatenate(cands, axis=0)
        cid = jnp.concatenate(ids, axis=0)
        order = jnp.concatenate(orders, axis=0)
        ts, tpos = _top16_rows(cand, order)
        experts = jnp.zeros((PEER_TOPK, tt), I32)
        out_iota = lax.broadcasted_iota(I32, (PEER_TOPK, tt), 0)
        for rnk in range(PEER_TOPK):
            sel = jnp.max(jnp.where(order == tpos[rnk:rnk + 1, :], cid, -1), axis=0, keepdims=True)
            experts = jnp.where(out_iota == rnk, sel, experts)
        ex = jnp.exp(ts - ts[0:1, :])
        gate = ex / jnp.sum(ex, axis=0, keepdims=True)
        e_ref[h * PEER_TOPK:(h + 1) * PEER_TOPK, :] = experts
        g_ref[h * PEER_TOPK:(h + 1) * PEER_TOPK, :] = gate


def peer_topk(q, k1, k2, tt=256):
    t = q.shape[0]
    tt = min(tt, t)
    ne = PEER_HEADS * PEER_TOPK
    keys = pl.BlockSpec((PEER_NKEYS, PEER_HALF), lambda i: (0, 0))
    out = pl.BlockSpec((ne, tt), lambda i: (0, i))
    return pl.pallas_call(
        _peer_topk_kernel,
        grid=(t // tt,),
        in_specs=[pl.BlockSpec((tt, q.shape[1]), lambda i: (i, 0)), keys, keys],
        out_specs=[out, out],
        out_shape=[jax.ShapeDtypeStruct((ne, t), I32), jax.ShapeDtypeStruct((ne, t), F32)],
        compiler_params=_cparams(1),
        name="peer_topk",
    )(q, k1, k2)


def _gelu_exact(x):
    return 0.5 * x * (1.0 + lax.erf(x * (2.0 ** -0.5)))


def _peer_dense_kernel(i1_ref, i2_ref, g_ref, xn_ref, ut_ref, v_ref, out_ref, gate_ref, *, tm, eb):
    j = pl.program_id(1)
    nk = PEER_NKEYS

    @pl.when(j == 0)
    def _():
        out_ref[...] = jnp.zeros_like(out_ref)
        key = lax.broadcasted_iota(I32, (nk, nk), 0)
        nt = (((1,), (1,)), ((), ()))

        def build(tq, carry):
            for k in range(PEER_BUILD_UNROLL):
                t = tq * PEER_BUILD_UNROLL + k
                a_t = jnp.where(key == i1_ref[pl.ds(t, 1), :], 1.0, 0.0).astype(BF16)
                b_t = jnp.where(key == i2_ref[pl.ds(t, 1), :], g_ref[pl.ds(t, 1), :],
                                0.0).astype(BF16)
                g_t = lax.dot_general(a_t, b_t, nt, preferred_element_type=F32)
                gate_ref[pl.ds(t, nk, stride=tm), :] = g_t
            return carry

        lax.fori_loop(0, tm // PEER_BUILD_UNROLL, build, 0)

    act = jnp.dot(xn_ref[...], ut_ref[...], preferred_element_type=F32)
    nb = eb // nk
    z = [(_gelu_exact(act[:, b * nk:(b + 1) * nk])
          * gate_ref[pl.ds(pl.multiple_of((j * nb + b) * tm, tm), tm), :]).astype(BF16)
         for b in range(nb)]
    out_ref[...] += jnp.dot(jnp.concatenate(z, axis=1), v_ref[...], preferred_element_type=F32)


def peer_dense(xn, expert_t, gate_t, u_tab, v_tab, tm=256, eb=1024):
    t, d = xn.shape
    tm = min(tm, t)
    ns = expert_t.shape[0]
    i1 = (expert_t // PEER_NKEYS).T
    i2 = (expert_t % PEER_NKEYS).T
    g = gate_t.T
    slot = pl.BlockSpec((tm, ns), lambda i, j: (i, 0))
    row = pl.BlockSpec((tm, d), lambda i, j: (i, 0))
    return pl.pallas_call(
        functools.partial(_peer_dense_kernel, tm=tm, eb=eb),
        grid=(t // tm, N_EXPERTS // eb),
        in_specs=[slot, slot, slot, row,
                  pl.BlockSpec((d, eb), lambda i, j: (0, j)),
                  pl.BlockSpec((eb, d), lambda i, j: (j, 0))],
        out_specs=row,
        out_shape=jax.ShapeDtypeStruct((t, d), F32),
        scratch_shapes=[pltpu.VMEM((PEER_NKEYS * tm, PEER_NKEYS), F32)],
        compiler_params=_cparams(2),
        name="peer_dense",
    )(i1, i2, g, xn.astype(BF16), u_tab.astype(BF16).T, v_tab.astype(BF16))


def peer_layer(xn, wq, k1, k2, u_tab, v_tab):
    q = matmul(xn, wq.astype(BF16), name="peer_query")
    expert_t, gate_t = peer_topk(q, k1, k2)
    return peer_dense(xn, expert_t, gate_t, u_tab, v_tab)


def _trunk(x, parts, even_mix_norm, even_w_in, even_gate_bias, even_pool_w, even_pool_scale,
           even_w_out, odd_mix_norm, odd_fourier_w, ffn_norm, peer_wq, peer_k1, peer_k2,
           peer_u, peer_v, final_norm):
    cat = lambda xs: xs[0] if len(xs) == 1 else jnp.concatenate(xs, axis=0)

    xn = rmsnorm(x, even_mix_norm[0])
    n_main = 4 * D_A + D_B
    w_in = even_w_in[0]
    z = matmul(xn, w_in[:, :n_main].astype(BF16), name="in_proj")
    w_gate = jnp.pad(w_in[:, n_main:], ((0, 0), (0, LANES - N_GATES)))
    gates = matmul_hi(xn, w_gate)
    gates_t = gates[:, :N_GATES].T
    bias = even_gate_bias[0].astype(F32)
    bias_row = jnp.pad(bias, (0, LANES - N_GATES)).reshape(1, LANES)
    bias_col = bias.reshape(N_GATES, 1)
    pw = even_pool_w[0].astype(BF16)
    hs, ybs = [], []
    for base, batch, seq in parts:
        hs.append(mlstm(z, gates, gates_t, bias_row, bias_col, base, batch, seq))
        ybs.append(pool(z, pw, even_pool_scale[0], 4 * D_A, base, batch, seq))
    h2 = hs[0] if len(hs) == 1 else jnp.concatenate(hs, axis=1)
    x = outproj(h2, z, cat(ybs), even_w_out[0].astype(BF16), x)

    xn = rmsnorm(x, ffn_norm[0])
    y = peer_layer(xn, peer_wq[0], peer_k1[0], peer_k2[0], peer_u[0], peer_v[0])

    x, xn = add_rmsnorm(x, y, odd_mix_norm[0])
    cd, sd = dft_tables(FOURIER_GROUP_DIM)
    yc, ys = fourier_channels(xn, cd, sd)
    fs = []
    for base, batch, seq in parts:
        cs, ss = dft_tables(seq)
        fs.append(fourier_seq(cs, ss, yc, ys, base, batch, seq,
                              tm=min(1024, seq), tk=min(512, seq)))
    x = matmul(cat(fs), odd_fourier_w[0].astype(BF16), res=x, name="fourier_out")

    xn = rmsnorm(x, ffn_norm[1])
    y = peer_layer(xn, peer_wq[1], peer_k1[1], peer_k2[1], peer_u[1], peer_v[1])
    _, out = add_rmsnorm(x, y, final_norm)
    return out


def kernel(x_prompt, x_sample, even_mix_norm, even_w_in, even_gate_bias, even_pool_w,
           even_pool_scale, even_w_out, odd_mix_norm, odd_fourier_w, ffn_norm, peer_wq,
           peer_k1, peer_k2, peer_u, peer_v, final_norm):
    bp, sp, d = x_prompt.shape
    bs, ss, _ = x_sample.shape
    tp = bp * sp
    x = jnp.concatenate([x_prompt.reshape(tp, d), x_sample.reshape(bs * ss, d)], axis=0)
    parts = ((0, bp, sp), (tp, bs, ss))
    y = _trunk(x, parts, even_mix_norm, even_w_in, even_gate_bias, even_pool_w,
               even_pool_scale, even_w_out, odd_mix_norm, odd_fourier_w, ffn_norm,
               peer_wq, peer_k1, peer_k2, peer_u, peer_v, final_norm)
    return y[:tp].reshape(bp, sp, d), y[tp:].reshape(bs, ss, d)
```

```python
import functools
import math

import jax
import jax.numpy as jnp
from jax import lax
from jax.experimental import pallas as pl
from jax.experimental.pallas import tpu as pltpu

F32 = jnp.float32
BF16 = jnp.bfloat16
I32 = jnp.int32
HI = lax.Precision.HIGHEST

D_MODEL = 2048
EPS = 1e-6
N_HEADS = 4
HEAD_DIM = 256
D_A = N_HEADS * HEAD_DIM
D_B = D_MODEL - D_A
N_GATES = 4 * N_HEADS
CHUNK = 128
N_POOL_GROUPS = 4
POOL_GROUP_DIM = D_B // N_POOL_GROUPS
N_FOURIER_GROUPS = 4
FOURIER_GROUP_DIM = D_MODEL // N_FOURIER_GROUPS
PEER_HEADS = 8
PEER_NKEYS = 128
PEER_HALF = 128
PEER_TOPK = 16
N_EXPERTS = PEER_NKEYS * PEER_NKEYS

LANES = 128
PEER_BUILD_UNROLL = 16
VMEM_LIMIT = 56 * 1024 * 1024


def _cparams(n_axes):
    return pltpu.CompilerParams(dimension_semantics=("arbitrary",) * n_axes,
                                vmem_limit_bytes=VMEM_LIMIT)


def _rms(x, g):
    ms = jnp.mean(x * x, axis=-1, keepdims=True)
    return (x * lax.rsqrt(ms + EPS)) * g


def _norm_kernel(a_ref, g_ref, xn_ref):
    xn_ref[...] = _rms(a_ref[...], g_ref[...])


def _addnorm_kernel(a_ref, b_ref, g_ref, x_ref, xn_ref):
    x = a_ref[...] + b_ref[...]
    x_ref[...] = x
    xn_ref[...] = _rms(x, g_ref[...])


def rmsnorm(a, g, tm=256):
    t, d = a.shape
    tm = min(tm, t)
    row = pl.BlockSpec((tm, d), lambda i: (i, 0))
    return pl.pallas_call(
        _norm_kernel,
        grid=(t // tm,),
        in_specs=[row, pl.BlockSpec((1, d), lambda i: (0, 0))],
        out_specs=row,
        out_shape=jax.ShapeDtypeStruct((t, d), F32),
        compiler_params=_cparams(1),
        name="rmsnorm",
    )(a, g.reshape(1, d))


def add_rmsnorm(a, b, g, tm=256):
    t, d = a.shape
    tm = min(tm, t)
    row = pl.BlockSpec((tm, d), lambda i: (i, 0))
    return pl.pallas_call(
        _addnorm_kernel,
        grid=(t // tm,),
        in_specs=[row, row, pl.BlockSpec((1, d), lambda i: (0, 0))],
        out_specs=[row, row],
        out_shape=[jax.ShapeDtypeStruct((t, d), F32)] * 2,
        compiler_params=_cparams(1),
        name="add_rmsnorm",
    )(a, b, g.reshape(1, d))


def _mm_kernel(a_ref, w_ref, o_ref):
    o_ref[...] = jnp.dot(a_ref[...].astype(BF16), w_ref[...],
                         preferred_element_type=F32).astype(o_ref.dtype)


def _mm_res_kernel(a_ref, w_ref, r_ref, o_ref):
    acc = jnp.dot(a_ref[...].astype(BF16), w_ref[...], preferred_element_type=F32)
    o_ref[...] = (acc + r_ref[...]).astype(o_ref.dtype)


def _mm_hi_kernel(a_ref, w_ref, o_ref):
    o_ref[...] = jnp.dot(a_ref[...], w_ref[...], precision=HI,
                         preferred_element_type=F32)


def matmul(a, w, res=None, tm=512, tn=512, out_dtype=F32, name="matmul"):
    m, k = a.shape
    n = w.shape[1]
    tm = min(tm, m)
    tn = min(tn, n)
    in_specs = [pl.BlockSpec((tm, k), lambda i, j: (i, 0)),
                pl.BlockSpec((k, tn), lambda i, j: (0, j))]
    args = [a, w]
    body = _mm_kernel
    if res is not None:
        in_specs.append(pl.BlockSpec((tm, tn), lambda i, j: (i, j)))
        args.append(res)
        body = _mm_res_kernel
    return pl.pallas_call(
        body,
        grid=(m // tm, n // tn),
        in_specs=in_specs,
        out_specs=pl.BlockSpec((tm, tn), lambda i, j: (i, j)),
        out_shape=jax.ShapeDtypeStruct((m, n), out_dtype),
        compiler_params=_cparams(2),
        name=name,
    )(*args)


def matmul_hi(a, w, tm=512):
    m, k = a.shape
    n = w.shape[1]
    tm = min(tm, m)
    return pl.pallas_call(
        _mm_hi_kernel,
        grid=(m // tm,),
        in_specs=[pl.BlockSpec((tm, k), lambda i: (i, 0)),
                  pl.BlockSpec((k, n), lambda i: (0, 0))],
        out_specs=pl.BlockSpec((tm, n), lambda i: (i, 0)),
        out_shape=jax.ShapeDtypeStruct((m, n), F32),
        compiler_params=_cparams(1),
        name="matmul_hi",
    )(a, w)


def _log_sigmoid(x):
    return jnp.minimum(x, 0.0) - jnp.log(1.0 + jnp.exp(-jnp.abs(x)))


def _mlstm_kernel(q_ref, k_ref, v_ref, gc_ref, gr_ref, bc_ref, br_ref, h_ref,
                  c_ref, n_ref, m_ref):
    d = pl.program_id(0)
    hh = pl.program_id(2)
    c = pl.program_id(3)
    L = CHUNK

    @pl.when(c == 0)
    def _():
        c_ref[...] = jnp.zeros_like(c_ref)
        n_ref[...] = jnp.zeros_like(n_ref)
        m_ref[...] = jnp.zeros_like(m_ref)

    q = q_ref[...] * (HEAD_DIM ** -0.5)
    k = k_ref[...]
    v = v_ref[...]
    gcol = gc_ref[...] + bc_ref[...]
    grow = gr_ref[...] + br_ref[...]
    li_idx = 2 * d * N_HEADS + hh
    lf_idx = li_idx + N_HEADS
    lane = lax.broadcasted_iota(I32, gcol.shape, 1)
    sub = lax.broadcasted_iota(I32, grow.shape, 0)
    li_col = jnp.sum(jnp.where(lane == li_idx, gcol, 0.0), axis=1, keepdims=True)
    gf_col = jnp.sum(jnp.where(lane == lf_idx, gcol, 0.0), axis=1, keepdims=True)
    li_row = jnp.sum(jnp.where(sub == li_idx, grow, 0.0), axis=0, keepdims=True)
    gf_row = jnp.sum(jnp.where(sub == lf_idx, grow, 0.0), axis=0, keepdims=True)
    lf_col = _log_sigmoid(gf_col)
    lf_row = _log_sigmoid(gf_row)

    r = lax.broadcasted_iota(I32, (L, L), 0)
    cc = lax.broadcasted_iota(I32, (L, L), 1)
    sign = 1 - 2 * d
    mask = (cc - r) * sign <= 0
    mask_t = (r - cc) * sign <= 0
    b_col = jnp.sum(jnp.where(mask, lf_row, 0.0), axis=1, keepdims=True)
    b_row = jnp.sum(jnp.where(mask_t, lf_col, 0.0), axis=0, keepdims=True)
    logw = jnp.where(mask, b_col - b_row + li_row, -jnp.inf)

    m_prev = m_ref[:, 0:1]
    inter = b_col + m_prev
    m_t = jnp.maximum(inter, jnp.max(logw, axis=1, keepdims=True))
    w = jnp.exp(logw - m_t)
    a_inter = jnp.exp(inter - m_t)
    qb = q.astype(BF16)
    vb = v.astype(BF16)
    s = lax.dot_general(qb, k.astype(BF16), (((1,), (1,)), ((), ())),
                        preferred_element_type=F32) * w
    cm = c_ref[...]
    num = (jnp.dot(s.astype(BF16), vb, preferred_element_type=F32)
           + a_inter * jnp.dot(qb, cm.astype(BF16), preferred_element_type=F32))
    den = (jnp.sum(s, axis=1, keepdims=True)
           + a_inter * jnp.sum(q * n_ref[...], axis=1, keepdims=True))
    h_ref[...] = num / jnp.maximum(jnp.abs(den), jnp.exp(-m_t))

    b_last = jnp.sum(lf_row, axis=1, keepdims=True)
    logu = b_last - b_col + li_col
    m_new = jnp.maximum(b_last + m_prev, jnp.max(logu, axis=0, keepdims=True))
    u = jnp.exp(logu - m_new)
    decay = jnp.exp(b_last + m_prev - m_new)
    uk = u * k
    c_ref[...] = decay * cm + lax.dot_general(
        uk.astype(BF16), vb, (((0,), (0,)), ((), ())), preferred_element_type=F32)
    n_ref[...] = decay * n_ref[...] + jnp.sum(uk, axis=0, keepdims=True)
    m_ref[...] = jnp.broadcast_to(m_new, m_ref.shape)


def mlstm(z, gates, gates_t, bias_row, bias_col, base_tok, batch, seq):
    nc = seq // CHUNK
    base = base_tok // CHUNK

    def blk(dd, b, c):
        return base + b * nc + c + dd * (nc - 1 - 2 * c)

    def sec(which):
        return pl.BlockSpec((CHUNK, HEAD_DIM),
                            lambda dd, b, h, c: (blk(dd, b, c), which * N_HEADS + h))

    return pl.pallas_call(
        _mlstm_kernel,
        grid=(2, batch, N_HEADS, nc),
        in_specs=[sec(0), sec(1), sec(2),
                  pl.BlockSpec((CHUNK, LANES), lambda dd, b, h, c: (blk(dd, b, c), 0)),
                  pl.BlockSpec((N_GATES, CHUNK), lambda dd, b, h, c: (0, blk(dd, b, c))),
                  pl.BlockSpec((1, LANES), lambda dd, b, h, c: (0, 0)),
                  pl.BlockSpec((N_GATES, 1), lambda dd, b, h, c: (0, 0))],
        out_specs=pl.BlockSpec((None, CHUNK, HEAD_DIM),
                               lambda dd, b, h, c: (dd, blk(dd, b, c) - base, h)),
        out_shape=jax.ShapeDtypeStruct((2, batch * seq, D_A), F32),
        scratch_shapes=[pltpu.VMEM((HEAD_DIM, HEAD_DIM), F32),
                        pltpu.VMEM((1, HEAD_DIM), F32),
                        pltpu.VMEM((1, LANES), F32)],
        compiler_params=_cparams(4),
        name="mlstm",
    )(z, z, z, gates, gates_t, bias_row, bias_col)


def _pool_kernel(prev_ref, mid_ref, next_ref, pw_ref, sc_ref, o_ref, *, seq):
    i = pl.program_id(1)
    g = pl.program_id(2)
    half = jnp.left_shift(1, g)
    T = CHUNK
    p = i * T + lax.broadcasted_iota(I32, (T, T), 0)
    lo = jnp.maximum(p - half, 0)
    hi = jnp.minimum(p + half, seq)
    acc = jnp.zeros((T, POOL_GROUP_DIM), F32)
    for off, ref in ((-T, prev_ref), (0, mid_ref), (T, next_ref)):
        pj = i * T + off + lax.broadcasted_iota(I32, (T, T), 1)
        band = jnp.where((pj >= lo) & (pj < hi), 1.0, 0.0).astype(BF16)
        u = ref[...]
        u_hi = u.astype(BF16)
        u_lo = (u - u_hi.astype(F32)).astype(BF16)
        acc = acc + (jnp.dot(band, u_hi, preferred_element_type=F32)
                     + jnp.dot(band, u_lo, preferred_element_type=F32))
    cnt = (hi - lo)[:, 0:1].astype(F32)
    pooled = acc / cnt - mid_ref[...]
    y = jnp.dot(pooled.astype(BF16), pw_ref[...], preferred_element_type=F32)
    o_ref[...] = y * sc_ref[...]


def pool(z, pool_w, pool_scale, col0, base_tok, batch, seq):
    nt = seq // CHUNK
    base = base_tok // CHUNK
    cb = col0 // POOL_GROUP_DIM

    def spec(shift):
        return pl.BlockSpec(
            (CHUNK, POOL_GROUP_DIM),
            lambda b, i, g: (base + b * nt + jnp.clip(i + shift, 0, nt - 1), cb + g))

    return pl.pallas_call(
        functools.partial(_pool_kernel, seq=seq),
        grid=(batch, nt, N_POOL_GROUPS),
        in_specs=[spec(-1), spec(0), spec(1),
                  pl.BlockSpec((None, POOL_GROUP_DIM, POOL_GROUP_DIM), lambda b, i, g: (g, 0, 0)),
                  pl.BlockSpec((1, POOL_GROUP_DIM), lambda b, i, g: (0, g))],
        out_specs=pl.BlockSpec((CHUNK, POOL_GROUP_DIM), lambda b, i, g: (b * nt + i, g)),
        out_shape=jax.ShapeDtypeStruct((batch * seq, D_B), F32),
        compiler_params=_cparams(3),
        name="pool",
    )(z, z, z, pool_w, pool_scale.reshape(1, D_B))


def _outproj_kernel(hf_ref, hb_ref, o_ref, yb_ref, wa_ref, wb_ref, x_ref, out_ref):
    ha = (hf_ref[...] + hb_ref[...]) * jax.nn.sigmoid(o_ref[...])
    acc = jnp.dot(ha.astype(BF16), wa_ref[...], preferred_element_type=F32)
    acc = acc + jnp.dot(yb_ref[...].astype(BF16), wb_ref[...], preferred_element_type=F32)
    out_ref[...] = acc + x_ref[...]


def outproj(h2, z, yb, w_out, x, tm=512, tn=512):
    t = x.shape[0]
    tm = min(tm, t)
    ocb = 3 * D_A // D_A
    return pl.pallas_call(
        _outproj_kernel,
        grid=(t // tm, D_MODEL // tn),
        in_specs=[pl.BlockSpec((None, tm, D_A), lambda i, j: (0, i, 0)),
                  pl.BlockSpec((None, tm, D_A), lambda i, j: (1, i, 0)),
                  pl.BlockSpec((tm, D_A), lambda i, j: (i, ocb)),
                  pl.BlockSpec((tm, D_B), lambda i, j: (i, 0)),
                  pl.BlockSpec((D_A, tn), lambda i, j: (0, j)),
                  pl.BlockSpec((D_B, tn), lambda i, j: (D_A // D_B, j)),
                  pl.BlockSpec((tm, tn), lambda i, j: (i, j))],
        out_specs=pl.BlockSpec((tm, tn), lambda i, j: (i, j)),
        out_shape=jax.ShapeDtypeStruct((t, D_MODEL), F32),
        compiler_params=_cparams(2),
        name="outproj",
    )(h2, h2, z, yb, w_out, w_out, x)


def _dft_kernel(c_ref, s_ref, *, n):
    tr = c_ref.shape[0]
    r = pl.program_id(0) * tr + lax.broadcasted_iota(I32, (tr, n), 0)
    c = lax.broadcasted_iota(I32, (tr, n), 1)
    ang = ((r * c) & (n - 1)).astype(F32) * (2.0 * math.pi / n)
    scale = n ** -0.5
    c_ref[...] = (jnp.cos(ang) * scale).astype(BF16)
    s_ref[...] = (jnp.sin(ang) * scale).astype(BF16)


def dft_tables(n, tr=256):
    assert n & (n - 1) == 0
    blk = pl.BlockSpec((tr, n), lambda i: (i, 0))
    return pl.pallas_call(
        functools.partial(_dft_kernel, n=n),
        grid=(n // tr,),
        out_specs=[blk, blk],
        out_shape=[jax.ShapeDtypeStruct((n, n), BF16)] * 2,
        compiler_params=_cparams(1),
        name="dft_tables",
    )()


def _fourier_ch_kernel(x_ref, cd_ref, sd_ref, yc_ref, ys_ref):
    xb = x_ref[...].astype(BF16)
    yc_ref[...] = jnp.dot(xb, cd_ref[...], preferred_element_type=F32).astype(BF16)
    ys_ref[...] = jnp.dot(xb, sd_ref[...], preferred_element_type=F32).astype(BF16)


def fourier_channels(xn, cd, sd, tm=512):
    t = xn.shape[0]
    tm = min(tm, t)
    gd = FOURIER_GROUP_DIM
    tile = pl.BlockSpec((tm, gd), lambda i, g: (i, g))
    tab = pl.BlockSpec((gd, gd), lambda i, g: (0, 0))
    return pl.pallas_call(
        _fourier_ch_kernel,
        grid=(t // tm, N_FOURIER_GROUPS),
        in_specs=[tile, tab, tab],
        out_specs=[tile, tile],
        out_shape=[jax.ShapeDtypeStruct((t, D_MODEL), BF16)] * 2,
        compiler_params=_cparams(2),
        name="fourier_channels",
    )(xn, cd, sd)


def _fourier_seq_kernel(cs_ref, ss_ref, yc_ref, ys_ref, o_ref, acc_ref):
    kk = pl.program_id(3)

    @pl.when(kk == 0)
    def _():
        acc_ref[...] = jnp.zeros_like(acc_ref)

    acc_ref[...] += (jnp.dot(cs_ref[...], yc_ref[...], preferred_element_type=F32)
                     - jnp.dot(ss_ref[...], ys_ref[...], preferred_element_type=F32))

    @pl.when(kk == pl.num_programs(3) - 1)
    def _():
        o_ref[...] = acc_ref[...].astype(o_ref.dtype)


def fourier_seq(cs, ss, yc, ys, base_tok, batch, seq, tm=1024, tn=1024, tk=512):
    nk = seq // tk
    base = base_tok // tk
    a_spec = pl.BlockSpec((tm, tk), lambda b, i, j, k: (i, k))
    y_spec = pl.BlockSpec((tk, tn), lambda b, i, j, k: (base + b * nk + k, j))
    return pl.pallas_call(
        _fourier_seq_kernel,
        grid=(batch, seq // tm, D_MODEL // tn, nk),
        in_specs=[a_spec, a_spec, y_spec, y_spec],
        out_specs=pl.BlockSpec((tm, tn), lambda b, i, j, k: (b * (seq // tm) + i, j)),
        out_shape=jax.ShapeDtypeStruct((batch * seq, D_MODEL), BF16),
        scratch_shapes=[pltpu.VMEM((tm, tn), F32)],
        compiler_params=_cparams(4),
        name="fourier_seq",
    )(cs, ss, yc, ys)


def _top16_rows(s, row_key):
    n = s.shape[1]
    out_iota = lax.broadcasted_iota(I32, (PEER_TOPK, n), 0)
    vals = jnp.zeros((PEER_TOPK, n), F32)
    keys = jnp.zeros((PEER_TOPK, n), I32)
    big = jnp.int32(1 << 30)
    for rnk in range(PEER_TOPK):
        mx = jnp.max(s, axis=0, keepdims=True)
        kmin = jnp.min(jnp.where(s == mx, row_key, big), axis=0, keepdims=True)
        vals = jnp.where(out_iota == rnk, mx, vals)
        keys = jnp.where(out_iota == rnk, kmin, keys)
        s = jnp.where(row_key == kmin, -jnp.inf, s)
    return vals, keys


def _peer_topk_kernel(q_ref, k1_ref, k2_ref, e_ref, g_ref):
    tt = q_ref.shape[0]
    k1 = k1_ref[...]
    k2 = k2_ref[...]
    key_iota = lax.broadcasted_iota(I32, (PEER_NKEYS, tt), 0)
    nt = (((1,), (1,)), ((), ()))
    blocks = [(a, 16 if a == 0 else 8) for a in range(8)]
    for h in range(PEER_HEADS):
        q1 = q_ref[:, h * 2 * PEER_HALF: h * 2 * PEER_HALF + PEER_HALF]
        q2 = q_ref[:, h * 2 * PEER_HALF + PEER_HALF: (h + 1) * 2 * PEER_HALF]
        s1 = lax.dot_general(k1, q1, nt, precision=HI, preferred_element_type=F32)
        s2 = lax.dot_general(k2, q2, nt, precision=HI, preferred_element_type=F32)
        v1, i1 = _top16_rows(s1, key_iota)
        v2, i2 = _top16_rows(s2, key_iota)
        cands, ids, orders = [], [], []
        for a, rows in blocks:
            bi = lax.broadcasted_iota(I32, (rows, tt), 0)
            ok = (bi + 1) * (a + 1) <= PEER_TOPK
            cands.append(jnp.where(ok, v1[a:a + 1, :] + v2[0:rows, :], -jnp.inf))
            ids.append(i1[a:a + 1, :] * PEER_NKEYS + i2[0:rows, :])
            orders.append(a * PEER_TOPK + bi)
        ai = lax.broadcasted_iota(I32, (8, tt), 0) + 8
        cands.append(v1[8:16, :] + v2[0:1, :])
        ids.append(i1[8:16, :] * PEER_NKEYS + i2[0:1, :])
        orders.append(ai * PEER_TOPK)
        cand = jnp.concatenate(cands, axis=0)
        cid = jnp.concatenate(ids, axis=0)
        order = jnp.concatenate(orders, axis=0)
        ts, tpos = _top16_rows(cand, order)
        experts = jnp.zeros((PEER_TOPK, tt), I32)
        out_iota = lax.broadcasted_iota(I32, (PEER_TOPK, tt), 0)
        for rnk in range(PEER_TOPK):
            sel = jnp.max(jnp.where(order == tpos[rnk:rnk + 1, :], cid, -1), axis=0, keepdims=True)
            experts = jnp.where(out_iota == rnk, sel, experts)
        ex = jnp.exp(ts - ts[0:1, :])
        gate = ex / jnp.sum(ex, axis=0, keepdims=True)
        e_ref[h * PEER_TOPK:(h + 1) * PEER_TOPK, :] = experts
        g_ref[h * PEER_TOPK:(h + 1) * PEER_TOPK, :] = gate


def peer_topk(q, k1, k2, tt=256):
    t = q.shape[0]
    tt = min(tt, t)
    ne = PEER_HEADS * PEER_TOPK
    keys = pl.BlockSpec((PEER_NKEYS, PEER_HALF), lambda i: (0, 0))
    out = pl.BlockSpec((ne, tt), lambda i: (0, i))
    return pl.pallas_call(
        _peer_topk_kernel,
        grid=(t // tt,),
        in_specs=[pl.BlockSpec((tt, q.shape[1]), lambda i: (i, 0)), keys, keys],
        out_specs=[out, out],
        out_shape=[jax.ShapeDtypeStruct((ne, t), I32), jax.ShapeDtypeStruct((ne, t), F32)],
        compiler_params=_cparams(1),
        name="peer_topk",
    )(q, k1, k2)


def _gelu_exact(x):
    return 0.5 * x * (1.0 + lax.erf(x * (2.0 ** -0.5)))


def _peer_dense_kernel(i1_ref, i2_ref, g_ref, xn_ref, ut_ref, v_ref, out_ref, gate_ref, *, tm, eb):
    j = pl.program_id(1)
    nk = PEER_NKEYS

    @pl.when(j == 0)
    def _():
        out_ref[...] = jnp.zeros_like(out_ref)
        key = lax.broadcasted_iota(I32, (nk, nk), 0)
        nt = (((1,), (1,)), ((), ()))

        def build(tq, carry):
            for k in range(PEER_BUILD_UNROLL):
                t = tq * PEER_BUILD_UNROLL + k
                a_t = jnp.where(key == i1_ref[pl.ds(t, 1), :], 1.0, 0.0).astype(BF16)
                b_t = jnp.where(key == i2_ref[pl.ds(t, 1), :], g_ref[pl.ds(t, 1), :],
                                0.0).astype(BF16)
                g_t = lax.dot_general(a_t, b_t, nt, preferred_element_type=F32)
                gate_ref[pl.ds(t, nk, stride=tm), :] = g_t
            return carry

        lax.fori_loop(0, tm // PEER_BUILD_UNROLL, build, 0)

    act = jnp.dot(xn_ref[...], ut_ref[...], preferred_element_type=F32)
    nb = eb // nk
    z = [(_gelu_exact(act[:, b * nk:(b + 1) * nk])
          * gate_ref[pl.ds(pl.multiple_of((j * nb + b) * tm, tm), tm), :]).astype(BF16)
         for b in range(nb)]
    out_ref[...] += jnp.dot(jnp.concatenate(z, axis=1), v_ref[...], preferred_element_type=F32)


def peer_dense(xn, expert_t, gate_t, u_tab, v_tab, tm=512, eb=512):
    t, d = xn.shape
    tm = min(tm, t)
    ns = expert_t.shape[0]
    i1 = (expert_t // PEER_NKEYS).T
    i2 = (expert_t % PEER_NKEYS).T
    g = gate_t.T
    slot = pl.BlockSpec((tm, ns), lambda i, j: (i, 0))
    row = pl.BlockSpec((tm, d), lambda i, j: (i, 0), pipeline_mode=pl.Buffered(1))
    return pl.pallas_call(
        functools.partial(_peer_dense_kernel, tm=tm, eb=eb),
        grid=(t // tm, N_EXPERTS // eb),
        in_specs=[slot, slot, slot, row,
                  pl.BlockSpec((d, eb), lambda i, j: (0, j)),
                  pl.BlockSpec((eb, d), lambda i, j: (j, 0))],
        out_specs=row,
        out_shape=jax.ShapeDtypeStruct((t, d), F32),
        scratch_shapes=[pltpu.VMEM((PEER_NKEYS * tm, PEER_NKEYS), F32)],
        compiler_params=_cparams(2),
        name="peer_dense",
    )(i1, i2, g, xn.astype(BF16), u_tab.astype(BF16).T, v_tab.astype(BF16))


def peer_layer(xn, wq, k1, k2, u_tab, v_tab):
    q = matmul(xn, wq.astype(BF16), name="peer_query")
    expert_t, gate_t = peer_topk(q, k1, k2)
    return peer_dense(xn, expert_t, gate_t, u_tab, v_tab)


def _trunk(x, parts, even_mix_norm, even_w_in, even_gate_bias, even_pool_w, even_pool_scale,
           even_w_out, odd_mix_norm, odd_fourier_w, ffn_norm, peer_wq, peer_k1, peer_k2,
           peer_u, peer_v, final_norm):
    cat = lambda xs: xs[0] if len(xs) == 1 else jnp.concatenate(xs, axis=0)

    xn = rmsnorm(x, even_mix_norm[0])
    n_main = 4 * D_A + D_B
    w_in = even_w_in[0]
    z = matmul(xn, w_in[:, :n_main].astype(BF16), name="in_proj")
    w_gate = jnp.pad(w_in[:, n_main:], ((0, 0), (0, LANES - N_GATES)))
    gates = matmul_hi(xn, w_gate)
    gates_t = gates[:, :N_GATES].T
    bias = even_gate_bias[0].astype(F32)
    bias_row = jnp.pad(bias, (0, LANES - N_GATES)).reshape(1, LANES)
    bias_col = bias.reshape(N_GATES, 1)
    pw = even_pool_w[0].astype(BF16)
    hs, ybs = [], []
    for base, batch, seq in parts:
        hs.append(mlstm(z, gates, gates_t, bias_row, bias_col, base, batch, seq))
        ybs.append(pool(z, pw, even_pool_scale[0], 4 * D_A, base, batch, seq))
    h2 = hs[0] if len(hs) == 1 else jnp.concatenate(hs, axis=1)
    x = outproj(h2, z, cat(ybs), even_w_out[0].astype(BF16), x)

    xn = rmsnorm(x, ffn_norm[0])
    y = peer_layer(xn, peer_wq[0], peer_k1[0], peer_k2[0], peer_u[0], peer_v[0])

    x, xn = add_rmsnorm(x, y, odd_mix_norm[0])
    cd, sd = dft_tables(FOURIER_GROUP_DIM)
    yc, ys = fourier_channels(xn, cd, sd)
    fs = []
    for base, batch, seq in parts:
        cs, ss = dft_tables(seq)
        fs.append(fourier_seq(cs, ss, yc, ys, base, batch, seq,
                              tm=min(1024, seq), tk=min(512, seq)))
    x = matmul(cat(fs), odd_fourier_w[0].astype(BF16), res=x, name="fourier_out")

    xn = rmsnorm(x, ffn_norm[1])
    y = peer_layer(xn, peer_wq[1], peer_k1[1], peer_k2[1], peer_u[1], peer_v[1])
    _, out = add_rmsnorm(x, y, final_norm)
    return out


def kernel(x_prompt, x_sample, even_mix_norm, even_w_in, even_gate_bias, even_pool_w,
           even_pool_scale, even_w_out, odd_mix_norm, odd_fourier_w, ffn_norm, peer_wq,
           peer_k1, peer_k2, peer_u, peer_v, final_norm):
    bp, sp, d = x_prompt.shape
    bs, ss, _ = x_sample.shape
    tp = bp * sp
    x = jnp.concatenate([x_prompt.reshape(tp, d), x_sample.reshape(bs * ss, d)], axis=0)
    parts = ((0, bp, sp), (tp, bs, ss))
    y = _trunk(x, parts, even_mix_norm, even_w_in, even_gate_bias, even_pool_w,
               even_pool_scale, even_w_out, odd_mix_norm, odd_fourier_w, ffn_norm,
               peer_wq, peer_k1, peer_k2, peer_u, peer_v, final_norm)
    return y[:tp].reshape(bp, sp, d), y[tp:].reshape(bs, ss, d)
```

```python
import functools
import math

import jax
import jax.numpy as jnp
from jax import lax
from jax.experimental import pallas as pl
from jax.experimental.pallas import tpu as pltpu

F32 = jnp.float32
BF16 = jnp.bfloat16
I32 = jnp.int32
HI = lax.Precision.HIGHEST

D_MODEL = 2048
EPS = 1e-6
N_HEADS = 4
HEAD_DIM = 256
D_A = N_HEADS * HEAD_DIM
D_B = D_MODEL - D_A
N_GATES = 4 * N_HEADS
CHUNK = 128
MLSTM_HEADS_PER_STEP = 2
N_POOL_GROUPS = 4
POOL_GROUP_DIM = D_B // N_POOL_GROUPS
N_FOURIER_GROUPS = 4
FOURIER_GROUP_DIM = D_MODEL // N_FOURIER_GROUPS
PEER_HEADS = 8
PEER_NKEYS = 128
PEER_HALF = 128
PEER_TOPK = 16
N_EXPERTS = PEER_NKEYS * PEER_NKEYS

LANES = 128
PEER_BUILD_UNROLL = 16
VMEM_LIMIT = 56 * 1024 * 1024


def _cparams(n_axes):
    return pltpu.CompilerParams(dimension_semantics=("arbitrary",) * n_axes,
                                vmem_limit_bytes=VMEM_LIMIT)


def _rms(x, g):
    ms = jnp.mean(x * x, axis=-1, keepdims=True)
    return (x * lax.rsqrt(ms + EPS)) * g


def _norm_kernel(a_ref, g_ref, xn_ref):
    xn_ref[...] = _rms(a_ref[...], g_ref[...])


def _addnorm_kernel(a_ref, b_ref, g_ref, x_ref, xn_ref):
    x = a_ref[...] + b_ref[...]
    x_ref[...] = x
    xn_ref[...] = _rms(x, g_ref[...])


def rmsnorm(a, g, tm=256):
    t, d = a.shape
    tm = min(tm, t)
    row = pl.BlockSpec((tm, d), lambda i: (i, 0))
    return pl.pallas_call(
        _norm_kernel,
        grid=(t // tm,),
        in_specs=[row, pl.BlockSpec((1, d), lambda i: (0, 0))],
        out_specs=row,
        out_shape=jax.ShapeDtypeStruct((t, d), F32),
        compiler_params=_cparams(1),
        name="rmsnorm",
    )(a, g.reshape(1, d))


def add_rmsnorm(a, b, g, tm=256):
    t, d = a.shape
    tm = min(tm, t)
    row = pl.BlockSpec((tm, d), lambda i: (i, 0))
    return pl.pallas_call(
        _addnorm_kernel,
        grid=(t // tm,),
        in_specs=[row, row, pl.BlockSpec((1, d), lambda i: (0, 0))],
        out_specs=[row, row],
        out_shape=[jax.ShapeDtypeStruct((t, d), F32)] * 2,
        compiler_params=_cparams(1),
        name="add_rmsnorm",
    )(a, b, g.reshape(1, d))


def _addnorm_out_kernel(a_ref, b_ref, g_ref, xn_ref):
    xn_ref[...] = _rms(a_ref[...] + b_ref[...], g_ref[...])


def add_rmsnorm_part(a, b, g, base_tok, rows, tm=256):
    d = a.shape[1]
    tm = min(tm, rows)
    src = pl.BlockSpec((tm, d), lambda i: (base_tok // tm + i, 0))
    return pl.pallas_call(
        _addnorm_out_kernel,
        grid=(rows // tm,),
        in_specs=[src, src, pl.BlockSpec((1, d), lambda i: (0, 0))],
        out_specs=pl.BlockSpec((tm, d), lambda i: (i, 0)),
        out_shape=jax.ShapeDtypeStruct((rows, d), F32),
        compiler_params=_cparams(1),
        name="final_norm",
    )(a, b, g.reshape(1, d))


def _mm_kernel(a_ref, w_ref, o_ref):
    o_ref[...] = jnp.dot(a_ref[...].astype(BF16), w_ref[...],
                         preferred_element_type=F32).astype(o_ref.dtype)


def _mm_res_kernel(a_ref, w_ref, r_ref, o_ref):
    acc = jnp.dot(a_ref[...].astype(BF16), w_ref[...], preferred_element_type=F32)
    o_ref[...] = (acc + r_ref[...]).astype(o_ref.dtype)


def _mm_hi_kernel(a_ref, w_ref, o_ref):
    o_ref[...] = jnp.dot(a_ref[...], w_ref[...], precision=HI,
                         preferred_element_type=F32)


def matmul(a, w, res=None, tm=512, tn=512, out_dtype=F32, name="matmul"):
    m, k = a.shape
    n = w.shape[1]
    tm = min(tm, m)
    tn = min(tn, n)
    in_specs = [pl.BlockSpec((tm, k), lambda i, j: (i, 0)),
                pl.BlockSpec((k, tn), lambda i, j: (0, j))]
    args = [a, w]
    body = _mm_kernel
    if res is not None:
        in_specs.append(pl.BlockSpec((tm, tn), lambda i, j: (i, j)))
        args.append(res)
        body = _mm_res_kernel
    return pl.pallas_call(
        body,
        grid=(m // tm, n // tn),
        in_specs=in_specs,
        out_specs=pl.BlockSpec((tm, tn), lambda i, j: (i, j)),
        out_shape=jax.ShapeDtypeStruct((m, n), out_dtype),
        compiler_params=_cparams(2),
        name=name,
    )(*args)


def matmul_hi(a, w, tm=512):
    m, k = a.shape
    n = w.shape[1]
    tm = min(tm, m)
    return pl.pallas_call(
        _mm_hi_kernel,
        grid=(m // tm,),
        in_specs=[pl.BlockSpec((tm, k), lambda i: (i, 0)),
                  pl.BlockSpec((k, n), lambda i: (0, 0))],
        out_specs=pl.BlockSpec((tm, n), lambda i: (i, 0)),
        out_shape=jax.ShapeDtypeStruct((m, n), F32),
        compiler_params=_cparams(1),
        name="matmul_hi",
    )(a, w)


def _log_sigmoid(x):
    return jnp.minimum(x, 0.0) - jnp.log(1.0 + jnp.exp(-jnp.abs(x)))


def _mlstm_kernel(q_ref, k_ref, v_ref, gc_ref, gr_ref, bc_ref, br_ref, h_ref,
                  c_ref, n_ref, m_ref):
    d = pl.program_id(0)
    hp = pl.program_id(2)
    c = pl.program_id(3)
    L = CHUNK

    @pl.when(c == 0)
    def _():
        c_ref[...] = jnp.zeros_like(c_ref)
        n_ref[...] = jnp.zeros_like(n_ref)
        m_ref[...] = jnp.zeros_like(m_ref)

    gcol = gc_ref[...] + bc_ref[...]
    grow = gr_ref[...] + br_ref[...]
    lane = lax.broadcasted_iota(I32, gcol.shape, 1)
    sub = lax.broadcasted_iota(I32, grow.shape, 0)
    r = lax.broadcasted_iota(I32, (L, L), 0)
    cc = lax.broadcasted_iota(I32, (L, L), 1)
    sign = 1 - 2 * d
    mask = (cc - r) * sign <= 0
    mask_t = (r - cc) * sign <= 0

    for hh in range(MLSTM_HEADS_PER_STEP):
        cols = slice(hh * HEAD_DIM, (hh + 1) * HEAD_DIM)
        q = q_ref[:, cols] * (HEAD_DIM ** -0.5)
        k = k_ref[:, cols]
        v = v_ref[:, cols]
        li_idx = 2 * d * N_HEADS + hp * MLSTM_HEADS_PER_STEP + hh
        lf_idx = li_idx + N_HEADS
        li_col = jnp.sum(jnp.where(lane == li_idx, gcol, 0.0), axis=1, keepdims=True)
        gf_col = jnp.sum(jnp.where(lane == lf_idx, gcol, 0.0), axis=1, keepdims=True)
        li_row = jnp.sum(jnp.where(sub == li_idx, grow, 0.0), axis=0, keepdims=True)
        gf_row = jnp.sum(jnp.where(sub == lf_idx, grow, 0.0), axis=0, keepdims=True)
        lf_col = _log_sigmoid(gf_col)
        lf_row = _log_sigmoid(gf_row)
        b_col = jnp.sum(jnp.where(mask, lf_row, 0.0), axis=1, keepdims=True)
        b_row = jnp.sum(jnp.where(mask_t, lf_col, 0.0), axis=0, keepdims=True)
        logw = jnp.where(mask, b_col - b_row + li_row, -jnp.inf)

        m_prev = m_ref[hh, :, 0:1]
        inter = b_col + m_prev
        m_t = jnp.maximum(inter, jnp.max(logw, axis=1, keepdims=True))
        w = jnp.exp(logw - m_t)
        a_inter = jnp.exp(inter - m_t)
        qb = q.astype(BF16)
        vb = v.astype(BF16)
        s = lax.dot_general(qb, k.astype(BF16), (((1,), (1,)), ((), ())),
                            preferred_element_type=F32) * w
        cm = c_ref[hh]
        nv = n_ref[hh]
        num = (jnp.dot(s.astype(BF16), vb, preferred_element_type=F32)
               + a_inter * jnp.dot(qb, cm.astype(BF16), preferred_element_type=F32))
        den = (jnp.sum(s, axis=1, keepdims=True)
               + a_inter * jnp.sum(q * nv, axis=1, keepdims=True))
        h_ref[:, cols] = num / jnp.maximum(jnp.abs(den), jnp.exp(-m_t))

        b_last = jnp.sum(lf_row, axis=1, keepdims=True)
        logu = b_last - b_col + li_col
        m_new = jnp.maximum(b_last + m_prev, jnp.max(logu, axis=0, keepdims=True))
        u = jnp.exp(logu - m_new)
        decay = jnp.exp(b_last + m_prev - m_new)
        uk = u * k
        c_ref[hh] = decay * cm + lax.dot_general(
            uk.astype(BF16), vb, (((0,), (0,)), ((), ())), preferred_element_type=F32)
        n_ref[hh] = decay * nv + jnp.sum(uk, axis=0, keepdims=True)
        m_ref[hh] = jnp.broadcast_to(m_new, (1, LANES))


def mlstm(z, gates, gates_t, bias_row, bias_col, base_tok, batch, seq):
    nc = seq // CHUNK
    base = base_tok // CHUNK
    hps = MLSTM_HEADS_PER_STEP
    ngrp = N_HEADS // hps

    def blk(dd, b, c):
        return base + b * nc + c + dd * (nc - 1 - 2 * c)

    def sec(which):
        return pl.BlockSpec((CHUNK, hps * HEAD_DIM),
                            lambda dd, b, h, c: (blk(dd, b, c), which * ngrp + h))

    return pl.pallas_call(
        _mlstm_kernel,
        grid=(2, batch, ngrp, nc),
        in_specs=[sec(0), sec(1), sec(2),
                  pl.BlockSpec((CHUNK, LANES), lambda dd, b, h, c: (blk(dd, b, c), 0)),
                  pl.BlockSpec((N_GATES, CHUNK), lambda dd, b, h, c: (0, blk(dd, b, c))),
                  pl.BlockSpec((1, LANES), lambda dd, b, h, c: (0, 0)),
                  pl.BlockSpec((N_GATES, 1), lambda dd, b, h, c: (0, 0))],
        out_specs=pl.BlockSpec((None, CHUNK, hps * HEAD_DIM),
                               lambda dd, b, h, c: (dd, blk(dd, b, c) - base, h)),
        out_shape=jax.ShapeDtypeStruct((2, batch * seq, D_A), F32),
        scratch_shapes=[pltpu.VMEM((hps, HEAD_DIM, HEAD_DIM), F32),
                        pltpu.VMEM((hps, 1, HEAD_DIM), F32),
                        pltpu.VMEM((hps, 1, LANES), F32)],
        compiler_params=_cparams(4),
        name="mlstm",
    )(z, z, z, gates, gates_t, bias_row, bias_col)


def _pool_kernel(prev_ref, mid_ref, next_ref, pw_ref, sc_ref, o_ref, *, seq):
    i = pl.program_id(1)
    g = pl.program_id(2)
    half = jnp.left_shift(1, g)
    T = CHUNK
    p = i * T + lax.broadcasted_iota(I32, (T, T), 0)
    lo = jnp.maximum(p - half, 0)
    hi = jnp.minimum(p + half, seq)
    acc = jnp.zeros((T, POOL_GROUP_DIM), F32)
    for off, ref in ((-T, prev_ref), (0, mid_ref), (T, next_ref)):
        pj = i * T + off + lax.broadcasted_iota(I32, (T, T), 1)
        band = jnp.where((pj >= lo) & (pj < hi), 1.0, 0.0).astype(BF16)
        u = ref[...]
        u_hi = u.astype(BF16)
        u_lo = (u - u_hi.astype(F32)).astype(BF16)
        acc = acc + (jnp.dot(band, u_hi, preferred_element_type=F32)
                     + jnp.dot(band, u_lo, preferred_element_type=F32))
    cnt = (hi - lo)[:, 0:1].astype(F32)
    pooled = acc / cnt - mid_ref[...]
    y = jnp.dot(pooled.astype(BF16), pw_ref[...], preferred_element_type=F32)
    o_ref[...] = y * sc_ref[...]


def pool(z, pool_w, pool_scale, col0, base_tok, batch, seq):
    nt = seq // CHUNK
    base = base_tok // CHUNK
    cb = col0 // POOL_GROUP_DIM

    def spec(shift):
        return pl.BlockSpec(
            (CHUNK, POOL_GROUP_DIM),
            lambda b, i, g: (base + b * nt + jnp.clip(i + shift, 0, nt - 1), cb + g))

    return pl.pallas_call(
        functools.partial(_pool_kernel, seq=seq),
        grid=(batch, nt, N_POOL_GROUPS),
        in_specs=[spec(-1), spec(0), spec(1),
                  pl.BlockSpec((None, POOL_GROUP_DIM, POOL_GROUP_DIM), lambda b, i, g: (g, 0, 0)),
                  pl.BlockSpec((1, POOL_GROUP_DIM), lambda b, i, g: (0, g))],
        out_specs=pl.BlockSpec((CHUNK, POOL_GROUP_DIM), lambda b, i, g: (b * nt + i, g)),
        out_shape=jax.ShapeDtypeStruct((batch * seq, D_B), F32),
        compiler_params=_cparams(3),
        name="pool",
    )(z, z, z, pool_w, pool_scale.reshape(1, D_B))


def _outproj_kernel(hf_ref, hb_ref, o_ref, yb_ref, wa_ref, wb_ref, x_ref, out_ref):
    ha = (hf_ref[...] + hb_ref[...]) * jax.nn.sigmoid(o_ref[...])
    acc = jnp.dot(ha.astype(BF16), wa_ref[...], preferred_element_type=F32)
    acc = acc + jnp.dot(yb_ref[...].astype(BF16), wb_ref[...], preferred_element_type=F32)
    out_ref[...] = acc + x_ref[...]


def outproj(h2, z, yb, w_out, x, tm=512, tn=512):
    t = x.shape[0]
    tm = min(tm, t)
    ocb = 3 * D_A // D_A
    return pl.pallas_call(
        _outproj_kernel,
        grid=(t // tm, D_MODEL // tn),
        in_specs=[pl.BlockSpec((None, tm, D_A), lambda i, j: (0, i, 0)),
                  pl.BlockSpec((None, tm, D_A), lambda i, j: (1, i, 0)),
                  pl.BlockSpec((tm, D_A), lambda i, j: (i, ocb)),
                  pl.BlockSpec((tm, D_B), lambda i, j: (i, 0)),
                  pl.BlockSpec((D_A, tn), lambda i, j: (0, j)),
                  pl.BlockSpec((D_B, tn), lambda i, j: (D_A // D_B, j)),
                  pl.BlockSpec((tm, tn), lambda i, j: (i, j))],
        out_specs=pl.BlockSpec((tm, tn), lambda i, j: (i, j)),
        out_shape=jax.ShapeDtypeStruct((t, D_MODEL), F32),
        compiler_params=_cparams(2),
        name="outproj",
    )(h2, h2, z, yb, w_out, w_out, x)


def _dft_kernel(c_ref, s_ref, *, n):
    tr = c_ref.shape[0]
    r = pl.program_id(0) * tr + lax.broadcasted_iota(I32, (tr, n), 0)
    c = lax.broadcasted_iota(I32, (tr, n), 1)
    ang = ((r * c) & (n - 1)).astype(F32) * (2.0 * math.pi / n)
    scale = n ** -0.5
    c_ref[...] = (jnp.cos(ang) * scale).astype(BF16)
    s_ref[...] = (jnp.sin(ang) * scale).astype(BF16)


def dft_tables(n, tr=256):
    assert n & (n - 1) == 0
    blk = pl.BlockSpec((tr, n), lambda i: (i, 0))
    return pl.pallas_call(
        functools.partial(_dft_kernel, n=n),
        grid=(n // tr,),
        out_specs=[blk, blk],
        out_shape=[jax.ShapeDtypeStruct((n, n), BF16)] * 2,
        compiler_params=_cparams(1),
        name="dft_tables",
    )()


def _fourier_ch_kernel(x_ref, cd_ref, sd_ref, yc_ref, ys_ref):
    xb = x_ref[...].astype(BF16)
    yc_ref[...] = jnp.dot(xb, cd_ref[...], preferred_element_type=F32).astype(BF16)
    ys_ref[...] = jnp.dot(xb, sd_ref[...], preferred_element_type=F32).astype(BF16)


def fourier_channels(xn, cd, sd, tm=512):
    t = xn.shape[0]
    tm = min(tm, t)
    gd = FOURIER_GROUP_DIM
    tile = pl.BlockSpec((tm, gd), lambda i, g: (i, g))
    tab = pl.BlockSpec((gd, gd), lambda i, g: (0, 0))
    return pl.pallas_call(
        _fourier_ch_kernel,
        grid=(t // tm, N_FOURIER_GROUPS),
        in_specs=[tile, tab, tab],
        out_specs=[tile, tile],
        out_shape=[jax.ShapeDtypeStruct((t, D_MODEL), BF16)] * 2,
        compiler_params=_cparams(2),
        name="fourier_channels",
    )(xn, cd, sd)


def _fourier_seq_kernel(cs_ref, ss_ref, yc_ref, ys_ref, o_ref, acc_ref):
    kk = pl.program_id(3)

    @pl.when(kk == 0)
    def _():
        acc_ref[...] = jnp.zeros_like(acc_ref)

    acc_ref[...] += (jnp.dot(cs_ref[...], yc_ref[...], preferred_element_type=F32)
                     - jnp.dot(ss_ref[...], ys_ref[...], preferred_element_type=F32))

    @pl.when(kk == pl.num_programs(3) - 1)
    def _():
        o_ref[...] = acc_ref[...].astype(o_ref.dtype)


def fourier_seq(cs, ss, yc, ys, base_tok, batch, seq, tm=1024, tn=1024, tk=512):
    nk = seq // tk
    base = base_tok // tk
    a_spec = pl.BlockSpec((tm, tk), lambda b, i, j, k: (i, k))
    y_spec = pl.BlockSpec((tk, tn), lambda b, i, j, k: (base + b * nk + k, j))
    return pl.pallas_call(
        _fourier_seq_kernel,
        grid=(batch, seq // tm, D_MODEL // tn, nk),
        in_specs=[a_spec, a_spec, y_spec, y_spec],
        out_specs=pl.BlockSpec((tm, tn), lambda b, i, j, k: (b * (seq // tm) + i, j)),
        out_shape=jax.ShapeDtypeStruct((batch * seq, D_MODEL), BF16),
        scratch_shapes=[pltpu.VMEM((tm, tn), F32)],
        compiler_params=_cparams(4),
        name="fourier_seq",
    )(cs, ss, yc, ys)


def _top16_rows(s, row_key):
    n = s.shape[1]
    out_iota = lax.broadcasted_iota(I32, (PEER_TOPK, n), 0)
    vals = jnp.zeros((PEER_TOPK, n), F32)
    keys = jnp.zeros((PEER_TOPK, n), I32)
    big = jnp.int32(1 << 30)
    for rnk in range(PEER_TOPK):
        mx = jnp.max(s, axis=0, keepdims=True)
        kmin = jnp.min(jnp.where(s == mx, row_key, big), axis=0, keepdims=True)
        vals = jnp.where(out_iota == rnk, mx, vals)
        keys = jnp.where(out_iota == rnk, kmin, keys)
        s = jnp.where(row_key == kmin, -jnp.inf, s)
    return vals, keys


def _peer_topk_kernel(q_ref, k1_ref, k2_ref, e_ref, g_ref):
    tt = q_ref.shape[0]
    k1 = k1_ref[...]
    k2 = k2_ref[...]
    key_iota = lax.broadcasted_iota(I32, (PEER_NKEYS, tt), 0)
    nt = (((1,), (1,)), ((), ()))
    blocks = [(a, 16 if a == 0 else 8) for a in range(8)]
    for h in range(PEER_HEADS):
        q1 = q_ref[:, h * 2 * PEER_HALF: h * 2 * PEER_HALF + PEER_HALF]
        q2 = q_ref[:, h * 2 * PEER_HALF + PEER_HALF: (h + 1) * 2 * PEER_HALF]
        s1 = lax.dot_general(k1, q1, nt, precision=HI, preferred_element_type=F32)
        s2 = lax.dot_general(k2, q2, nt, precision=HI, preferred_element_type=F32)
        v1, i1 = _top16_rows(s1, key_iota)
        v2, i2 = _top16_rows(s2, key_iota)
        cands, ids, orders = [], [], []
        for a, rows in blocks:
            bi = lax.broadcasted_iota(I32, (rows, tt), 0)
            ok = (bi + 1) * (a + 1) <= PEER_TOPK
            cands.append(jnp.where(ok, v1[a:a + 1, :] + v2[0:rows, :], -jnp.inf))
            ids.append(i1[a:a + 1, :] * PEER_NKEYS + i2[0:rows, :])
            orders.append(a * PEER_TOPK + bi)
        ai = lax.broadcasted_iota(I32, (8, tt), 0) + 8
        cands.append(v1[8:16, :] + v2[0:1, :])
        ids.append(i1[8:16, :] * PEER_NKEYS + i2[0:1, :])
        orders.append(ai * PEER_TOPK)
        cand = jnp.concatenate(cands, axis=0)
        cid = jnp.concatenate(ids, axis=0)
        order = jnp.concatenate(orders, axis=0)
        ts, tpos = _top16_rows(cand, order)
        experts = jnp.zeros((PEER_TOPK, tt), I32)
        out_iota = lax.broadcasted_iota(I32, (PEER_TOPK, tt), 0)
        for rnk in range(PEER_TOPK):
            sel = jnp.max(jnp.where(order == tpos[rnk:rnk + 1, :], cid, -1), axis=0, keepdims=True)
            experts = jnp.where(out_iota == rnk, sel, experts)
        ex = jnp.exp(ts - ts[0:1, :])
        gate = ex / jnp.sum(ex, axis=0, keepdims=True)
        e_ref[h * PEER_TOPK:(h + 1) * PEER_TOPK, :] = experts
        g_ref[h * PEER_TOPK:(h + 1) * PEER_TOPK, :] = gate


def peer_topk(q, k1, k2, tt=256):
    t = q.shape[0]
    tt = min(tt, t)
    ne = PEER_HEADS * PEER_TOPK
    keys = pl.BlockSpec((PEER_NKEYS, PEER_HALF), lambda i: (0, 0))
    out = pl.BlockSpec((ne, tt), lambda i: (0, i))
    return pl.pallas_call(
        _peer_topk_kernel,
        grid=(t // tt,),
        in_specs=[pl.BlockSpec((tt, q.shape[1]), lambda i: (i, 0)), keys, keys],
        out_specs=[out, out],
        out_shape=[jax.ShapeDtypeStruct((ne, t), I32), jax.ShapeDtypeStruct((ne, t), F32)],
        compiler_params=_cparams(1),
        name="peer_topk",
    )(q, k1, k2)


def _gelu_exact(x):
    return 0.5 * x * (1.0 + lax.erf(x * (2.0 ** -0.5)))


def _peer_dense_kernel(i1_ref, i2_ref, g_ref, xn_ref, ut_ref, v_ref, out_ref, gate_ref, *, tm, eb):
    j = pl.program_id(1)
    nk = PEER_NKEYS

    @pl.when(j == 0)
    def _():
        out_ref[...] = jnp.zeros_like(out_ref)
        key = lax.broadcasted_iota(I32, (nk, nk), 0)
        nt = (((1,), (1,)), ((), ()))

        def build(tq, carry):
            for k in range(PEER_BUILD_UNROLL):
                t = tq * PEER_BUILD_UNROLL + k
                a_t = jnp.where(key == i1_ref[pl.ds(t, 1), :], 1.0, 0.0).astype(BF16)
                b_t = jnp.where(key == i2_ref[pl.ds(t, 1), :], g_ref[pl.ds(t, 1), :],
                                0.0).astype(BF16)
                g_t = lax.dot_general(a_t, b_t, nt, preferred_element_type=F32)
                gate_ref[pl.ds(t, nk, stride=tm), :] = g_t
            return carry

        lax.fori_loop(0, tm // PEER_BUILD_UNROLL, build, 0)

    act = jnp.dot(xn_ref[...], ut_ref[...], preferred_element_type=F32)
    nb = eb // nk
    z = [(_gelu_exact(act[:, b * nk:(b + 1) * nk])
          * gate_ref[pl.ds(pl.multiple_of((j * nb + b) * tm, tm), tm), :]).astype(BF16)
         for b in range(nb)]
    out_ref[...] += jnp.dot(jnp.concatenate(z, axis=1), v_ref[...], preferred_element_type=F32)


def peer_dense(xn, expert_t, gate_t, u_tab, v_tab, tm=512, eb=512):
    t, d = xn.shape
    tm = min(tm, t)
    ns = expert_t.shape[0]
    i1 = (expert_t // PEER_NKEYS).T
    i2 = (expert_t % PEER_NKEYS).T
    g = gate_t.T
    slot = pl.BlockSpec((tm, ns), lambda i, j: (i, 0))
    row = pl.BlockSpec((tm, d), lambda i, j: (i, 0), pipeline_mode=pl.Buffered(1))
    return pl.pallas_call(
        functools.partial(_peer_dense_kernel, tm=tm, eb=eb),
        grid=(t // tm, N_EXPERTS // eb),
        in_specs=[slot, slot, slot, row,
                  pl.BlockSpec((d, eb), lambda i, j: (0, j)),
                  pl.BlockSpec((eb, d), lambda i, j: (j, 0))],
        out_specs=row,
        out_shape=jax.ShapeDtypeStruct((t, d), F32),
        scratch_shapes=[pltpu.VMEM((PEER_NKEYS * tm, PEER_NKEYS), F32)],
        compiler_params=_cparams(2),
        name="peer_dense",
    )(i1, i2, g, xn.astype(BF16), u_tab.astype(BF16).T, v_tab.astype(BF16))


def peer_layer(xn, wq, k1, k2, u_tab, v_tab):
    q = matmul(xn, wq.astype(BF16), tn=1024, name="peer_query")
    expert_t, gate_t = peer_topk(q, k1, k2)
    return peer_dense(xn, expert_t, gate_t, u_tab, v_tab)


def _trunk(x, parts, even_mix_norm, even_w_in, even_gate_bias, even_pool_w, even_pool_scale,
           even_w_out, odd_mix_norm, odd_fourier_w, ffn_norm, peer_wq, peer_k1, peer_k2,
           peer_u, peer_v, final_norm):
    cat = lambda xs: xs[0] if len(xs) == 1 else jnp.concatenate(xs, axis=0)

    xn = rmsnorm(x, even_mix_norm[0])
    n_main = 4 * D_A + D_B
    w_in = even_w_in[0]
    z = matmul(xn, w_in[:, :n_main].astype(BF16), tn=1024, name="in_proj")
    w_gate = jnp.pad(w_in[:, n_main:], ((0, 0), (0, LANES - N_GATES)))
    gates = matmul_hi(xn, w_gate)
    gates_t = gates[:, :N_GATES].T
    bias = even_gate_bias[0].astype(F32)
    bias_row = jnp.pad(bias, (0, LANES - N_GATES)).reshape(1, LANES)
    bias_col = bias.reshape(N_GATES, 1)
    pw = even_pool_w[0].astype(BF16)
    hs, ybs = [], []
    for base, batch, seq in parts:
        hs.append(mlstm(z, gates, gates_t, bias_row, bias_col, base, batch, seq))
        ybs.append(pool(z, pw, even_pool_scale[0], 4 * D_A, base, batch, seq))
    h2 = hs[0] if len(hs) == 1 else jnp.concatenate(hs, axis=1)
    x = outproj(h2, z, cat(ybs), even_w_out[0].astype(BF16), x)

    xn = rmsnorm(x, ffn_norm[0])
    y = peer_layer(xn, peer_wq[0], peer_k1[0], peer_k2[0], peer_u[0], peer_v[0])

    x, xn = add_rmsnorm(x, y, odd_mix_norm[0])
    cd, sd = dft_tables(FOURIER_GROUP_DIM)
    yc, ys = fourier_channels(xn, cd, sd)
    fs = []
    for base, batch, seq in parts:
        cs, ss = dft_tables(seq)
        fs.append(fourier_seq(cs, ss, yc, ys, base, batch, seq,
                              tm=min(1024, seq), tk=min(512, seq)))
    x = matmul(cat(fs), odd_fourier_w[0].astype(BF16), res=x, tn=1024, name="fourier_out")

    xn = rmsnorm(x, ffn_norm[1])
    y = peer_layer(xn, peer_wq[1], peer_k1[1], peer_k2[1], peer_u[1], peer_v[1])
    return [add_rmsnorm_part(x, y, final_norm, base, batch * seq) for base, batch, seq in parts]


def kernel(x_prompt, x_sample, even_mix_norm, even_w_in, even_gate_bias, even_pool_w,
           even_pool_scale, even_w_out, odd_mix_norm, odd_fourier_w, ffn_norm, peer_wq,
           peer_k1, peer_k2, peer_u, peer_v, final_norm):
    bp, sp, d = x_prompt.shape
    bs, ss, _ = x_sample.shape
    tp = bp * sp
    x = jnp.concatenate([x_prompt.reshape(tp, d), x_sample.reshape(bs * ss, d)], axis=0)
    parts = ((0, bp, sp), (tp, bs, ss))
    yp, ys = _trunk(x, parts, even_mix_norm, even_w_in, even_gate_bias, even_pool_w,
                    even_pool_scale, even_w_out, odd_mix_norm, odd_fourier_w, ffn_norm,
                    peer_wq, peer_k1, peer_k2, peer_u, peer_v, final_norm)
    return yp.reshape(bp, sp, d), ys.reshape(bs, ss, d)
```

```python
import functools
import math

import jax
import jax.numpy as jnp
from jax import lax
from jax.experimental import pallas as pl
from jax.experimental.pallas import tpu as pltpu

F32 = jnp.float32
BF16 = jnp.bfloat16
I32 = jnp.int32
HI = lax.Precision.HIGHEST

D_MODEL = 2048
EPS = 1e-6
N_HEADS = 4
HEAD_DIM = 256
D_A = N_HEADS * HEAD_DIM
D_B = D_MODEL - D_A
N_GATES = 4 * N_HEADS
CHUNK = 128
MLSTM_HEADS_PER_STEP = 2
N_POOL_GROUPS = 4
POOL_GROUP_DIM = D_B // N_POOL_GROUPS
N_FOURIER_GROUPS = 4
FOURIER_GROUP_DIM = D_MODEL // N_FOURIER_GROUPS
PEER_HEADS = 8
PEER_NKEYS = 128
PEER_HALF = 128
PEER_TOPK = 16
N_EXPERTS = PEER_NKEYS * PEER_NKEYS

LANES = 128
PEER_BUILD_GROUP = 16
VMEM_LIMIT = 56 * 1024 * 1024


def _cparams(n_axes):
    return pltpu.CompilerParams(dimension_semantics=("arbitrary",) * n_axes,
                                vmem_limit_bytes=VMEM_LIMIT)


def _rms(x, g):
    ms = jnp.mean(x * x, axis=-1, keepdims=True)
    return (x * lax.rsqrt(ms + EPS)) * g


def _norm_kernel(a_ref, g_ref, xn_ref):
    xn_ref[...] = _rms(a_ref[...], g_ref[...])


def _addnorm_kernel(a_ref, b_ref, g_ref, x_ref, xn_ref):
    x = a_ref[...] + b_ref[...]
    x_ref[...] = x
    xn_ref[...] = _rms(x, g_ref[...])


def rmsnorm(a, g, tm=256):
    t, d = a.shape
    tm = min(tm, t)
    row = pl.BlockSpec((tm, d), lambda i: (i, 0))
    return pl.pallas_call(
        _norm_kernel,
        grid=(t // tm,),
        in_specs=[row, pl.BlockSpec((1, d), lambda i: (0, 0))],
        out_specs=row,
        out_shape=jax.ShapeDtypeStruct((t, d), F32),
        compiler_params=_cparams(1),
        name="rmsnorm",
    )(a, g.reshape(1, d))


def add_rmsnorm(a, b, g, tm=256):
    t, d = a.shape
    tm = min(tm, t)
    row = pl.BlockSpec((tm, d), lambda i: (i, 0))
    return pl.pallas_call(
        _addnorm_kernel,
        grid=(t // tm,),
        in_specs=[row, row, pl.BlockSpec((1, d), lambda i: (0, 0))],
        out_specs=[row, row],
        out_shape=[jax.ShapeDtypeStruct((t, d), F32)] * 2,
        compiler_params=_cparams(1),
        name="add_rmsnorm",
    )(a, b, g.reshape(1, d))


def _addnorm_out_kernel(a_ref, b_ref, g_ref, xn_ref):
    xn_ref[...] = _rms(a_ref[...] + b_ref[...], g_ref[...])


def add_rmsnorm_part(a, b, g, base_tok, rows, tm=256):
    d = a.shape[1]
    tm = min(tm, rows)
    src = pl.BlockSpec((tm, d), lambda i: (base_tok // tm + i, 0))
    return pl.pallas_call(
        _addnorm_out_kernel,
        grid=(rows // tm,),
        in_specs=[src, src, pl.BlockSpec((1, d), lambda i: (0, 0))],
        out_specs=pl.BlockSpec((tm, d), lambda i: (i, 0)),
        out_shape=jax.ShapeDtypeStruct((rows, d), F32),
        compiler_params=_cparams(1),
        name="final_norm",
    )(a, b, g.reshape(1, d))


def _mm_kernel(a_ref, w_ref, o_ref):
    o_ref[...] = jnp.dot(a_ref[...].astype(BF16), w_ref[...],
                         preferred_element_type=F32).astype(o_ref.dtype)


def _mm_res_kernel(a_ref, w_ref, r_ref, o_ref):
    acc = jnp.dot(a_ref[...].astype(BF16), w_ref[...], preferred_element_type=F32)
    o_ref[...] = (acc + r_ref[...]).astype(o_ref.dtype)


def _mm_hi_kernel(a_ref, w_ref, o_ref):
    o_ref[...] = jnp.dot(a_ref[...], w_ref[...], precision=HI,
                         preferred_element_type=F32)


def matmul(a, w, res=None, tm=512, tn=512, out_dtype=F32, name="matmul"):
    m, k = a.shape
    n = w.shape[1]
    tm = min(tm, m)
    tn = min(tn, n)
    in_specs = [pl.BlockSpec((tm, k), lambda i, j: (i, 0)),
                pl.BlockSpec((k, tn), lambda i, j: (0, j))]
    args = [a, w]
    body = _mm_kernel
    if res is not None:
        in_specs.append(pl.BlockSpec((tm, tn), lambda i, j: (i, j)))
        args.append(res)
        body = _mm_res_kernel
    return pl.pallas_call(
        body,
        grid=(m // tm, n // tn),
        in_specs=in_specs,
        out_specs=pl.BlockSpec((tm, tn), lambda i, j: (i, j)),
        out_shape=jax.ShapeDtypeStruct((m, n), out_dtype),
        compiler_params=_cparams(2),
        name=name,
    )(*args)


def matmul_hi(a, w, tm=512):
    m, k = a.shape
    n = w.shape[1]
    tm = min(tm, m)
    return pl.pallas_call(
        _mm_hi_kernel,
        grid=(m // tm,),
        in_specs=[pl.BlockSpec((tm, k), lambda i: (i, 0)),
                  pl.BlockSpec((k, n), lambda i: (0, 0))],
        out_specs=pl.BlockSpec((tm, n), lambda i: (i, 0)),
        out_shape=jax.ShapeDtypeStruct((m, n), F32),
        compiler_params=_cparams(1),
        name="matmul_hi",
    )(a, w)


def _log_sigmoid(x):
    return jnp.minimum(x, 0.0) - jnp.log(1.0 + jnp.exp(-jnp.abs(x)))


def _mlstm_kernel(q_ref, k_ref, v_ref, gc_ref, gr_ref, bc_ref, br_ref, h_ref,
                  c_ref, n_ref, m_ref):
    d = pl.program_id(0)
    hp = pl.program_id(2)
    c = pl.program_id(3)
    L = CHUNK

    @pl.when(c == 0)
    def _():
        c_ref[...] = jnp.zeros_like(c_ref)
        n_ref[...] = jnp.zeros_like(n_ref)
        m_ref[...] = jnp.zeros_like(m_ref)

    gcol = gc_ref[...] + bc_ref[...]
    grow = gr_ref[...] + br_ref[...]
    lane = lax.broadcasted_iota(I32, gcol.shape, 1)
    sub = lax.broadcasted_iota(I32, grow.shape, 0)
    r = lax.broadcasted_iota(I32, (L, L), 0)
    cc = lax.broadcasted_iota(I32, (L, L), 1)
    sign = 1 - 2 * d
    mask = (cc - r) * sign <= 0
    mask_t = (r - cc) * sign <= 0

    for hh in range(MLSTM_HEADS_PER_STEP):
        cols = slice(hh * HEAD_DIM, (hh + 1) * HEAD_DIM)
        q = q_ref[:, cols] * (HEAD_DIM ** -0.5)
        k = k_ref[:, cols]
        v = v_ref[:, cols]
        li_idx = 2 * d * N_HEADS + hp * MLSTM_HEADS_PER_STEP + hh
        lf_idx = li_idx + N_HEADS
        li_col = jnp.sum(jnp.where(lane == li_idx, gcol, 0.0), axis=1, keepdims=True)
        gf_col = jnp.sum(jnp.where(lane == lf_idx, gcol, 0.0), axis=1, keepdims=True)
        li_row = jnp.sum(jnp.where(sub == li_idx, grow, 0.0), axis=0, keepdims=True)
        gf_row = jnp.sum(jnp.where(sub == lf_idx, grow, 0.0), axis=0, keepdims=True)
        lf_col = _log_sigmoid(gf_col)
        lf_row = _log_sigmoid(gf_row)
        b_col = jnp.sum(jnp.where(mask, lf_row, 0.0), axis=1, keepdims=True)
        b_row = jnp.sum(jnp.where(mask_t, lf_col, 0.0), axis=0, keepdims=True)
        logw = jnp.where(mask, b_col - b_row + li_row, -jnp.inf)

        m_prev = m_ref[hh, :, 0:1]
        inter = b_col + m_prev
        m_t = jnp.maximum(inter, jnp.max(logw, axis=1, keepdims=True))
        w = jnp.exp(logw - m_t)
        a_inter = jnp.exp(inter - m_t)
        qb = q.astype(BF16)
        vb = v.astype(BF16)
        s = lax.dot_general(qb, k.astype(BF16), (((1,), (1,)), ((), ())),
                            preferred_element_type=F32) * w
        cm = c_ref[hh]
        nv = n_ref[hh]
        num = (jnp.dot(s.astype(BF16), vb, preferred_element_type=F32)
               + a_inter * jnp.dot(qb, cm.astype(BF16), preferred_element_type=F32))
        den = (jnp.sum(s, axis=1, keepdims=True)
               + a_inter * jnp.sum(q * nv, axis=1, keepdims=True))
        h_ref[:, cols] = num / jnp.maximum(jnp.abs(den), jnp.exp(-m_t))

        b_last = jnp.sum(lf_row, axis=1, keepdims=True)
        logu = b_last - b_col + li_col
        m_new = jnp.maximum(b_last + m_prev, jnp.max(logu, axis=0, keepdims=True))
        u = jnp.exp(logu - m_new)
        decay = jnp.exp(b_last + m_prev - m_new)
        uk = u * k
        c_ref[hh] = decay * cm + lax.dot_general(
            uk.astype(BF16), vb, (((0,), (0,)), ((), ())), preferred_element_type=F32)
        n_ref[hh] = decay * nv + jnp.sum(uk, axis=0, keepdims=True)
        m_ref[hh] = jnp.broadcast_to(m_new, (1, LANES))


def mlstm(z, gates, gates_t, bias_row, bias_col, base_tok, batch, seq):
    nc = seq // CHUNK
    base = base_tok // CHUNK
    hps = MLSTM_HEADS_PER_STEP
    ngrp = N_HEADS // hps

    def blk(dd, b, c):
        return base + b * nc + c + dd * (nc - 1 - 2 * c)

    def sec(which):
        return pl.BlockSpec((CHUNK, hps * HEAD_DIM),
                            lambda dd, b, h, c: (blk(dd, b, c), which * ngrp + h))

    return pl.pallas_call(
        _mlstm_kernel,
        grid=(2, batch, ngrp, nc),
        in_specs=[sec(0), sec(1), sec(2),
                  pl.BlockSpec((CHUNK, LANES), lambda dd, b, h, c: (blk(dd, b, c), 0)),
                  pl.BlockSpec((N_GATES, CHUNK), lambda dd, b, h, c: (0, blk(dd, b, c))),
                  pl.BlockSpec((1, LANES), lambda dd, b, h, c: (0, 0)),
                  pl.BlockSpec((N_GATES, 1), lambda dd, b, h, c: (0, 0))],
        out_specs=pl.BlockSpec((None, CHUNK, hps * HEAD_DIM),
                               lambda dd, b, h, c: (dd, blk(dd, b, c) - base, h)),
        out_shape=jax.ShapeDtypeStruct((2, batch * seq, D_A), F32),
        scratch_shapes=[pltpu.VMEM((hps, HEAD_DIM, HEAD_DIM), F32),
                        pltpu.VMEM((hps, 1, HEAD_DIM), F32),
                        pltpu.VMEM((hps, 1, LANES), F32)],
        compiler_params=_cparams(4),
        name="mlstm",
    )(z, z, z, gates, gates_t, bias_row, bias_col)


def _pool_kernel(prev_ref, mid_ref, next_ref, pw_ref, sc_ref, o_ref, *, seq):
    i = pl.program_id(1)
    g = pl.program_id(2)
    half = jnp.left_shift(1, g)
    T = CHUNK
    p = i * T + lax.broadcasted_iota(I32, (T, T), 0)
    lo = jnp.maximum(p - half, 0)
    hi = jnp.minimum(p + half, seq)
    acc = jnp.zeros((T, POOL_GROUP_DIM), F32)
    for off, ref in ((-T, prev_ref), (0, mid_ref), (T, next_ref)):
        pj = i * T + off + lax.broadcasted_iota(I32, (T, T), 1)
        band = jnp.where((pj >= lo) & (pj < hi), 1.0, 0.0).astype(BF16)
        u = ref[...]
        u_hi = u.astype(BF16)
        u_lo = (u - u_hi.astype(F32)).astype(BF16)
        acc = acc + (jnp.dot(band, u_hi, preferred_element_type=F32)
                     + jnp.dot(band, u_lo, preferred_element_type=F32))
    cnt = (hi - lo)[:, 0:1].astype(F32)
    pooled = acc / cnt - mid_ref[...]
    y = jnp.dot(pooled.astype(BF16), pw_ref[...], preferred_element_type=F32)
    o_ref[...] = y * sc_ref[...]


def pool(z, pool_w, pool_scale, col0, base_tok, batch, seq):
    nt = seq // CHUNK
    base = base_tok // CHUNK
    cb = col0 // POOL_GROUP_DIM

    def spec(shift):
        return pl.BlockSpec(
            (CHUNK, POOL_GROUP_DIM),
            lambda b, i, g: (base + b * nt + jnp.clip(i + shift, 0, nt - 1), cb + g))

    return pl.pallas_call(
        functools.partial(_pool_kernel, seq=seq),
        grid=(batch, nt, N_POOL_GROUPS),
        in_specs=[spec(-1), spec(0), spec(1),
                  pl.BlockSpec((None, POOL_GROUP_DIM, POOL_GROUP_DIM), lambda b, i, g: (g, 0, 0)),
                  pl.BlockSpec((1, POOL_GROUP_DIM), lambda b, i, g: (0, g))],
        out_specs=pl.BlockSpec((CHUNK, POOL_GROUP_DIM), lambda b, i, g: (b * nt + i, g)),
        out_shape=jax.ShapeDtypeStruct((batch * seq, D_B), F32),
        compiler_params=_cparams(3),
        name="pool",
    )(z, z, z, pool_w, pool_scale.reshape(1, D_B))


def _outproj_kernel(hf_ref, hb_ref, o_ref, yb_ref, wa_ref, wb_ref, x_ref, out_ref):
    ha = (hf_ref[...] + hb_ref[...]) * jax.nn.sigmoid(o_ref[...])
    acc = jnp.dot(ha.astype(BF16), wa_ref[...], preferred_element_type=F32)
    acc = acc + jnp.dot(yb_ref[...].astype(BF16), wb_ref[...], preferred_element_type=F32)
    out_ref[...] = acc + x_ref[...]


def outproj(h2, z, yb, w_out, x, tm=512, tn=512):
    t = x.shape[0]
    tm = min(tm, t)
    ocb = 3 * D_A // D_A
    return pl.pallas_call(
        _outproj_kernel,
        grid=(t // tm, D_MODEL // tn),
        in_specs=[pl.BlockSpec((None, tm, D_A), lambda i, j: (0, i, 0)),
                  pl.BlockSpec((None, tm, D_A), lambda i, j: (1, i, 0)),
                  pl.BlockSpec((tm, D_A), lambda i, j: (i, ocb)),
                  pl.BlockSpec((tm, D_B), lambda i, j: (i, 0)),
                  pl.BlockSpec((D_A, tn), lambda i, j: (0, j)),
                  pl.BlockSpec((D_B, tn), lambda i, j: (D_A // D_B, j)),
                  pl.BlockSpec((tm, tn), lambda i, j: (i, j))],
        out_specs=pl.BlockSpec((tm, tn), lambda i, j: (i, j)),
        out_shape=jax.ShapeDtypeStruct((t, D_MODEL), F32),
        compiler_params=_cparams(2),
        name="outproj",
    )(h2, h2, z, yb, w_out, w_out, x)


def _dft_kernel(c_ref, s_ref, *, n):
    tr = c_ref.shape[0]
    r = pl.program_id(0) * tr + lax.broadcasted_iota(I32, (tr, n), 0)
    c = lax.broadcasted_iota(I32, (tr, n), 1)
    ang = ((r * c) & (n - 1)).astype(F32) * (2.0 * math.pi / n)
    scale = n ** -0.5
    c_ref[...] = (jnp.cos(ang) * scale).astype(BF16)
    s_ref[...] = (jnp.sin(ang) * scale).astype(BF16)


def dft_tables(n, tr=256):
    assert n & (n - 1) == 0
    blk = pl.BlockSpec((tr, n), lambda i: (i, 0))
    return pl.pallas_call(
        functools.partial(_dft_kernel, n=n),
        grid=(n // tr,),
        out_specs=[blk, blk],
        out_shape=[jax.ShapeDtypeStruct((n, n), BF16)] * 2,
        compiler_params=_cparams(1),
        name="dft_tables",
    )()


def _fourier_ch_kernel(x_ref, cd_ref, sd_ref, yc_ref, ys_ref):
    xb = x_ref[...].astype(BF16)
    yc_ref[...] = jnp.dot(xb, cd_ref[...], preferred_element_type=F32).astype(BF16)
    ys_ref[...] = jnp.dot(xb, sd_ref[...], preferred_element_type=F32).astype(BF16)


def fourier_channels(xn, cd, sd, tm=512):
    t = xn.shape[0]
    tm = min(tm, t)
    gd = FOURIER_GROUP_DIM
    tile = pl.BlockSpec((tm, gd), lambda i, g: (i, g))
    tab = pl.BlockSpec((gd, gd), lambda i, g: (0, 0))
    return pl.pallas_call(
        _fourier_ch_kernel,
        grid=(t // tm, N_FOURIER_GROUPS),
        in_specs=[tile, tab, tab],
        out_specs=[tile, tile],
        out_shape=[jax.ShapeDtypeStruct((t, D_MODEL), BF16)] * 2,
        compiler_params=_cparams(2),
        name="fourier_channels",
    )(xn, cd, sd)


def _fourier_seq_kernel(cs_ref, ss_ref, yc_ref, ys_ref, o_ref, acc_ref):
    kk = pl.program_id(3)

    @pl.when(kk == 0)
    def _():
        acc_ref[...] = jnp.zeros_like(acc_ref)

    acc_ref[...] += (jnp.dot(cs_ref[...], yc_ref[...], preferred_element_type=F32)
                     - jnp.dot(ss_ref[...], ys_ref[...], preferred_element_type=F32))

    @pl.when(kk == pl.num_programs(3) - 1)
    def _():
        o_ref[...] = acc_ref[...].astype(o_ref.dtype)


def fourier_seq(cs, ss, yc, ys, base_tok, batch, seq, tm=1024, tn=1024, tk=512):
    nk = seq // tk
    base = base_tok // tk
    a_spec = pl.BlockSpec((tm, tk), lambda b, i, j, k: (i, k))
    y_spec = pl.BlockSpec((tk, tn), lambda b, i, j, k: (base + b * nk + k, j))
    return pl.pallas_call(
        _fourier_seq_kernel,
        grid=(batch, seq // tm, D_MODEL // tn, nk),
        in_specs=[a_spec, a_spec, y_spec, y_spec],
        out_specs=pl.BlockSpec((tm, tn), lambda b, i, j, k: (b * (seq // tm) + i, j)),
        out_shape=jax.ShapeDtypeStruct((batch * seq, D_MODEL), BF16),
        scratch_shapes=[pltpu.VMEM((tm, tn), F32)],
        compiler_params=_cparams(4),
        name="fourier_seq",
    )(cs, ss, yc, ys)


def _top16_rows(s, row_key):
    n = s.shape[1]
    out_iota = lax.broadcasted_iota(I32, (PEER_TOPK, n), 0)
    vals = jnp.zeros((PEER_TOPK, n), F32)
    keys = jnp.zeros((PEER_TOPK, n), I32)
    big = jnp.int32(1 << 30)
    for rnk in range(PEER_TOPK):
        mx = jnp.max(s, axis=0, keepdims=True)
        kmin = jnp.min(jnp.where(s == mx, row_key, big), axis=0, keepdims=True)
        vals = jnp.where(out_iota == rnk, mx, vals)
        keys = jnp.where(out_iota == rnk, kmin, keys)
        s = jnp.where(row_key == kmin, -jnp.inf, s)
    return vals, keys


def _peer_topk_kernel(q_ref, k1_ref, k2_ref, e_ref, g_ref):
    tt = q_ref.shape[0]
    k1 = k1_ref[...]
    k2 = k2_ref[...]
    key_iota = lax.broadcasted_iota(I32, (PEER_NKEYS, tt), 0)
    nt = (((1,), (1,)), ((), ()))
    blocks = [(a, 16 if a == 0 else 8) for a in range(8)]
    for h in range(PEER_HEADS):
        q1 = q_ref[:, h * 2 * PEER_HALF: h * 2 * PEER_HALF + PEER_HALF]
        q2 = q_ref[:, h * 2 * PEER_HALF + PEER_HALF: (h + 1) * 2 * PEER_HALF]
        s1 = lax.dot_general(k1, q1, nt, precision=HI, preferred_element_type=F32)
        s2 = lax.dot_general(k2, q2, nt, precision=HI, preferred_element_type=F32)
        v1, i1 = _top16_rows(s1, key_iota)
        v2, i2 = _top16_rows(s2, key_iota)
        cands, ids, orders = [], [], []
        for a, rows in blocks:
            bi = lax.broadcasted_iota(I32, (rows, tt), 0)
            ok = (bi + 1) * (a + 1) <= PEER_TOPK
            cands.append(jnp.where(ok, v1[a:a + 1, :] + v2[0:rows, :], -jnp.inf))
            ids.append(i1[a:a + 1, :] * PEER_NKEYS + i2[0:rows, :])
            orders.append(a * PEER_TOPK + bi)
        ai = lax.broadcasted_iota(I32, (8, tt), 0) + 8
        cands.append(v1[8:16, :] + v2[0:1, :])
        ids.append(i1[8:16, :] * PEER_NKEYS + i2[0:1, :])
        orders.append(ai * PEER_TOPK)
        cand = jnp.concatenate(cands, axis=0)
        cid = jnp.concatenate(ids, axis=0)
        order = jnp.concatenate(orders, axis=0)
        ts, tpos = _top16_rows(cand, order)
        experts = jnp.zeros((PEER_TOPK, tt), I32)
        out_iota = lax.broadcasted_iota(I32, (PEER_TOPK, tt), 0)
        for rnk in range(PEER_TOPK):
            sel = jnp.max(jnp.where(order == tpos[rnk:rnk + 1, :], cid, -1), axis=0, keepdims=True)
            experts = jnp.where(out_iota == rnk, sel, experts)
        ex = jnp.exp(ts - ts[0:1, :])
        gate = ex / jnp.sum(ex, axis=0, keepdims=True)
        e_ref[h * PEER_TOPK:(h + 1) * PEER_TOPK, :] = experts
        g_ref[h * PEER_TOPK:(h + 1) * PEER_TOPK, :] = gate


def peer_topk(q, k1, k2, tt=256):
    t = q.shape[0]
    tt = min(tt, t)
    ne = PEER_HEADS * PEER_TOPK
    keys = pl.BlockSpec((PEER_NKEYS, PEER_HALF), lambda i: (0, 0))
    out = pl.BlockSpec((ne, tt), lambda i: (0, i))
    return pl.pallas_call(
        _peer_topk_kernel,
        grid=(t // tt,),
        in_specs=[pl.BlockSpec((tt, q.shape[1]), lambda i: (i, 0)), keys, keys],
        out_specs=[out, out],
        out_shape=[jax.ShapeDtypeStruct((ne, t), I32), jax.ShapeDtypeStruct((ne, t), F32)],
        compiler_params=_cparams(1),
        name="peer_topk",
    )(q, k1, k2)


def _gelu_exact(x):
    return 0.5 * x * (1.0 + lax.erf(x * (2.0 ** -0.5)))


def _peer_dense_kernel(i1a_ref, i2a_ref, ga_ref, i1b_ref, i2b_ref, gb_ref, xn_ref, ut_ref, v_ref,
                       out_ref, gate_ref, stage_ref, *, tm, eb):
    i = pl.program_id(0)
    j = pl.program_id(1)
    nk = PEER_NKEYS
    grp = PEER_BUILD_GROUP
    par = i % 2
    key = lax.broadcasted_iota(I32, (nk, nk), 0)
    nt = (((1,), (1,)), ((), ()))

    def build_group(i1_ref, i2_ref, g_ref, t0, slot):
        for k in range(grp):
            t = t0 + k
            a_t = jnp.where(key == i1_ref[pl.ds(t, 1), :], 1.0, 0.0).astype(BF16)
            b_t = jnp.where(key == i2_ref[pl.ds(t, 1), :], g_ref[pl.ds(t, 1), :],
                            0.0).astype(BF16)
            g_t = lax.dot_general(a_t, b_t, nt, preferred_element_type=F32)
            stage_ref[pl.ds(k, nk, stride=grp), :] = g_t
        for r in range(nk):
            gate_ref[slot, pl.ds(pl.multiple_of(r * tm + t0, grp), grp), :] = (
                stage_ref[r * grp:(r + 1) * grp, :].astype(BF16))

    @pl.when(j == 0)
    def _():
        out_ref[...] = jnp.zeros_like(out_ref)

    @pl.when((i == 0) & (j == 0))
    def _():
        def first(tq, carry):
            build_group(i1a_ref, i2a_ref, ga_ref, pl.multiple_of(tq * grp, grp), 0)
            return carry

        lax.fori_loop(0, tm // grp, first, 0)

    build_group(i1b_ref, i2b_ref, gb_ref, pl.multiple_of((j * grp) % tm, grp), 1 - par)

    act = jnp.dot(xn_ref[...], ut_ref[...], preferred_element_type=F32)
    nb = eb // nk
    z = [(_gelu_exact(act[:, b * nk:(b + 1) * nk])
          * gate_ref[par, pl.ds(pl.multiple_of((j * nb + b) * tm, tm), tm), :].astype(F32)
          ).astype(BF16)
         for b in range(nb)]
    out_ref[...] += jnp.dot(jnp.concatenate(z, axis=1), v_ref[...], preferred_element_type=F32)


def peer_dense(xn, expert_t, gate_t, u_tab, v_tab, tm=512, eb=512):
    t, d = xn.shape
    tm = min(tm, t)
    nti = t // tm
    nsteps = N_EXPERTS // eb
    assert tm % PEER_BUILD_GROUP == 0 and tm <= nsteps * PEER_BUILD_GROUP
    ns = expert_t.shape[0]
    i1 = (expert_t // PEER_NKEYS).T
    i2 = (expert_t % PEER_NKEYS).T
    g = gate_t.T
    slot_a = pl.BlockSpec((tm, ns), lambda i, j: (0, 0))
    slot_b = pl.BlockSpec((tm, ns), lambda i, j: (jnp.minimum(i + 1, nti - 1), 0))
    row = pl.BlockSpec((tm, d), lambda i, j: (i, 0), pipeline_mode=pl.Buffered(1))
    return pl.pallas_call(
        functools.partial(_peer_dense_kernel, tm=tm, eb=eb),
        grid=(nti, nsteps),
        in_specs=[slot_a, slot_a, slot_a, slot_b, slot_b, slot_b, row,
                  pl.BlockSpec((d, eb), lambda i, j: (0, j)),
                  pl.BlockSpec((eb, d), lambda i, j: (j, 0))],
        out_specs=row,
        out_shape=jax.ShapeDtypeStruct((t, d), F32),
        scratch_shapes=[pltpu.VMEM((2, PEER_NKEYS * tm, PEER_NKEYS), BF16),
                        pltpu.VMEM((PEER_NKEYS * PEER_BUILD_GROUP, PEER_NKEYS), F32)],
        compiler_params=_cparams(2),
        name="peer_dense",
    )(i1, i2, g, i1, i2, g, xn.astype(BF16), u_tab.astype(BF16).T, v_tab.astype(BF16))


def peer_layer(xn, wq, k1, k2, u_tab, v_tab):
    q = matmul(xn, wq.astype(BF16), tn=1024, name="peer_query")
    expert_t, gate_t = peer_topk(q, k1, k2)
    return peer_dense(xn, expert_t, gate_t, u_tab, v_tab)


def _trunk(x, parts, even_mix_norm, even_w_in, even_gate_bias, even_pool_w, even_pool_scale,
           even_w_out, odd_mix_norm, odd_fourier_w, ffn_norm, peer_wq, peer_k1, peer_k2,
           peer_u, peer_v, final_norm):
    cat = lambda xs: xs[0] if len(xs) == 1 else jnp.concatenate(xs, axis=0)

    xn = rmsnorm(x, even_mix_norm[0])
    n_main = 4 * D_A + D_B
    w_in = even_w_in[0]
    z = matmul(xn, w_in[:, :n_main].astype(BF16), tn=1024, name="in_proj")
    w_gate = jnp.pad(w_in[:, n_main:], ((0, 0), (0, LANES - N_GATES)))
    gates = matmul_hi(xn, w_gate)
    gates_t = gates[:, :N_GATES].T
    bias = even_gate_bias[0].astype(F32)
    bias_row = jnp.pad(bias, (0, LANES - N_GATES)).reshape(1, LANES)
    bias_col = bias.reshape(N_GATES, 1)
    pw = even_pool_w[0].astype(BF16)
    hs, ybs = [], []
    for base, batch, seq in parts:
        hs.append(mlstm(z, gates, gates_t, bias_row, bias_col, base, batch, seq))
        ybs.append(pool(z, pw, even_pool_scale[0], 4 * D_A, base, batch, seq))
    h2 = hs[0] if len(hs) == 1 else jnp.concatenate(hs, axis=1)
    x = outproj(h2, z, cat(ybs), even_w_out[0].astype(BF16), x)

    xn = rmsnorm(x, ffn_norm[0])
    y = peer_layer(xn, peer_wq[0], peer_k1[0], peer_k2[0], peer_u[0], peer_v[0])

    x, xn = add_rmsnorm(x, y, odd_mix_norm[0])
    cd, sd = dft_tables(FOURIER_GROUP_DIM)
    yc, ys = fourier_channels(xn, cd, sd)
    fs = []
    for base, batch, seq in parts:
        cs, ss = dft_tables(seq)
        fs.append(fourier_seq(cs, ss, yc, ys, base, batch, seq,
                              tm=min(1024, seq), tk=min(512, seq)))
    x = matmul(cat(fs), odd_fourier_w[0].astype(BF16), res=x, tn=1024, name="fourier_out")

    xn = rmsnorm(x, ffn_norm[1])
    y = peer_layer(xn, peer_wq[1], peer_k1[1], peer_k2[1], peer_u[1], peer_v[1])
    return [add_rmsnorm_part(x, y, final_norm, base, batch * seq) for base, batch, seq in parts]


def kernel(x_prompt, x_sample, even_mix_norm, even_w_in, even_gate_bias, even_pool_w,
           even_pool_scale, even_w_out, odd_mix_norm, odd_fourier_w, ffn_norm, peer_wq,
           peer_k1, peer_k2, peer_u, peer_v, final_norm):
    bp, sp, d = x_prompt.shape
    bs, ss, _ = x_sample.shape
    tp = bp * sp
    x = jnp.concatenate([x_prompt.reshape(tp, d), x_sample.reshape(bs * ss, d)], axis=0)
    parts = ((0, bp, sp), (tp, bs, ss))
    yp, ys = _trunk(x, parts, even_mix_norm, even_w_in, even_gate_bias, even_pool_w,
                    even_pool_scale, even_w_out, odd_mix_norm, odd_fourier_w, ffn_norm,
                    peer_wq, peer_k1, peer_k2, peer_u, peer_v, final_norm)
    return yp.reshape(bp, sp, d), ys.reshape(bs, ss, d)
```

```python
import functools
import math

import jax
import jax.numpy as jnp
from jax import lax
from jax.experimental import pallas as pl
from jax.experimental.pallas import tpu as pltpu

F32 = jnp.float32
BF16 = jnp.bfloat16
I32 = jnp.int32
HI = lax.Precision.HIGHEST

D_MODEL = 2048
EPS = 1e-6
N_HEADS = 4
HEAD_DIM = 256
D_A = N_HEADS * HEAD_DIM
D_B = D_MODEL - D_A
N_GATES = 4 * N_HEADS
CHUNK = 128
MLSTM_HEADS_PER_STEP = 2
N_POOL_GROUPS = 4
POOL_GROUP_DIM = D_B // N_POOL_GROUPS
N_FOURIER_GROUPS = 4
FOURIER_GROUP_DIM = D_MODEL // N_FOURIER_GROUPS
PEER_HEADS = 8
PEER_NKEYS = 128
PEER_HALF = 128
PEER_TOPK = 16
N_EXPERTS = PEER_NKEYS * PEER_NKEYS
KEY_SENTINEL = 1e9

LANES = 128
PEER_BUILD_GROUP = 16
VMEM_LIMIT = 56 * 1024 * 1024


def _cparams(n_axes):
    return pltpu.CompilerParams(dimension_semantics=("arbitrary",) * n_axes,
                                vmem_limit_bytes=VMEM_LIMIT)


def _rms(x, g):
    ms = jnp.mean(x * x, axis=-1, keepdims=True)
    return (x * lax.rsqrt(ms + EPS)) * g


def _norm_kernel(a_ref, g_ref, xn_ref):
    xn_ref[...] = _rms(a_ref[...], g_ref[...])


def _addnorm_kernel(a_ref, b_ref, g_ref, x_ref, xn_ref):
    x = a_ref[...] + b_ref[...]
    x_ref[...] = x
    xn_ref[...] = _rms(x, g_ref[...])


def rmsnorm(a, g, tm=256):
    t, d = a.shape
    tm = min(tm, t)
    row = pl.BlockSpec((tm, d), lambda i: (i, 0))
    return pl.pallas_call(
        _norm_kernel,
        grid=(t // tm,),
        in_specs=[row, pl.BlockSpec((1, d), lambda i: (0, 0))],
        out_specs=row,
        out_shape=jax.ShapeDtypeStruct((t, d), F32),
        compiler_params=_cparams(1),
        name="rmsnorm",
    )(a, g.reshape(1, d))


def add_rmsnorm(a, b, g, tm=256):
    t, d = a.shape
    tm = min(tm, t)
    row = pl.BlockSpec((tm, d), lambda i: (i, 0))
    return pl.pallas_call(
        _addnorm_kernel,
        grid=(t // tm,),
        in_specs=[row, row, pl.BlockSpec((1, d), lambda i: (0, 0))],
        out_specs=[row, row],
        out_shape=[jax.ShapeDtypeStruct((t, d), F32)] * 2,
        compiler_params=_cparams(1),
        name="add_rmsnorm",
    )(a, b, g.reshape(1, d))


def _addnorm_out_kernel(a_ref, b_ref, g_ref, xn_ref):
    xn_ref[...] = _rms(a_ref[...] + b_ref[...], g_ref[...])


def add_rmsnorm_part(a, b, g, base_tok, rows, tm=256):
    d = a.shape[1]
    tm = min(tm, rows)
    src = pl.BlockSpec((tm, d), lambda i: (base_tok // tm + i, 0))
    return pl.pallas_call(
        _addnorm_out_kernel,
        grid=(rows // tm,),
        in_specs=[src, src, pl.BlockSpec((1, d), lambda i: (0, 0))],
        out_specs=pl.BlockSpec((tm, d), lambda i: (i, 0)),
        out_shape=jax.ShapeDtypeStruct((rows, d), F32),
        compiler_params=_cparams(1),
        name="final_norm",
    )(a, b, g.reshape(1, d))


def _mm_kernel(a_ref, w_ref, o_ref):
    o_ref[...] = jnp.dot(a_ref[...].astype(BF16), w_ref[...],
                         preferred_element_type=F32).astype(o_ref.dtype)


def _mm_res_kernel(a_ref, w_ref, r_ref, o_ref):
    acc = jnp.dot(a_ref[...].astype(BF16), w_ref[...], preferred_element_type=F32)
    o_ref[...] = (acc + r_ref[...]).astype(o_ref.dtype)


def _mm_hi_kernel(a_ref, w_ref, o_ref):
    o_ref[...] = jnp.dot(a_ref[...], w_ref[...], precision=HI,
                         preferred_element_type=F32)


def _mm_res_norm_kernel(a_ref, w_ref, r_ref, g_ref, o_ref, xn_ref):
    x = jnp.dot(a_ref[...].astype(BF16), w_ref[...], preferred_element_type=F32) + r_ref[...]
    o_ref[...] = x
    xn_ref[...] = _rms(x, g_ref[...])


def matmul_res_norm(a, w, res, g, tm=256, name="matmul_res_norm"):
    m, k = a.shape
    n = w.shape[1]
    tm = min(tm, m)
    row = pl.BlockSpec((tm, n), lambda i: (i, 0))
    return pl.pallas_call(
        _mm_res_norm_kernel,
        grid=(m // tm,),
        in_specs=[pl.BlockSpec((tm, k), lambda i: (i, 0)),
                  pl.BlockSpec((k, n), lambda i: (0, 0)),
                  row,
                  pl.BlockSpec((1, n), lambda i: (0, 0))],
        out_specs=[row, row],
        out_shape=[jax.ShapeDtypeStruct((m, n), F32)] * 2,
        compiler_params=_cparams(1),
        name=name,
    )(a, w, res, g.reshape(1, n))


def matmul(a, w, res=None, tm=512, tn=512, out_dtype=F32, name="matmul"):
    m, k = a.shape
    n = w.shape[1]
    tm = min(tm, m)
    tn = min(tn, n)
    in_specs = [pl.BlockSpec((tm, k), lambda i, j: (i, 0)),
                pl.BlockSpec((k, tn), lambda i, j: (0, j))]
    args = [a, w]
    body = _mm_kernel
    if res is not None:
        in_specs.append(pl.BlockSpec((tm, tn), lambda i, j: (i, j)))
        args.append(res)
        body = _mm_res_kernel
    return pl.pallas_call(
        body,
        grid=(m // tm, n // tn),
        in_specs=in_specs,
        out_specs=pl.BlockSpec((tm, tn), lambda i, j: (i, j)),
        out_shape=jax.ShapeDtypeStruct((m, n), out_dtype),
        compiler_params=_cparams(2),
        name=name,
    )(*args)


def matmul_hi(a, w, tm=512):
    m, k = a.shape
    n = w.shape[1]
    tm = min(tm, m)
    return pl.pallas_call(
        _mm_hi_kernel,
        grid=(m // tm,),
        in_specs=[pl.BlockSpec((tm, k), lambda i: (i, 0)),
                  pl.BlockSpec((k, n), lambda i: (0, 0))],
        out_specs=pl.BlockSpec((tm, n), lambda i: (i, 0)),
        out_shape=jax.ShapeDtypeStruct((m, n), F32),
        compiler_params=_cparams(1),
        name="matmul_hi",
    )(a, w)


def _log_sigmoid(x):
    return jnp.minimum(x, 0.0) - jnp.log(1.0 + jnp.exp(-jnp.abs(x)))


def _mlstm_kernel(q_ref, k_ref, v_ref, gc_ref, gr_ref, bc_ref, br_ref, h_ref,
                  c_ref, n_ref, m_ref):
    d = pl.program_id(0)
    hp = pl.program_id(2)
    c = pl.program_id(3)
    L = CHUNK

    @pl.when(c == 0)
    def _():
        c_ref[...] = jnp.zeros_like(c_ref)
        n_ref[...] = jnp.zeros_like(n_ref)
        m_ref[...] = jnp.zeros_like(m_ref)

    gcol = gc_ref[...] + bc_ref[...]
    grow = gr_ref[...] + br_ref[...]
    lane = lax.broadcasted_iota(I32, gcol.shape, 1)
    sub = lax.broadcasted_iota(I32, grow.shape, 0)
    r = lax.broadcasted_iota(I32, (L, L), 0)
    cc = lax.broadcasted_iota(I32, (L, L), 1)
    sign = 1 - 2 * d
    mask = (cc - r) * sign <= 0
    mask_t = (r - cc) * sign <= 0

    for hh in range(MLSTM_HEADS_PER_STEP):
        cols = slice(hh * HEAD_DIM, (hh + 1) * HEAD_DIM)
        q = q_ref[:, cols] * (HEAD_DIM ** -0.5)
        k = k_ref[:, cols]
        v = v_ref[:, cols]
        li_idx = 2 * d * N_HEADS + hp * MLSTM_HEADS_PER_STEP + hh
        lf_idx = li_idx + N_HEADS
        li_col = jnp.sum(jnp.where(lane == li_idx, gcol, 0.0), axis=1, keepdims=True)
        gf_col = jnp.sum(jnp.where(lane == lf_idx, gcol, 0.0), axis=1, keepdims=True)
        li_row = jnp.sum(jnp.where(sub == li_idx, grow, 0.0), axis=0, keepdims=True)
        gf_row = jnp.sum(jnp.where(sub == lf_idx, grow, 0.0), axis=0, keepdims=True)
        lf_col = _log_sigmoid(gf_col)
        lf_row = _log_sigmoid(gf_row)
        b_col = jnp.sum(jnp.where(mask, lf_row, 0.0), axis=1, keepdims=True)
        b_row = jnp.sum(jnp.where(mask_t, lf_col, 0.0), axis=0, keepdims=True)
        logw = jnp.where(mask, b_col - b_row + li_row, -jnp.inf)

        m_prev = m_ref[hh, :, 0:1]
        inter = b_col + m_prev
        m_t = jnp.maximum(inter, jnp.max(logw, axis=1, keepdims=True))
        w = jnp.exp(logw - m_t)
        a_inter = jnp.exp(inter - m_t)
        qb = q.astype(BF16)
        vb = v.astype(BF16)
        s = lax.dot_general(qb, k.astype(BF16), (((1,), (1,)), ((), ())),
                            preferred_element_type=F32) * w
        cm = c_ref[hh]
        nv = n_ref[hh]
        num = (jnp.dot(s.astype(BF16), vb, preferred_element_type=F32)
               + a_inter * jnp.dot(qb, cm.astype(BF16), preferred_element_type=F32))
        den = (jnp.sum(s, axis=1, keepdims=True)
               + a_inter * jnp.sum(q * nv, axis=1, keepdims=True))
        h_ref[:, cols] = num / jnp.maximum(jnp.abs(den), jnp.exp(-m_t))

        b_last = jnp.sum(lf_row, axis=1, keepdims=True)
        logu = b_last - b_col + li_col
        m_new = jnp.maximum(b_last + m_prev, jnp.max(logu, axis=0, keepdims=True))
        u = jnp.exp(logu - m_new)
        decay = jnp.exp(b_last + m_prev - m_new)
        uk = u * k
        c_ref[hh] = decay * cm + lax.dot_general(
            uk.astype(BF16), vb, (((0,), (0,)), ((), ())), preferred_element_type=F32)
        n_ref[hh] = decay * nv + jnp.sum(uk, axis=0, keepdims=True)
        m_ref[hh] = jnp.broadcast_to(m_new, (1, LANES))


def mlstm(z, gates, gates_t, bias_row, bias_col, base_tok, batch, seq):
    nc = seq // CHUNK
    base = base_tok // CHUNK
    hps = MLSTM_HEADS_PER_STEP
    ngrp = N_HEADS // hps

    def blk(dd, b, c):
        return base + b * nc + c + dd * (nc - 1 - 2 * c)

    def sec(which):
        return pl.BlockSpec((CHUNK, hps * HEAD_DIM),
                            lambda dd, b, h, c: (blk(dd, b, c), which * ngrp + h))

    return pl.pallas_call(
        _mlstm_kernel,
        grid=(2, batch, ngrp, nc),
        in_specs=[sec(0), sec(1), sec(2),
                  pl.BlockSpec((CHUNK, LANES), lambda dd, b, h, c: (blk(dd, b, c), 0)),
                  pl.BlockSpec((N_GATES, CHUNK), lambda dd, b, h, c: (0, blk(dd, b, c))),
                  pl.BlockSpec((1, LANES), lambda dd, b, h, c: (0, 0)),
                  pl.BlockSpec((N_GATES, 1), lambda dd, b, h, c: (0, 0))],
        out_specs=pl.BlockSpec((None, CHUNK, hps * HEAD_DIM),
                               lambda dd, b, h, c: (dd, blk(dd, b, c) - base, h)),
        out_shape=jax.ShapeDtypeStruct((2, batch * seq, D_A), F32),
        scratch_shapes=[pltpu.VMEM((hps, HEAD_DIM, HEAD_DIM), F32),
                        pltpu.VMEM((hps, 1, HEAD_DIM), F32),
                        pltpu.VMEM((hps, 1, LANES), F32)],
        compiler_params=_cparams(4),
        name="mlstm",
    )(z, z, z, gates, gates_t, bias_row, bias_col)


def _pool_kernel(prev_ref, mid_ref, next_ref, pw_ref, sc_ref, o_ref, *, seq):
    i = pl.program_id(1)
    g = pl.program_id(2)
    half = jnp.left_shift(1, g)
    T = CHUNK
    p = i * T + lax.broadcasted_iota(I32, (T, T), 0)
    lo = jnp.maximum(p - half, 0)
    hi = jnp.minimum(p + half, seq)
    acc = jnp.zeros((T, POOL_GROUP_DIM), F32)
    for off, ref in ((-T, prev_ref), (0, mid_ref), (T, next_ref)):
        pj = i * T + off + lax.broadcasted_iota(I32, (T, T), 1)
        band = jnp.where((pj >= lo) & (pj < hi), 1.0, 0.0).astype(BF16)
        u = ref[...]
        u_hi = u.astype(BF16)
        u_lo = (u - u_hi.astype(F32)).astype(BF16)
        acc = acc + (jnp.dot(band, u_hi, preferred_element_type=F32)
                     + jnp.dot(band, u_lo, preferred_element_type=F32))
    cnt = (hi - lo)[:, 0:1].astype(F32)
    pooled = acc / cnt - mid_ref[...]
    y = jnp.dot(pooled.astype(BF16), pw_ref[...], preferred_element_type=F32)
    o_ref[...] = y * sc_ref[...]


def pool(z, pool_w, pool_scale, col0, base_tok, batch, seq):
    nt = seq // CHUNK
    base = base_tok // CHUNK
    cb = col0 // POOL_GROUP_DIM

    def spec(shift):
        return pl.BlockSpec(
            (CHUNK, POOL_GROUP_DIM),
            lambda b, i, g: (base + b * nt + jnp.clip(i + shift, 0, nt - 1), cb + g))

    return pl.pallas_call(
        functools.partial(_pool_kernel, seq=seq),
        grid=(batch, nt, N_POOL_GROUPS),
        in_specs=[spec(-1), spec(0), spec(1),
                  pl.BlockSpec((None, POOL_GROUP_DIM, POOL_GROUP_DIM), lambda b, i, g: (g, 0, 0)),
                  pl.BlockSpec((1, POOL_GROUP_DIM), lambda b, i, g: (0, g))],
        out_specs=pl.BlockSpec((CHUNK, POOL_GROUP_DIM), lambda b, i, g: (b * nt + i, g)),
        out_shape=jax.ShapeDtypeStruct((batch * seq, D_B), F32),
        compiler_params=_cparams(3),
        name="pool",
    )(z, z, z, pool_w, pool_scale.reshape(1, D_B))


def _outproj_kernel(hf_ref, hb_ref, o_ref, yb_ref, wa_ref, wb_ref, x_ref, g_ref, out_ref, xn_ref):
    ha = (hf_ref[...] + hb_ref[...]) * jax.nn.sigmoid(o_ref[...])
    acc = jnp.dot(ha.astype(BF16), wa_ref[...], preferred_element_type=F32)
    acc = acc + jnp.dot(yb_ref[...].astype(BF16), wb_ref[...], preferred_element_type=F32)
    x = acc + x_ref[...]
    out_ref[...] = x
    xn_ref[...] = _rms(x, g_ref[...])


def outproj(h2, z, yb, w_out, x, g, tm=256):
    t, d = x.shape
    tm = min(tm, t)
    ocb = 3 * D_A // D_A
    row = pl.BlockSpec((tm, d), lambda i: (i, 0))
    return pl.pallas_call(
        _outproj_kernel,
        grid=(t // tm,),
        in_specs=[pl.BlockSpec((None, tm, D_A), lambda i: (0, i, 0)),
                  pl.BlockSpec((None, tm, D_A), lambda i: (1, i, 0)),
                  pl.BlockSpec((tm, D_A), lambda i: (i, ocb)),
                  pl.BlockSpec((tm, D_B), lambda i: (i, 0)),
                  pl.BlockSpec((D_A, d), lambda i: (0, 0)),
                  pl.BlockSpec((D_B, d), lambda i: (D_A // D_B, 0)),
                  row,
                  pl.BlockSpec((1, d), lambda i: (0, 0))],
        out_specs=[row, row],
        out_shape=[jax.ShapeDtypeStruct((t, d), F32)] * 2,
        compiler_params=_cparams(1),
        name="outproj",
    )(h2, h2, z, yb, w_out, w_out, x, g.reshape(1, d))


def _dft_kernel(c_ref, s_ref, *, n):
    tr = c_ref.shape[0]
    r = pl.program_id(0) * tr + lax.broadcasted_iota(I32, (tr, n), 0)
    c = lax.broadcasted_iota(I32, (tr, n), 1)
    ang = ((r * c) & (n - 1)).astype(F32) * (2.0 * math.pi / n)
    scale = n ** -0.5
    c_ref[...] = (jnp.cos(ang) * scale).astype(BF16)
    s_ref[...] = (jnp.sin(ang) * scale).astype(BF16)


def dft_tables(n, tr=256):
    assert n & (n - 1) == 0
    blk = pl.BlockSpec((tr, n), lambda i: (i, 0))
    return pl.pallas_call(
        functools.partial(_dft_kernel, n=n),
        grid=(n // tr,),
        out_specs=[blk, blk],
        out_shape=[jax.ShapeDtypeStruct((n, n), BF16)] * 2,
        compiler_params=_cparams(1),
        name="dft_tables",
    )()


def _fourier_ch_kernel(x_ref, cd_ref, sd_ref, yc_ref, ys_ref):
    xb = x_ref[...].astype(BF16)
    yc_ref[...] = jnp.dot(xb, cd_ref[...], preferred_element_type=F32).astype(BF16)
    ys_ref[...] = jnp.dot(xb, sd_ref[...], preferred_element_type=F32).astype(BF16)


def fourier_channels(xn, cd, sd, tm=512):
    t = xn.shape[0]
    tm = min(tm, t)
    gd = FOURIER_GROUP_DIM
    tile = pl.BlockSpec((tm, gd), lambda i, g: (i, g))
    tab = pl.BlockSpec((gd, gd), lambda i, g: (0, 0))
    return pl.pallas_call(
        _fourier_ch_kernel,
        grid=(t // tm, N_FOURIER_GROUPS),
        in_specs=[tile, tab, tab],
        out_specs=[tile, tile],
        out_shape=[jax.ShapeDtypeStruct((t, D_MODEL), BF16)] * 2,
        compiler_params=_cparams(2),
        name="fourier_channels",
    )(xn, cd, sd)


def _fourier_seq_kernel(cs_ref, ss_ref, yc_ref, ys_ref, o_ref, acc_ref):
    kk = pl.program_id(3)

    @pl.when(kk == 0)
    def _():
        acc_ref[...] = jnp.zeros_like(acc_ref)

    acc_ref[...] += (jnp.dot(cs_ref[...], yc_ref[...], preferred_element_type=F32)
                     - jnp.dot(ss_ref[...], ys_ref[...], preferred_element_type=F32))

    @pl.when(kk == pl.num_programs(3) - 1)
    def _():
        o_ref[...] = acc_ref[...].astype(o_ref.dtype)


def fourier_seq(cs, ss, yc, ys, base_tok, batch, seq, tm=1024, tn=1024, tk=512):
    nk = seq // tk
    base = base_tok // tk
    a_spec = pl.BlockSpec((tm, tk), lambda b, i, j, k: (i, k))
    y_spec = pl.BlockSpec((tk, tn), lambda b, i, j, k: (base + b * nk + k, j))
    return pl.pallas_call(
        _fourier_seq_kernel,
        grid=(batch, seq // tm, D_MODEL // tn, nk),
        in_specs=[a_spec, a_spec, y_spec, y_spec],
        out_specs=pl.BlockSpec((tm, tn), lambda b, i, j, k: (b * (seq // tm) + i, j)),
        out_shape=jax.ShapeDtypeStruct((batch * seq, D_MODEL), BF16),
        scratch_shapes=[pltpu.VMEM((tm, tn), F32)],
        compiler_params=_cparams(4),
        name="fourier_seq",
    )(cs, ss, yc, ys)


def _top16_rows(s, row_key):
    n = s.shape[1]
    out_iota = lax.broadcasted_iota(I32, (PEER_TOPK, n), 0)
    vals = jnp.zeros((PEER_TOPK, n), F32)
    keys = jnp.zeros((PEER_TOPK, n), F32)
    for rnk in range(PEER_TOPK):
        mx = jnp.max(s, axis=0, keepdims=True)
        kmin = jnp.min(jnp.where(s == mx, row_key, KEY_SENTINEL), axis=0, keepdims=True)
        vals = jnp.where(out_iota == rnk, mx, vals)
        keys = jnp.where(out_iota == rnk, kmin, keys)
        s = jnp.where(row_key == kmin, -jnp.inf, s)
    return vals, keys


def _peer_topk_kernel(q_ref, k1_ref, k2_ref, e_ref, g_ref):
    tt = q_ref.shape[0]
    k1 = k1_ref[...]
    k2 = k2_ref[...]
    fiota = lambda shape: lax.broadcasted_iota(I32, shape, 0).astype(F32)
    key_iota = fiota((PEER_NKEYS, tt))
    nt = (((1,), (1,)), ((), ()))
    blocks = [(a, 16 if a == 0 else 8) for a in range(8)]
    for h in range(PEER_HEADS):
        q1 = q_ref[:, h * 2 * PEER_HALF: h * 2 * PEER_HALF + PEER_HALF]
        q2 = q_ref[:, h * 2 * PEER_HALF + PEER_HALF: (h + 1) * 2 * PEER_HALF]
        s1 = lax.dot_general(k1, q1, nt, precision=HI, preferred_element_type=F32)
        s2 = lax.dot_general(k2, q2, nt, precision=HI, preferred_element_type=F32)
        v1, i1 = _top16_rows(s1, key_iota)
        v2, i2 = _top16_rows(s2, key_iota)
        cands, ids, orders = [], [], []
        for a, rows in blocks:
            bi = fiota((rows, tt))
            ok = (bi + 1.0) * (a + 1) <= PEER_TOPK
            cands.append(jnp.where(ok, v1[a:a + 1, :] + v2[0:rows, :], -jnp.inf))
            ids.append(i1[a:a + 1, :] * PEER_NKEYS + i2[0:rows, :])
            orders.append(a * PEER_TOPK + bi)
        ai = fiota((8, tt)) + 8.0
        cands.append(v1[8:16, :] + v2[0:1, :])
        ids.append(i1[8:16, :] * PEER_NKEYS + i2[0:1, :])
        orders.append(ai * PEER_TOPK)
        cand = jnp.concatenate(cands, axis=0)
        cid = jnp.concatenate(ids, axis=0)
        order = jnp.concatenate(orders, axis=0)
        ts, tpos = _top16_rows(cand, order)
        experts = jnp.zeros((PEER_TOPK, tt), F32)
        out_iota = lax.broadcasted_iota(I32, (PEER_TOPK, tt), 0)
        for rnk in range(PEER_TOPK):
            sel = jnp.max(jnp.where(order == tpos[rnk:rnk + 1, :], cid, -1.0), axis=0, keepdims=True)
            experts = jnp.where(out_iota == rnk, sel, experts)
        ex = jnp.exp(ts - ts[0:1, :])
        gate = ex / jnp.sum(ex, axis=0, keepdims=True)
        e_ref[h * PEER_TOPK:(h + 1) * PEER_TOPK, :] = experts.astype(I32)
        g_ref[h * PEER_TOPK:(h + 1) * PEER_TOPK, :] = gate


def peer_topk(q, k1, k2, tt=256):
    t = q.shape[0]
    tt = min(tt, t)
    ne = PEER_HEADS * PEER_TOPK
    keys = pl.BlockSpec((PEER_NKEYS, PEER_HALF), lambda i: (0, 0))
    out = pl.BlockSpec((ne, tt), lambda i: (0, i))
    return pl.pallas_call(
        _peer_topk_kernel,
        grid=(t // tt,),
        in_specs=[pl.BlockSpec((tt, q.shape[1]), lambda i: (i, 0)), keys, keys],
        out_specs=[out, out],
        out_shape=[jax.ShapeDtypeStruct((ne, t), I32), jax.ShapeDtypeStruct((ne, t), F32)],
        compiler_params=_cparams(1),
        name="peer_topk",
    )(q, k1, k2)


def _gelu_exact(x):
    return 0.5 * x * (1.0 + lax.erf(x * (2.0 ** -0.5)))


def _peer_dense_kernel(i1a_ref, i2a_ref, ga_ref, i1b_ref, i2b_ref, gb_ref, xn_ref, ut_ref, v_ref,
                       out_ref, gate_ref, stage_ref, *, tm, eb):
    i = pl.program_id(0)
    j = pl.program_id(1)
    nk = PEER_NKEYS
    grp = PEER_BUILD_GROUP
    par = i % 2
    key = lax.broadcasted_iota(I32, (nk, nk), 0)
    nt = (((1,), (1,)), ((), ()))

    def build_group(i1_ref, i2_ref, g_ref, t0, slot):
        for k in range(grp):
            t = t0 + k
            a_t = jnp.where(key == i1_ref[pl.ds(t, 1), :], 1.0, 0.0).astype(BF16)
            b_t = jnp.where(key == i2_ref[pl.ds(t, 1), :], g_ref[pl.ds(t, 1), :],
                            0.0).astype(BF16)
            g_t = lax.dot_general(a_t, b_t, nt, preferred_element_type=F32)
            stage_ref[pl.ds(k, nk, stride=grp), :] = g_t
        for r in range(nk):
            gate_ref[slot, pl.ds(pl.multiple_of(r * tm + t0, grp), grp), :] = (
                stage_ref[r * grp:(r + 1) * grp, :].astype(BF16))

    @pl.when(j == 0)
    def _():
        out_ref[...] = jnp.zeros_like(out_ref)

    @pl.when((i == 0) & (j == 0))
    def _():
        def first(tq, carry):
            build_group(i1a_ref, i2a_ref, ga_ref, pl.multiple_of(tq * grp, grp), 0)
            return carry

        lax.fori_loop(0, tm // grp, first, 0)

    build_group(i1b_ref, i2b_ref, gb_ref, pl.multiple_of((j * grp) % tm, grp), 1 - par)

    act = jnp.dot(xn_ref[...], ut_ref[...], preferred_element_type=F32)
    nb = eb // nk
    z = [(_gelu_exact(act[:, b * nk:(b + 1) * nk])
          * gate_ref[par, pl.ds(pl.multiple_of((j * nb + b) * tm, tm), tm), :].astype(F32)
          ).astype(BF16)
         for b in range(nb)]
    out_ref[...] += jnp.dot(jnp.concatenate(z, axis=1), v_ref[...], preferred_element_type=F32)


def peer_dense(xn, expert_t, gate_t, u_tab, v_tab, tm=512, eb=512):
    t, d = xn.shape
    tm = min(tm, t)
    nti = t // tm
    nsteps = N_EXPERTS // eb
    assert tm % PEER_BUILD_GROUP == 0 and tm <= nsteps * PEER_BUILD_GROUP
    ns = expert_t.shape[0]
    i1 = (expert_t // PEER_NKEYS).T
    i2 = (expert_t % PEER_NKEYS).T
    g = gate_t.T
    slot_a = pl.BlockSpec((tm, ns), lambda i, j: (0, 0))
    slot_b = pl.BlockSpec((tm, ns), lambda i, j: (jnp.minimum(i + 1, nti - 1), 0))
    row = pl.BlockSpec((tm, d), lambda i, j: (i, 0), pipeline_mode=pl.Buffered(1))
    return pl.pallas_call(
        functools.partial(_peer_dense_kernel, tm=tm, eb=eb),
        grid=(nti, nsteps),
        in_specs=[slot_a, slot_a, slot_a, slot_b, slot_b, slot_b, row,
                  pl.BlockSpec((d, eb), lambda i, j: (0, j)),
                  pl.BlockSpec((eb, d), lambda i, j: (j, 0))],
        out_specs=row,
        out_shape=jax.ShapeDtypeStruct((t, d), F32),
        scratch_shapes=[pltpu.VMEM((2, PEER_NKEYS * tm, PEER_NKEYS), BF16),
                        pltpu.VMEM((PEER_NKEYS * PEER_BUILD_GROUP, PEER_NKEYS), F32)],
        compiler_params=_cparams(2),
        name="peer_dense",
    )(i1, i2, g, i1, i2, g, xn.astype(BF16), u_tab.astype(BF16).T, v_tab.astype(BF16))


def peer_layer(xn, wq, k1, k2, u_tab, v_tab):
    q = matmul(xn, wq.astype(BF16), tn=1024, name="peer_query")
    expert_t, gate_t = peer_topk(q, k1, k2)
    return peer_dense(xn, expert_t, gate_t, u_tab, v_tab)


def _trunk(x, parts, even_mix_norm, even_w_in, even_gate_bias, even_pool_w, even_pool_scale,
           even_w_out, odd_mix_norm, odd_fourier_w, ffn_norm, peer_wq, peer_k1, peer_k2,
           peer_u, peer_v, final_norm):
    cat = lambda xs: xs[0] if len(xs) == 1 else jnp.concatenate(xs, axis=0)

    xn = rmsnorm(x, even_mix_norm[0])
    n_main = 4 * D_A + D_B
    w_in = even_w_in[0]
    z = matmul(xn, w_in[:, :n_main].astype(BF16), tn=1024, name="in_proj")
    w_gate = jnp.pad(w_in[:, n_main:], ((0, 0), (0, LANES - N_GATES)))
    gates = matmul_hi(xn, w_gate)
    gates_t = gates[:, :N_GATES].T
    bias = even_gate_bias[0].astype(F32)
    bias_row = jnp.pad(bias, (0, LANES - N_GATES)).reshape(1, LANES)
    bias_col = bias.reshape(N_GATES, 1)
    pw = even_pool_w[0].astype(BF16)
    hs, ybs = [], []
    for base, batch, seq in parts:
        hs.append(mlstm(z, gates, gates_t, bias_row, bias_col, base, batch, seq))
        ybs.append(pool(z, pw, even_pool_scale[0], 4 * D_A, base, batch, seq))
    h2 = hs[0] if len(hs) == 1 else jnp.concatenate(hs, axis=1)
    x, xn = outproj(h2, z, cat(ybs), even_w_out[0].astype(BF16), x, ffn_norm[0])
    y = peer_layer(xn, peer_wq[0], peer_k1[0], peer_k2[0], peer_u[0], peer_v[0])

    x, xn = add_rmsnorm(x, y, odd_mix_norm[0])
    cd, sd = dft_tables(FOURIER_GROUP_DIM)
    yc, ys = fourier_channels(xn, cd, sd)
    fs = []
    for base, batch, seq in parts:
        cs, ss = dft_tables(seq)
        fs.append(fourier_seq(cs, ss, yc, ys, base, batch, seq,
                              tm=min(1024, seq), tk=min(512, seq)))
    x, xn = matmul_res_norm(cat(fs), odd_fourier_w[0].astype(BF16), x, ffn_norm[1],
                            name="fourier_out")
    y = peer_layer(xn, peer_wq[1], peer_k1[1], peer_k2[1], peer_u[1], peer_v[1])
    return [add_rmsnorm_part(x, y, final_norm, base, batch * seq) for base, batch, seq in parts]


def kernel(x_prompt, x_sample, even_mix_norm, even_w_in, even_gate_bias, even_pool_w,
           even_pool_scale, even_w_out, odd_mix_norm, odd_fourier_w, ffn_norm, peer_wq,
           peer_k1, peer_k2, peer_u, peer_v, final_norm):
    bp, sp, d = x_prompt.shape
    bs, ss, _ = x_sample.shape
    tp = bp * sp
    x = jnp.concatenate([x_prompt.reshape(tp, d), x_sample.reshape(bs * ss, d)], axis=0)
    parts = ((0, bp, sp), (tp, bs, ss))
    yp, ys = _trunk(x, parts, even_mix_norm, even_w_in, even_gate_bias, even_pool_w,
                    even_pool_scale, even_w_out, odd_mix_norm, odd_fourier_w, ffn_norm,
                    peer_wq, peer_k1, peer_k2, peer_u, peer_v, final_norm)
    return yp.reshape(bp, sp, d), ys.reshape(bs, ss, d)
```

```python
import functools
import math

import jax
import jax.numpy as jnp
from jax import lax
from jax.experimental import pallas as pl
from jax.experimental.pallas import tpu as pltpu

F32 = jnp.float32
BF16 = jnp.bfloat16
I32 = jnp.int32
HI = lax.Precision.HIGHEST

D_MODEL = 2048
EPS = 1e-6
N_HEADS = 4
HEAD_DIM = 256
D_A = N_HEADS * HEAD_DIM
D_B = D_MODEL - D_A
N_GATES = 4 * N_HEADS
CHUNK = 128
MLSTM_HEADS_PER_STEP = 2
N_POOL_GROUPS = 4
POOL_TILE = 256
POOL_GROUP_DIM = D_B // N_POOL_GROUPS
N_FOURIER_GROUPS = 4
FOURIER_GROUP_DIM = D_MODEL // N_FOURIER_GROUPS
PEER_HEADS = 8
PEER_NKEYS = 128
PEER_HALF = 128
PEER_TOPK = 16
N_EXPERTS = PEER_NKEYS * PEER_NKEYS
KEY_SENTINEL = 1e9

LANES = 128
PEER_BUILD_GROUP = 16
VMEM_LIMIT = 56 * 1024 * 1024


def _cparams(n_axes):
    return pltpu.CompilerParams(dimension_semantics=("arbitrary",) * n_axes,
                                vmem_limit_bytes=VMEM_LIMIT)


def _rms(x, g):
    ms = jnp.mean(x * x, axis=-1, keepdims=True)
    return (x * lax.rsqrt(ms + EPS)) * g


def _norm_kernel(a_ref, g_ref, xn_ref):
    xn_ref[...] = _rms(a_ref[...], g_ref[...])


def _addnorm_kernel(a_ref, b_ref, g_ref, x_ref, xn_ref):
    x = a_ref[...] + b_ref[...]
    x_ref[...] = x
    xn_ref[...] = _rms(x, g_ref[...])


def _concat_norm_kernel(a_ref, b_ref, g_ref, x_ref, xn_ref, *, na):
    i = pl.program_id(0)

    @pl.when(i < na)
    def _():
        x_ref[...] = a_ref[...]

    @pl.when(i >= na)
    def _():
        x_ref[...] = b_ref[...]

    xn_ref[...] = _rms(x_ref[...], g_ref[...])


def concat_rmsnorm(a, b, g, tm=256):
    d = a.shape[1]
    na, nb = a.shape[0] // tm, b.shape[0] // tm
    row = pl.BlockSpec((tm, d), lambda i: (i, 0))
    return pl.pallas_call(
        functools.partial(_concat_norm_kernel, na=na),
        grid=(na + nb,),
        in_specs=[pl.BlockSpec((tm, d), lambda i: (jnp.minimum(i, na - 1), 0)),
                  pl.BlockSpec((tm, d), lambda i: (jnp.maximum(i - na, 0), 0)),
                  pl.BlockSpec((1, d), lambda i: (0, 0))],
        out_specs=[row, row],
        out_shape=[jax.ShapeDtypeStruct((a.shape[0] + b.shape[0], d), F32)] * 2,
        compiler_params=_cparams(1),
        name="concat_rmsnorm",
    )(a, b, g.reshape(1, d))


def rmsnorm(a, g, tm=256):
    t, d = a.shape
    tm = min(tm, t)
    row = pl.BlockSpec((tm, d), lambda i: (i, 0))
    return pl.pallas_call(
        _norm_kernel,
        grid=(t // tm,),
        in_specs=[row, pl.BlockSpec((1, d), lambda i: (0, 0))],
        out_specs=row,
        out_shape=jax.ShapeDtypeStruct((t, d), F32),
        compiler_params=_cparams(1),
        name="rmsnorm",
    )(a, g.reshape(1, d))


def add_rmsnorm(a, b, g, tm=256):
    t, d = a.shape
    tm = min(tm, t)
    row = pl.BlockSpec((tm, d), lambda i: (i, 0))
    return pl.pallas_call(
        _addnorm_kernel,
        grid=(t // tm,),
        in_specs=[row, row, pl.BlockSpec((1, d), lambda i: (0, 0))],
        out_specs=[row, row],
        out_shape=[jax.ShapeDtypeStruct((t, d), F32)] * 2,
        compiler_params=_cparams(1),
        name="add_rmsnorm",
    )(a, b, g.reshape(1, d))


def _addnorm_out_kernel(a_ref, b_ref, g_ref, xn_ref):
    xn_ref[...] = _rms(a_ref[...] + b_ref[...], g_ref[...])


def add_rmsnorm_part(a, b, g, base_tok, rows, tm=256):
    d = a.shape[1]
    tm = min(tm, rows)
    src = pl.BlockSpec((tm, d), lambda i: (base_tok // tm + i, 0))
    return pl.pallas_call(
        _addnorm_out_kernel,
        grid=(rows // tm,),
        in_specs=[src, src, pl.BlockSpec((1, d), lambda i: (0, 0))],
        out_specs=pl.BlockSpec((tm, d), lambda i: (i, 0)),
        out_shape=jax.ShapeDtypeStruct((rows, d), F32),
        compiler_params=_cparams(1),
        name="final_norm",
    )(a, b, g.reshape(1, d))


def _mm_kernel(a_ref, w_ref, o_ref):
    o_ref[...] = jnp.dot(a_ref[...].astype(BF16), w_ref[...],
                         preferred_element_type=F32).astype(o_ref.dtype)


def _mm_res_kernel(a_ref, w_ref, r_ref, o_ref):
    acc = jnp.dot(a_ref[...].astype(BF16), w_ref[...], preferred_element_type=F32)
    o_ref[...] = (acc + r_ref[...]).astype(o_ref.dtype)


def _mm_hi_kernel(a_ref, w_ref, o_ref):
    o_ref[...] = jnp.dot(a_ref[...], w_ref[...], precision=HI,
                         preferred_element_type=F32)


def _mm_res_norm_kernel(a_ref, w_ref, r_ref, g_ref, o_ref, xn_ref):
    x = jnp.dot(a_ref[...].astype(BF16), w_ref[...], preferred_element_type=F32) + r_ref[...]
    o_ref[...] = x
    xn_ref[...] = _rms(x, g_ref[...])


def matmul_res_norm(a, w, res, g, tm=256, name="matmul_res_norm"):
    m, k = a.shape
    n = w.shape[1]
    tm = min(tm, m)
    row = pl.BlockSpec((tm, n), lambda i: (i, 0))
    return pl.pallas_call(
        _mm_res_norm_kernel,
        grid=(m // tm,),
        in_specs=[pl.BlockSpec((tm, k), lambda i: (i, 0)),
                  pl.BlockSpec((k, n), lambda i: (0, 0)),
                  row,
                  pl.BlockSpec((1, n), lambda i: (0, 0))],
        out_specs=[row, row],
        out_shape=[jax.ShapeDtypeStruct((m, n), F32)] * 2,
        compiler_params=_cparams(1),
        name=name,
    )(a, w, res, g.reshape(1, n))


def matmul(a, w, res=None, tm=512, tn=512, out_dtype=F32, name="matmul"):
    m, k = a.shape
    n = w.shape[1]
    tm = min(tm, m)
    tn = min(tn, n)
    in_specs = [pl.BlockSpec((tm, k), lambda i, j: (i, 0)),
                pl.BlockSpec((k, tn), lambda i, j: (0, j))]
    args = [a, w]
    body = _mm_kernel
    if res is not None:
        in_specs.append(pl.BlockSpec((tm, tn), lambda i, j: (i, j)))
        args.append(res)
        body = _mm_res_kernel
    return pl.pallas_call(
        body,
        grid=(m // tm, n // tn),
        in_specs=in_specs,
        out_specs=pl.BlockSpec((tm, tn), lambda i, j: (i, j)),
        out_shape=jax.ShapeDtypeStruct((m, n), out_dtype),
        compiler_params=_cparams(2),
        name=name,
    )(*args)


def matmul_hi(a, w, tm=512):
    m, k = a.shape
    n = w.shape[1]
    tm = min(tm, m)
    return pl.pallas_call(
        _mm_hi_kernel,
        grid=(m // tm,),
        in_specs=[pl.BlockSpec((tm, k), lambda i: (i, 0)),
                  pl.BlockSpec((k, n), lambda i: (0, 0))],
        out_specs=pl.BlockSpec((tm, n), lambda i: (i, 0)),
        out_shape=jax.ShapeDtypeStruct((m, n), F32),
        compiler_params=_cparams(1),
        name="matmul_hi",
    )(a, w)


def _log_sigmoid(x):
    return jnp.minimum(x, 0.0) - jnp.log(1.0 + jnp.exp(-jnp.abs(x)))


def _mlstm_kernel(q_ref, k_ref, v_ref, gc_ref, gr_ref, bc_ref, br_ref, h_ref,
                  c_ref, n_ref, m_ref):
    d = pl.program_id(0)
    hp = pl.program_id(2)
    c = pl.program_id(3)
    L = CHUNK

    @pl.when(c == 0)
    def _():
        c_ref[...] = jnp.zeros_like(c_ref)
        n_ref[...] = jnp.zeros_like(n_ref)
        m_ref[...] = jnp.zeros_like(m_ref)

    gcol = gc_ref[...] + bc_ref[...]
    grow = gr_ref[...] + br_ref[...]
    lane = lax.broadcasted_iota(I32, gcol.shape, 1)
    sub = lax.broadcasted_iota(I32, grow.shape, 0)
    r = lax.broadcasted_iota(I32, (L, L), 0)
    cc = lax.broadcasted_iota(I32, (L, L), 1)
    sign = 1 - 2 * d
    mask = (cc - r) * sign <= 0
    mask_t = (r - cc) * sign <= 0

    for hh in range(MLSTM_HEADS_PER_STEP):
        cols = slice(hh * HEAD_DIM, (hh + 1) * HEAD_DIM)
        q = q_ref[:, cols] * (HEAD_DIM ** -0.5)
        k = k_ref[:, cols]
        v = v_ref[:, cols]
        li_idx = 2 * d * N_HEADS + hp * MLSTM_HEADS_PER_STEP + hh
        lf_idx = li_idx + N_HEADS
        li_col = jnp.sum(jnp.where(lane == li_idx, gcol, 0.0), axis=1, keepdims=True)
        gf_col = jnp.sum(jnp.where(lane == lf_idx, gcol, 0.0), axis=1, keepdims=True)
        li_row = jnp.sum(jnp.where(sub == li_idx, grow, 0.0), axis=0, keepdims=True)
        gf_row = jnp.sum(jnp.where(sub == lf_idx, grow, 0.0), axis=0, keepdims=True)
        lf_col = _log_sigmoid(gf_col)
        lf_row = _log_sigmoid(gf_row)
        b_col = jnp.sum(jnp.where(mask, lf_row, 0.0), axis=1, keepdims=True)
        b_row = jnp.sum(jnp.where(mask_t, lf_col, 0.0), axis=0, keepdims=True)
        logw = jnp.where(mask, b_col - b_row + li_row, -jnp.inf)

        m_prev = m_ref[hh, :, 0:1]
        inter = b_col + m_prev
        m_t = jnp.maximum(inter, jnp.max(logw, axis=1, keepdims=True))
        w = jnp.exp(logw - m_t)
        a_inter = jnp.exp(inter - m_t)
        qb = q.astype(BF16)
        vb = v.astype(BF16)
        s = lax.dot_general(qb, k.astype(BF16), (((1,), (1,)), ((), ())),
                            preferred_element_type=F32) * w
        cm = c_ref[hh]
        nv = n_ref[hh]
        num = (jnp.dot(s.astype(BF16), vb, preferred_element_type=F32)
               + a_inter * jnp.dot(qb, cm.astype(BF16), preferred_element_type=F32))
        den = (jnp.sum(s, axis=1, keepdims=True)
               + a_inter * jnp.sum(q * nv, axis=1, keepdims=True))
        h_ref[:, cols] = num / jnp.maximum(jnp.abs(den), jnp.exp(-m_t))

        b_last = jnp.sum(lf_row, axis=1, keepdims=True)
        logu = b_last - b_col + li_col
        m_new = jnp.maximum(b_last + m_prev, jnp.max(logu, axis=0, keepdims=True))
        u = jnp.exp(logu - m_new)
        decay = jnp.exp(b_last + m_prev - m_new)
        uk = u * k
        c_ref[hh] = decay * cm + lax.dot_general(
            uk.astype(BF16), vb, (((0,), (0,)), ((), ())), preferred_element_type=F32)
        n_ref[hh] = decay * nv + jnp.sum(uk, axis=0, keepdims=True)
        m_ref[hh] = jnp.broadcast_to(m_new, (1, LANES))


def mlstm(z, gates, gates_t, bias_row, bias_col, base_tok, batch, seq):
    nc = seq // CHUNK
    base = base_tok // CHUNK
    hps = MLSTM_HEADS_PER_STEP
    ngrp = N_HEADS // hps

    def blk(dd, b, c):
        return base + b * nc + c + dd * (nc - 1 - 2 * c)

    def sec(which):
        return pl.BlockSpec((CHUNK, hps * HEAD_DIM),
                            lambda dd, b, h, c: (blk(dd, b, c), which * ngrp + h))

    return pl.pallas_call(
        _mlstm_kernel,
        grid=(2, batch, ngrp, nc),
        in_specs=[sec(0), sec(1), sec(2),
                  pl.BlockSpec((CHUNK, LANES), lambda dd, b, h, c: (blk(dd, b, c), 0)),
                  pl.BlockSpec((N_GATES, CHUNK), lambda dd, b, h, c: (0, blk(dd, b, c))),
                  pl.BlockSpec((1, LANES), lambda dd, b, h, c: (0, 0)),
                  pl.BlockSpec((N_GATES, 1), lambda dd, b, h, c: (0, 0))],
        out_specs=pl.BlockSpec((None, CHUNK, hps * HEAD_DIM),
                               lambda dd, b, h, c: (dd, blk(dd, b, c) - base, h)),
        out_shape=jax.ShapeDtypeStruct((2, batch * seq, D_A), F32),
        scratch_shapes=[pltpu.VMEM((hps, HEAD_DIM, HEAD_DIM), F32),
                        pltpu.VMEM((hps, 1, HEAD_DIM), F32),
                        pltpu.VMEM((hps, 1, LANES), F32)],
        compiler_params=_cparams(4),
        name="mlstm",
    )(z, z, z, gates, gates_t, bias_row, bias_col)


def _pool_kernel(prev_ref, mid_ref, next_ref, pw_ref, sc_ref, o_ref, *, seq):
    i = pl.program_id(1)
    g = pl.program_id(2)
    half = jnp.left_shift(1, g)
    T = POOL_TILE
    p = i * T + lax.broadcasted_iota(I32, (T, T), 0)
    lo = jnp.maximum(p - half, 0)
    hi = jnp.minimum(p + half, seq)
    acc = jnp.zeros((T, POOL_GROUP_DIM), F32)
    for off, ref in ((-T, prev_ref), (0, mid_ref), (T, next_ref)):
        pj = i * T + off + lax.broadcasted_iota(I32, (T, T), 1)
        band = jnp.where((pj >= lo) & (pj < hi), 1.0, 0.0).astype(BF16)
        u = ref[...]
        u_hi = u.astype(BF16)
        u_lo = (u - u_hi.astype(F32)).astype(BF16)
        acc = acc + (jnp.dot(band, u_hi, preferred_element_type=F32)
                     + jnp.dot(band, u_lo, preferred_element_type=F32))
    cnt = (hi - lo)[:, 0:1].astype(F32)
    pooled = acc / cnt - mid_ref[...]
    y = jnp.dot(pooled.astype(BF16), pw_ref[...], preferred_element_type=F32)
    o_ref[...] = y * sc_ref[...]


def pool(z, pool_w, pool_scale, col0, base_tok, batch, seq):
    nt = seq // POOL_TILE
    base = base_tok // POOL_TILE
    cb = col0 // POOL_GROUP_DIM

    def spec(shift):
        return pl.BlockSpec(
            (POOL_TILE, POOL_GROUP_DIM),
            lambda b, i, g: (base + b * nt + jnp.clip(i + shift, 0, nt - 1), cb + g))

    return pl.pallas_call(
        functools.partial(_pool_kernel, seq=seq),
        grid=(batch, nt, N_POOL_GROUPS),
        in_specs=[spec(-1), spec(0), spec(1),
                  pl.BlockSpec((None, POOL_GROUP_DIM, POOL_GROUP_DIM), lambda b, i, g: (g, 0, 0)),
                  pl.BlockSpec((1, POOL_GROUP_DIM), lambda b, i, g: (0, g))],
        out_specs=pl.BlockSpec((POOL_TILE, POOL_GROUP_DIM), lambda b, i, g: (b * nt + i, g)),
        out_shape=jax.ShapeDtypeStruct((batch * seq, D_B), F32),
        compiler_params=_cparams(3),
        name="pool",
    )(z, z, z, pool_w, pool_scale.reshape(1, D_B))


def _outproj_kernel(hf_ref, hb_ref, o_ref, yb_ref, wa_ref, wb_ref, x_ref, g_ref, out_ref, xn_ref):
    ha = (hf_ref[...] + hb_ref[...]) * jax.nn.sigmoid(o_ref[...])
    acc = jnp.dot(ha.astype(BF16), wa_ref[...], preferred_element_type=F32)
    acc = acc + jnp.dot(yb_ref[...].astype(BF16), wb_ref[...], preferred_element_type=F32)
    x = acc + x_ref[...]
    out_ref[...] = x
    xn_ref[...] = _rms(x, g_ref[...])


def outproj(h2, z, yb, w_out, x, g, tm=256):
    t, d = x.shape
    tm = min(tm, t)
    ocb = 3 * D_A // D_A
    row = pl.BlockSpec((tm, d), lambda i: (i, 0))
    return pl.pallas_call(
        _outproj_kernel,
        grid=(t // tm,),
        in_specs=[pl.BlockSpec((None, tm, D_A), lambda i: (0, i, 0)),
                  pl.BlockSpec((None, tm, D_A), lambda i: (1, i, 0)),
                  pl.BlockSpec((tm, D_A), lambda i: (i, ocb)),
                  pl.BlockSpec((tm, D_B), lambda i: (i, 0)),
                  pl.BlockSpec((D_A, d), lambda i: (0, 0)),
                  pl.BlockSpec((D_B, d), lambda i: (D_A // D_B, 0)),
                  row,
                  pl.BlockSpec((1, d), lambda i: (0, 0))],
        out_specs=[row, row],
        out_shape=[jax.ShapeDtypeStruct((t, d), F32)] * 2,
        compiler_params=_cparams(1),
        name="outproj",
    )(h2, h2, z, yb, w_out, w_out, x, g.reshape(1, d))


def _dft_kernel(c_ref, s_ref, *, n):
    tr = c_ref.shape[0]
    r = pl.program_id(0) * tr + lax.broadcasted_iota(I32, (tr, n), 0)
    c = lax.broadcasted_iota(I32, (tr, n), 1)
    ang = ((r * c) & (n - 1)).astype(F32) * (2.0 * math.pi / n)
    scale = n ** -0.5
    c_ref[...] = (jnp.cos(ang) * scale).astype(BF16)
    s_ref[...] = (jnp.sin(ang) * scale).astype(BF16)


def dft_tables(n, tr=256):
    assert n & (n - 1) == 0
    blk = pl.BlockSpec((tr, n), lambda i: (i, 0))
    return pl.pallas_call(
        functools.partial(_dft_kernel, n=n),
        grid=(n // tr,),
        out_specs=[blk, blk],
        out_shape=[jax.ShapeDtypeStruct((n, n), BF16)] * 2,
        compiler_params=_cparams(1),
        name="dft_tables",
    )()


def _fourier_ch_kernel(x_ref, cd_ref, sd_ref, yc_ref, ys_ref):
    xb = x_ref[...].astype(BF16)
    yc_ref[...] = jnp.dot(xb, cd_ref[...], preferred_element_type=F32).astype(BF16)
    ys_ref[...] = jnp.dot(xb, sd_ref[...], preferred_element_type=F32).astype(BF16)


def fourier_channels(xn, cd, sd, tm=512):
    t = xn.shape[0]
    tm = min(tm, t)
    gd = FOURIER_GROUP_DIM
    tile = pl.BlockSpec((tm, gd), lambda i, g: (i, g))
    tab = pl.BlockSpec((gd, gd), lambda i, g: (0, 0))
    return pl.pallas_call(
        _fourier_ch_kernel,
        grid=(t // tm, N_FOURIER_GROUPS),
        in_specs=[tile, tab, tab],
        out_specs=[tile, tile],
        out_shape=[jax.ShapeDtypeStruct((t, D_MODEL), BF16)] * 2,
        compiler_params=_cparams(2),
        name="fourier_channels",
    )(xn, cd, sd)


def _fourier_seq_kernel(cs_ref, ss_ref, yc_ref, ys_ref, o_ref, acc_ref):
    kk = pl.program_id(3)

    @pl.when(kk == 0)
    def _():
        acc_ref[...] = jnp.zeros_like(acc_ref)

    acc_ref[...] += (jnp.dot(cs_ref[...], yc_ref[...], preferred_element_type=F32)
                     - jnp.dot(ss_ref[...], ys_ref[...], preferred_element_type=F32))

    @pl.when(kk == pl.num_programs(3) - 1)
    def _():
        o_ref[...] = acc_ref[...].astype(o_ref.dtype)


def fourier_seq(cs, ss, yc, ys, base_tok, batch, seq, tm=1024, tn=1024, tk=512):
    nk = seq // tk
    base = base_tok // tk
    a_spec = pl.BlockSpec((tm, tk), lambda b, i, j, k: (i, k))
    y_spec = pl.BlockSpec((tk, tn), lambda b, i, j, k: (base + b * nk + k, j))
    return pl.pallas_call(
        _fourier_seq_kernel,
        grid=(batch, seq // tm, D_MODEL // tn, nk),
        in_specs=[a_spec, a_spec, y_spec, y_spec],
        out_specs=pl.BlockSpec((tm, tn), lambda b, i, j, k: (b * (seq // tm) + i, j)),
        out_shape=jax.ShapeDtypeStruct((batch * seq, D_MODEL), BF16),
        scratch_shapes=[pltpu.VMEM((tm, tn), F32)],
        compiler_params=_cparams(4),
        name="fourier_seq",
    )(cs, ss, yc, ys)


def _top16_rows(s, row_key):
    n = s.shape[1]
    out_iota = lax.broadcasted_iota(I32, (PEER_TOPK, n), 0)
    vals = jnp.zeros((PEER_TOPK, n), F32)
    keys = jnp.zeros((PEER_TOPK, n), F32)
    for rnk in range(PEER_TOPK):
        mx = jnp.max(s, axis=0, keepdims=True)
        kmin = jnp.min(jnp.where(s == mx, row_key, KEY_SENTINEL), axis=0, keepdims=True)
        vals = jnp.where(out_iota == rnk, mx, vals)
        keys = jnp.where(out_iota == rnk, kmin, keys)
        s = jnp.where(row_key == kmin, -jnp.inf, s)
    return vals, keys


def _peer_topk_kernel(q_ref, k1_ref, k2_ref, e_ref, g_ref):
    tt = q_ref.shape[0]
    k1 = k1_ref[...]
    k2 = k2_ref[...]
    fiota = lambda shape: lax.broadcasted_iota(I32, shape, 0).astype(F32)
    key_iota = fiota((PEER_NKEYS, tt))
    nt = (((1,), (1,)), ((), ()))
    blocks = [(a, 16 if a == 0 else 8) for a in range(8)]
    for h in range(PEER_HEADS):
        q1 = q_ref[:, h * 2 * PEER_HALF: h * 2 * PEER_HALF + PEER_HALF]
        q2 = q_ref[:, h * 2 * PEER_HALF + PEER_HALF: (h + 1) * 2 * PEER_HALF]
        s1 = lax.dot_general(k1, q1, nt, precision=HI, preferred_element_type=F32)
        s2 = lax.dot_general(k2, q2, nt, precision=HI, preferred_element_type=F32)
        v1, i1 = _top16_rows(s1, key_iota)
        v2, i2 = _top16_rows(s2, key_iota)
        cands, ids, orders = [], [], []
        for a, rows in blocks:
            bi = fiota((rows, tt))
            ok = (bi + 1.0) * (a + 1) <= PEER_TOPK
            cands.append(jnp.where(ok, v1[a:a + 1, :] + v2[0:rows, :], -jnp.inf))
            ids.append(i1[a:a + 1, :] * PEER_NKEYS + i2[0:rows, :])
            orders.append(a * PEER_TOPK + bi)
        ai = fiota((8, tt)) + 8.0
        cands.append(v1[8:16, :] + v2[0:1, :])
        ids.append(i1[8:16, :] * PEER_NKEYS + i2[0:1, :])
        orders.append(ai * PEER_TOPK)
        cand = jnp.concatenate(cands, axis=0)
        cid = jnp.concatenate(ids, axis=0)
        order = jnp.concatenate(orders, axis=0)
        ts, tpos = _top16_rows(cand, order)
        experts = jnp.zeros((PEER_TOPK, tt), F32)
        out_iota = lax.broadcasted_iota(I32, (PEER_TOPK, tt), 0)
        for rnk in range(PEER_TOPK):
            sel = jnp.max(jnp.where(order == tpos[rnk:rnk + 1, :], cid, -1.0), axis=0, keepdims=True)
            experts = jnp.where(out_iota == rnk, sel, experts)
        ex = jnp.exp(ts - ts[0:1, :])
        gate = ex / jnp.sum(ex, axis=0, keepdims=True)
        e_ref[h * PEER_TOPK:(h + 1) * PEER_TOPK, :] = experts.astype(I32)
        g_ref[h * PEER_TOPK:(h + 1) * PEER_TOPK, :] = gate


def peer_topk(q, k1, k2, tt=256):
    t = q.shape[0]
    tt = min(tt, t)
    ne = PEER_HEADS * PEER_TOPK
    keys = pl.BlockSpec((PEER_NKEYS, PEER_HALF), lambda i: (0, 0))
    out = pl.BlockSpec((ne, tt), lambda i: (0, i))
    return pl.pallas_call(
        _peer_topk_kernel,
        grid=(t // tt,),
        in_specs=[pl.BlockSpec((tt, q.shape[1]), lambda i: (i, 0)), keys, keys],
        out_specs=[out, out],
        out_shape=[jax.ShapeDtypeStruct((ne, t), I32), jax.ShapeDtypeStruct((ne, t), F32)],
        compiler_params=_cparams(1),
        name="peer_topk",
    )(q, k1, k2)


def _gelu_exact(x):
    return 0.5 * x * (1.0 + lax.erf(x * (2.0 ** -0.5)))


def _peer_dense_kernel(i1a_ref, i2a_ref, ga_ref, i1b_ref, i2b_ref, gb_ref, xn_ref, ut_ref, v_ref,
                       out_ref, gate_ref, stage_ref, *, tm, eb):
    i = pl.program_id(0)
    j = pl.program_id(1)
    nk = PEER_NKEYS
    grp = PEER_BUILD_GROUP
    par = i % 2
    key = lax.broadcasted_iota(I32, (nk, nk), 0)
    nt = (((1,), (1,)), ((), ()))

    def build_group(i1_ref, i2_ref, g_ref, t0, slot):
        for k in range(grp):
            t = t0 + k
            a_t = jnp.where(key == i1_ref[pl.ds(t, 1), :], 1.0, 0.0).astype(BF16)
            b_t = jnp.where(key == i2_ref[pl.ds(t, 1), :], g_ref[pl.ds(t, 1), :],
                            0.0).astype(BF16)
            g_t = lax.dot_general(a_t, b_t, nt, preferred_element_type=F32)
            stage_ref[pl.ds(k, nk, stride=grp), :] = g_t
        for r in range(nk):
            gate_ref[slot, pl.ds(pl.multiple_of(r * tm + t0, grp), grp), :] = (
                stage_ref[r * grp:(r + 1) * grp, :].astype(BF16))

    @pl.when(j == 0)
    def _():
        out_ref[...] = jnp.zeros_like(out_ref)

    @pl.when((i == 0) & (j == 0))
    def _():
        def first(tq, carry):
            build_group(i1a_ref, i2a_ref, ga_ref, pl.multiple_of(tq * grp, grp), 0)
            return carry

        lax.fori_loop(0, tm // grp, first, 0)

    build_group(i1b_ref, i2b_ref, gb_ref, pl.multiple_of((j * grp) % tm, grp), 1 - par)

    act = jnp.dot(xn_ref[...], ut_ref[...], preferred_element_type=F32)
    nb = eb // nk
    z = [(_gelu_exact(act[:, b * nk:(b + 1) * nk])
          * gate_ref[par, pl.ds(pl.multiple_of((j * nb + b) * tm, tm), tm), :].astype(F32)
          ).astype(BF16)
         for b in range(nb)]
    out_ref[...] += jnp.dot(jnp.concatenate(z, axis=1), v_ref[...], preferred_element_type=F32)


def peer_dense(xn, expert_t, gate_t, u_tab, v_tab, tm=512, eb=512):
    t, d = xn.shape
    tm = min(tm, t)
    nti = t // tm
    nsteps = N_EXPERTS // eb
    assert tm % PEER_BUILD_GROUP == 0 and tm <= nsteps * PEER_BUILD_GROUP
    ns = expert_t.shape[0]
    i1 = (expert_t // PEER_NKEYS).T
    i2 = (expert_t % PEER_NKEYS).T
    g = gate_t.T
    slot_a = pl.BlockSpec((tm, ns), lambda i, j: (0, 0))
    slot_b = pl.BlockSpec((tm, ns), lambda i, j: (jnp.minimum(i + 1, nti - 1), 0))
    row = pl.BlockSpec((tm, d), lambda i, j: (i, 0), pipeline_mode=pl.Buffered(1))
    return pl.pallas_call(
        functools.partial(_peer_dense_kernel, tm=tm, eb=eb),
        grid=(nti, nsteps),
        in_specs=[slot_a, slot_a, slot_a, slot_b, slot_b, slot_b, row,
                  pl.BlockSpec((d, eb), lambda i, j: (0, j)),
                  pl.BlockSpec((eb, d), lambda i, j: (j, 0))],
        out_specs=row,
        out_shape=jax.ShapeDtypeStruct((t, d), F32),
        scratch_shapes=[pltpu.VMEM((2, PEER_NKEYS * tm, PEER_NKEYS), BF16),
                        pltpu.VMEM((PEER_NKEYS * PEER_BUILD_GROUP, PEER_NKEYS), F32)],
        compiler_params=_cparams(2),
        name="peer_dense",
    )(i1, i2, g, i1, i2, g, xn.astype(BF16), u_tab.astype(BF16).T, v_tab.astype(BF16))


def peer_layer(xn, wq, k1, k2, u_tab, v_tab):
    q = matmul(xn, wq.astype(BF16), tn=1024, name="peer_query")
    expert_t, gate_t = peer_topk(q, k1, k2)
    return peer_dense(xn, expert_t, gate_t, u_tab, v_tab)


def _trunk(x, xn, parts, even_mix_norm, even_w_in, even_gate_bias, even_pool_w, even_pool_scale,
           even_w_out, odd_mix_norm, odd_fourier_w, ffn_norm, peer_wq, peer_k1, peer_k2,
           peer_u, peer_v, final_norm):
    cat = lambda xs: xs[0] if len(xs) == 1 else jnp.concatenate(xs, axis=0)

    n_main = 4 * D_A + D_B
    w_in = even_w_in[0]
    z = matmul(xn, w_in[:, :n_main].astype(BF16), tn=1024, name="in_proj")
    w_gate = jnp.pad(w_in[:, n_main:], ((0, 0), (0, LANES - N_GATES)))
    gates = matmul_hi(xn, w_gate)
    gates_t = gates[:, :N_GATES].T
    bias = even_gate_bias[0].astype(F32)
    bias_row = jnp.pad(bias, (0, LANES - N_GATES)).reshape(1, LANES)
    bias_col = bias.reshape(N_GATES, 1)
    pw = even_pool_w[0].astype(BF16)
    hs, ybs = [], []
    for base, batch, seq in parts:
        hs.append(mlstm(z, gates, gates_t, bias_row, bias_col, base, batch, seq))
        ybs.append(pool(z, pw, even_pool_scale[0], 4 * D_A, base, batch, seq))
    h2 = hs[0] if len(hs) == 1 else jnp.concatenate(hs, axis=1)
    x, xn = outproj(h2, z, cat(ybs), even_w_out[0].astype(BF16), x, ffn_norm[0])
    y = peer_layer(xn, peer_wq[0], peer_k1[0], peer_k2[0], peer_u[0], peer_v[0])

    x, xn = add_rmsnorm(x, y, odd_mix_norm[0])
    cd, sd = dft_tables(FOURIER_GROUP_DIM)
    yc, ys = fourier_channels(xn, cd, sd)
    fs = []
    for base, batch, seq in parts:
        cs, ss = dft_tables(seq)
        fs.append(fourier_seq(cs, ss, yc, ys, base, batch, seq,
                              tm=min(1024, seq), tk=min(512, seq)))
    x, xn = matmul_res_norm(cat(fs), odd_fourier_w[0].astype(BF16), x, ffn_norm[1],
                            name="fourier_out")
    y = peer_layer(xn, peer_wq[1], peer_k1[1], peer_k2[1], peer_u[1], peer_v[1])
    return [add_rmsnorm_part(x, y, final_norm, base, batch * seq) for base, batch, seq in parts]


def kernel(x_prompt, x_sample, even_mix_norm, even_w_in, even_gate_bias, even_pool_w,
           even_pool_scale, even_w_out, odd_mix_norm, odd_fourier_w, ffn_norm, peer_wq,
           peer_k1, peer_k2, peer_u, peer_v, final_norm):
    bp, sp, d = x_prompt.shape
    bs, ss, _ = x_sample.shape
    tp = bp * sp
    x, xn = concat_rmsnorm(x_prompt.reshape(tp, d), x_sample.reshape(bs * ss, d), even_mix_norm[0])
    parts = ((0, bp, sp), (tp, bs, ss))
    yp, ys = _trunk(x, xn, parts, even_mix_norm, even_w_in, even_gate_bias, even_pool_w,
                    even_pool_scale, even_w_out, odd_mix_norm, odd_fourier_w, ffn_norm,
                    peer_wq, peer_k1, peer_k2, peer_u, peer_v, final_norm)
    return yp.reshape(bp, sp, d), ys.reshape(bs, ss, d)
```

```python
import functools
import math

import jax
import jax.numpy as jnp
from jax import lax
from jax.experimental import pallas as pl
from jax.experimental.pallas import tpu as pltpu

F32 = jnp.float32
BF16 = jnp.bfloat16
I32 = jnp.int32
HI = lax.Precision.HIGHEST

D_MODEL = 2048
EPS = 1e-6
N_HEADS = 4
HEAD_DIM = 256
D_A = N_HEADS * HEAD_DIM
D_B = D_MODEL - D_A
N_GATES = 4 * N_HEADS
CHUNK = 128
MLSTM_HEADS_PER_STEP = 2
N_POOL_GROUPS = 4
POOL_TILE = 256
POOL_GROUP_DIM = D_B // N_POOL_GROUPS
N_FOURIER_GROUPS = 4
DFT_FINE_ROWS = 64
FOURIER_GROUP_DIM = D_MODEL // N_FOURIER_GROUPS
PEER_HEADS = 8
PEER_NKEYS = 128
PEER_HALF = 128
PEER_TOPK = 16
N_EXPERTS = PEER_NKEYS * PEER_NKEYS
KEY_SENTINEL = 1e9

LANES = 128
PEER_BUILD_GROUP = 16
VMEM_LIMIT = 56 * 1024 * 1024


def _cparams(n_axes):
    return pltpu.CompilerParams(dimension_semantics=("arbitrary",) * n_axes,
                                vmem_limit_bytes=VMEM_LIMIT)


def _rms(x, g):
    ms = jnp.mean(x * x, axis=-1, keepdims=True)
    return (x * lax.rsqrt(ms + EPS)) * g


def _norm_kernel(a_ref, g_ref, xn_ref):
    xn_ref[...] = _rms(a_ref[...], g_ref[...])


def _addnorm_kernel(a_ref, b_ref, g_ref, x_ref, xn_ref):
    x = a_ref[...] + b_ref[...]
    x_ref[...] = x
    xn_ref[...] = _rms(x, g_ref[...])


def _concat_norm_kernel(a_ref, b_ref, g_ref, x_ref, xn_ref, *, na):
    i = pl.program_id(0)

    @pl.when(i < na)
    def _():
        x_ref[...] = a_ref[...]

    @pl.when(i >= na)
    def _():
        x_ref[...] = b_ref[...]

    xn_ref[...] = _rms(x_ref[...], g_ref[...])


def concat_rmsnorm(a, b, g, tm=256):
    d = a.shape[1]
    na, nb = a.shape[0] // tm, b.shape[0] // tm
    row = pl.BlockSpec((tm, d), lambda i: (i, 0))
    return pl.pallas_call(
        functools.partial(_concat_norm_kernel, na=na),
        grid=(na + nb,),
        in_specs=[pl.BlockSpec((tm, d), lambda i: (jnp.minimum(i, na - 1), 0)),
                  pl.BlockSpec((tm, d), lambda i: (jnp.maximum(i - na, 0), 0)),
                  pl.BlockSpec((1, d), lambda i: (0, 0))],
        out_specs=[row, row],
        out_shape=[jax.ShapeDtypeStruct((a.shape[0] + b.shape[0], d), F32)] * 2,
        compiler_params=_cparams(1),
        name="concat_rmsnorm",
    )(a, b, g.reshape(1, d))


def rmsnorm(a, g, tm=256):
    t, d = a.shape
    tm = min(tm, t)
    row = pl.BlockSpec((tm, d), lambda i: (i, 0))
    return pl.pallas_call(
        _norm_kernel,
        grid=(t // tm,),
        in_specs=[row, pl.BlockSpec((1, d), lambda i: (0, 0))],
        out_specs=row,
        out_shape=jax.ShapeDtypeStruct((t, d), F32),
        compiler_params=_cparams(1),
        name="rmsnorm",
    )(a, g.reshape(1, d))


def add_rmsnorm(a, b, g, tm=256):
    t, d = a.shape
    tm = min(tm, t)
    row = pl.BlockSpec((tm, d), lambda i: (i, 0))
    return pl.pallas_call(
        _addnorm_kernel,
        grid=(t // tm,),
        in_specs=[row, row, pl.BlockSpec((1, d), lambda i: (0, 0))],
        out_specs=[row, row],
        out_shape=[jax.ShapeDtypeStruct((t, d), F32)] * 2,
        compiler_params=_cparams(1),
        name="add_rmsnorm",
    )(a, b, g.reshape(1, d))


def _addnorm_out_kernel(a_ref, b_ref, g_ref, xn_ref):
    xn_ref[...] = _rms(a_ref[...] + b_ref[...], g_ref[...])


def add_rmsnorm_part(a, b, g, base_tok, rows, tm=256):
    d = a.shape[1]
    tm = min(tm, rows)
    src = pl.BlockSpec((tm, d), lambda i: (base_tok // tm + i, 0))
    return pl.pallas_call(
        _addnorm_out_kernel,
        grid=(rows // tm,),
        in_specs=[src, src, pl.BlockSpec((1, d), lambda i: (0, 0))],
        out_specs=pl.BlockSpec((tm, d), lambda i: (i, 0)),
        out_shape=jax.ShapeDtypeStruct((rows, d), F32),
        compiler_params=_cparams(1),
        name="final_norm",
    )(a, b, g.reshape(1, d))


def _mm_kernel(a_ref, w_ref, o_ref):
    o_ref[...] = jnp.dot(a_ref[...].astype(BF16), w_ref[...],
                         preferred_element_type=F32).astype(o_ref.dtype)


def _mm_res_kernel(a_ref, w_ref, r_ref, o_ref):
    acc = jnp.dot(a_ref[...].astype(BF16), w_ref[...], preferred_element_type=F32)
    o_ref[...] = (acc + r_ref[...]).astype(o_ref.dtype)


def _mm_hi_kernel(a_ref, w_ref, o_ref):
    o_ref[...] = jnp.dot(a_ref[...], w_ref[...], precision=HI,
                         preferred_element_type=F32)


def _mm_res_norm_kernel(a_ref, w_ref, r_ref, g_ref, o_ref, xn_ref):
    x = jnp.dot(a_ref[...].astype(BF16), w_ref[...], preferred_element_type=F32) + r_ref[...]
    o_ref[...] = x
    xn_ref[...] = _rms(x, g_ref[...])


def matmul_res_norm(a, w, res, g, tm=256, name="matmul_res_norm"):
    m, k = a.shape
    n = w.shape[1]
    tm = min(tm, m)
    row = pl.BlockSpec((tm, n), lambda i: (i, 0))
    return pl.pallas_call(
        _mm_res_norm_kernel,
        grid=(m // tm,),
        in_specs=[pl.BlockSpec((tm, k), lambda i: (i, 0)),
                  pl.BlockSpec((k, n), lambda i: (0, 0)),
                  row,
                  pl.BlockSpec((1, n), lambda i: (0, 0))],
        out_specs=[row, row],
        out_shape=[jax.ShapeDtypeStruct((m, n), F32)] * 2,
        compiler_params=_cparams(1),
        name=name,
    )(a, w, res, g.reshape(1, n))


def matmul(a, w, res=None, tm=512, tn=512, out_dtype=F32, name="matmul"):
    m, k = a.shape
    n = w.shape[1]
    tm = min(tm, m)
    tn = min(tn, n)
    in_specs = [pl.BlockSpec((tm, k), lambda i, j: (i, 0)),
                pl.BlockSpec((k, tn), lambda i, j: (0, j))]
    args = [a, w]
    body = _mm_kernel
    if res is not None:
        in_specs.append(pl.BlockSpec((tm, tn), lambda i, j: (i, j)))
        args.append(res)
        body = _mm_res_kernel
    return pl.pallas_call(
        body,
        grid=(m // tm, n // tn),
        in_specs=in_specs,
        out_specs=pl.BlockSpec((tm, tn), lambda i, j: (i, j)),
        out_shape=jax.ShapeDtypeStruct((m, n), out_dtype),
        compiler_params=_cparams(2),
        name=name,
    )(*args)


def matmul_hi(a, w, tm=512):
    m, k = a.shape
    n = w.shape[1]
    tm = min(tm, m)
    return pl.pallas_call(
        _mm_hi_kernel,
        grid=(m // tm,),
        in_specs=[pl.BlockSpec((tm, k), lambda i: (i, 0)),
                  pl.BlockSpec((k, n), lambda i: (0, 0))],
        out_specs=pl.BlockSpec((tm, n), lambda i: (i, 0)),
        out_shape=jax.ShapeDtypeStruct((m, n), F32),
        compiler_params=_cparams(1),
        name="matmul_hi",
    )(a, w)


def _log_sigmoid(x):
    return jnp.minimum(x, 0.0) - jnp.log(1.0 + jnp.exp(-jnp.abs(x)))


def _mlstm_kernel(q_ref, k_ref, v_ref, gc_ref, gr_ref, bc_ref, br_ref, h_ref,
                  c_ref, n_ref, m_ref):
    d = pl.program_id(0)
    hp = pl.program_id(2)
    c = pl.program_id(3)
    L = CHUNK

    @pl.when(c == 0)
    def _():
        c_ref[...] = jnp.zeros_like(c_ref)
        n_ref[...] = jnp.zeros_like(n_ref)
        m_ref[...] = jnp.zeros_like(m_ref)

    gcol = gc_ref[...] + bc_ref[...]
    grow = gr_ref[...] + br_ref[...]
    lane = lax.broadcasted_iota(I32, gcol.shape, 1)
    sub = lax.broadcasted_iota(I32, grow.shape, 0)
    r = lax.broadcasted_iota(I32, (L, L), 0)
    cc = lax.broadcasted_iota(I32, (L, L), 1)
    sign = 1 - 2 * d
    mask = (cc - r) * sign <= 0
    mask_t = (r - cc) * sign <= 0

    for hh in range(MLSTM_HEADS_PER_STEP):
        cols = slice(hh * HEAD_DIM, (hh + 1) * HEAD_DIM)
        q = q_ref[:, cols] * (HEAD_DIM ** -0.5)
        k = k_ref[:, cols]
        v = v_ref[:, cols]
        li_idx = 2 * d * N_HEADS + hp * MLSTM_HEADS_PER_STEP + hh
        lf_idx = li_idx + N_HEADS
        li_col = jnp.sum(jnp.where(lane == li_idx, gcol, 0.0), axis=1, keepdims=True)
        gf_col = jnp.sum(jnp.where(lane == lf_idx, gcol, 0.0), axis=1, keepdims=True)
        li_row = jnp.sum(jnp.where(sub == li_idx, grow, 0.0), axis=0, keepdims=True)
        gf_row = jnp.sum(jnp.where(sub == lf_idx, grow, 0.0), axis=0, keepdims=True)
        lf_col = _log_sigmoid(gf_col)
        lf_row = _log_sigmoid(gf_row)
        b_col = jnp.sum(jnp.where(mask, lf_row, 0.0), axis=1, keepdims=True)
        b_row = jnp.sum(jnp.where(mask_t, lf_col, 0.0), axis=0, keepdims=True)
        logw = jnp.where(mask, b_col - b_row + li_row, -jnp.inf)

        m_prev = m_ref[hh, :, 0:1]
        inter = b_col + m_prev
        m_t = jnp.maximum(inter, jnp.max(logw, axis=1, keepdims=True))
        w = jnp.exp(logw - m_t)
        a_inter = jnp.exp(inter - m_t)
        qb = q.astype(BF16)
        vb = v.astype(BF16)
        s = lax.dot_general(qb, k.astype(BF16), (((1,), (1,)), ((), ())),
                            preferred_element_type=F32) * w
        cm = c_ref[hh]
        nv = n_ref[hh]
        num = (jnp.dot(s.astype(BF16), vb, preferred_element_type=F32)
               + a_inter * jnp.dot(qb, cm.astype(BF16), preferred_element_type=F32))
        den = (jnp.sum(s, axis=1, keepdims=True)
               + a_inter * jnp.sum(q * nv, axis=1, keepdims=True))
        h_ref[:, cols] = num / jnp.maximum(jnp.abs(den), jnp.exp(-m_t))

        b_last = jnp.sum(lf_row, axis=1, keepdims=True)
        logu = b_last - b_col + li_col
        m_new = jnp.maximum(b_last + m_prev, jnp.max(logu, axis=0, keepdims=True))
        u = jnp.exp(logu - m_new)
        decay = jnp.exp(b_last + m_prev - m_new)
        uk = u * k
        c_ref[hh] = decay * cm + lax.dot_general(
            uk.astype(BF16), vb, (((0,), (0,)), ((), ())), preferred_element_type=F32)
        n_ref[hh] = decay * nv + jnp.sum(uk, axis=0, keepdims=True)
        m_ref[hh] = jnp.broadcast_to(m_new, (1, LANES))


def mlstm(z, gates, gates_t, bias_row, bias_col, base_tok, batch, seq):
    nc = seq // CHUNK
    base = base_tok // CHUNK
    hps = MLSTM_HEADS_PER_STEP
    ngrp = N_HEADS // hps

    def blk(dd, b, c):
        return base + b * nc + c + dd * (nc - 1 - 2 * c)

    def sec(which):
        return pl.BlockSpec((CHUNK, hps * HEAD_DIM),
                            lambda dd, b, h, c: (blk(dd, b, c), which * ngrp + h))

    return pl.pallas_call(
        _mlstm_kernel,
        grid=(2, batch, ngrp, nc),
        in_specs=[sec(0), sec(1), sec(2),
                  pl.BlockSpec((CHUNK, LANES), lambda dd, b, h, c: (blk(dd, b, c), 0)),
                  pl.BlockSpec((N_GATES, CHUNK), lambda dd, b, h, c: (0, blk(dd, b, c))),
                  pl.BlockSpec((1, LANES), lambda dd, b, h, c: (0, 0)),
                  pl.BlockSpec((N_GATES, 1), lambda dd, b, h, c: (0, 0))],
        out_specs=pl.BlockSpec((None, CHUNK, hps * HEAD_DIM),
                               lambda dd, b, h, c: (dd, blk(dd, b, c) - base, h)),
        out_shape=jax.ShapeDtypeStruct((2, batch * seq, D_A), F32),
        scratch_shapes=[pltpu.VMEM((hps, HEAD_DIM, HEAD_DIM), F32),
                        pltpu.VMEM((hps, 1, HEAD_DIM), F32),
                        pltpu.VMEM((hps, 1, LANES), F32)],
        compiler_params=_cparams(4),
        name="mlstm",
    )(z, z, z, gates, gates_t, bias_row, bias_col)


def _pool_kernel(prev_ref, mid_ref, next_ref, pw_ref, sc_ref, o_ref, *, seq):
    i = pl.program_id(1)
    g = pl.program_id(2)
    half = jnp.left_shift(1, g)
    T = POOL_TILE
    p = i * T + lax.broadcasted_iota(I32, (T, T), 0)
    lo = jnp.maximum(p - half, 0)
    hi = jnp.minimum(p + half, seq)
    acc = jnp.zeros((T, POOL_GROUP_DIM), F32)
    for off, ref in ((-T, prev_ref), (0, mid_ref), (T, next_ref)):
        pj = i * T + off + lax.broadcasted_iota(I32, (T, T), 1)
        band = jnp.where((pj >= lo) & (pj < hi), 1.0, 0.0).astype(BF16)
        u = ref[...]
        u_hi = u.astype(BF16)
        u_lo = (u - u_hi.astype(F32)).astype(BF16)
        acc = acc + (jnp.dot(band, u_hi, preferred_element_type=F32)
                     + jnp.dot(band, u_lo, preferred_element_type=F32))
    cnt = (hi - lo)[:, 0:1].astype(F32)
    pooled = acc / cnt - mid_ref[...]
    y = jnp.dot(pooled.astype(BF16), pw_ref[...], preferred_element_type=F32)
    o_ref[...] = y * sc_ref[...]


def pool(z, pool_w, pool_scale, col0, base_tok, batch, seq):
    nt = seq // POOL_TILE
    base = base_tok // POOL_TILE
    cb = col0 // POOL_GROUP_DIM

    def spec(shift):
        return pl.BlockSpec(
            (POOL_TILE, POOL_GROUP_DIM),
            lambda b, i, g: (base + b * nt + jnp.clip(i + shift, 0, nt - 1), cb + g))

    return pl.pallas_call(
        functools.partial(_pool_kernel, seq=seq),
        grid=(batch, nt, N_POOL_GROUPS),
        in_specs=[spec(-1), spec(0), spec(1),
                  pl.BlockSpec((None, POOL_GROUP_DIM, POOL_GROUP_DIM), lambda b, i, g: (g, 0, 0)),
                  pl.BlockSpec((1, POOL_GROUP_DIM), lambda b, i, g: (0, g))],
        out_specs=pl.BlockSpec((POOL_TILE, POOL_GROUP_DIM), lambda b, i, g: (b * nt + i, g)),
        out_shape=jax.ShapeDtypeStruct((batch * seq, D_B), F32),
        compiler_params=_cparams(3),
        name="pool",
    )(z, z, z, pool_w, pool_scale.reshape(1, D_B))


def _outproj_kernel(hf_ref, hb_ref, o_ref, yb_ref, wa_ref, wb_ref, x_ref, g_ref, out_ref, xn_ref):
    ha = (hf_ref[...] + hb_ref[...]) * jax.nn.sigmoid(o_ref[...])
    acc = jnp.dot(ha.astype(BF16), wa_ref[...], preferred_element_type=F32)
    acc = acc + jnp.dot(yb_ref[...].astype(BF16), wb_ref[...], preferred_element_type=F32)
    x = acc + x_ref[...]
    out_ref[...] = x
    xn_ref[...] = _rms(x, g_ref[...])


def outproj(h2, z, yb, w_out, x, g, tm=256):
    t, d = x.shape
    tm = min(tm, t)
    ocb = 3 * D_A // D_A
    row = pl.BlockSpec((tm, d), lambda i: (i, 0))
    return pl.pallas_call(
        _outproj_kernel,
        grid=(t // tm,),
        in_specs=[pl.BlockSpec((None, tm, D_A), lambda i: (0, i, 0)),
                  pl.BlockSpec((None, tm, D_A), lambda i: (1, i, 0)),
                  pl.BlockSpec((tm, D_A), lambda i: (i, ocb)),
                  pl.BlockSpec((tm, D_B), lambda i: (i, 0)),
                  pl.BlockSpec((D_A, d), lambda i: (0, 0)),
                  pl.BlockSpec((D_B, d), lambda i: (D_A // D_B, 0)),
                  row,
                  pl.BlockSpec((1, d), lambda i: (0, 0))],
        out_specs=[row, row],
        out_shape=[jax.ShapeDtypeStruct((t, d), F32)] * 2,
        compiler_params=_cparams(1),
        name="outproj",
    )(h2, h2, z, yb, w_out, w_out, x, g.reshape(1, d))


def _dft_rows_kernel(c_ref, s_ref, *, n, stride, scale):
    r = lax.broadcasted_iota(I32, c_ref.shape, 0) * stride
    c = lax.broadcasted_iota(I32, c_ref.shape, 1)
    ang = ((r * c) & (n - 1)).astype(F32) * (2.0 * math.pi / n)
    c_ref[...] = jnp.cos(ang) * scale
    s_ref[...] = jnp.sin(ang) * scale


def _dft_rows(n, rows, stride, scale):
    return pl.pallas_call(
        functools.partial(_dft_rows_kernel, n=n, stride=stride, scale=scale),
        out_shape=[jax.ShapeDtypeStruct((rows, n), F32)] * 2,
        compiler_params=_cparams(0),
        name="dft_rows",
    )()


def _dft_combine_kernel(ch_ref, sh_ref, cl_ref, sl_ref, c_ref, s_ref):
    ch, sh = ch_ref[...], sh_ref[...]
    cl, sl = cl_ref[...], sl_ref[...]
    c_ref[...] = (ch * cl - sh * sl).astype(BF16)
    s_ref[...] = (sh * cl + ch * sl).astype(BF16)


def dft_tables(n):
    assert n & (n - 1) == 0
    r = DFT_FINE_ROWS
    nh = n // r
    cl, sl = _dft_rows(n, r, 1, n ** -0.5)
    ch, sh = _dft_rows(n, nh, r, 1.0)
    coarse = pl.BlockSpec((None, 1, n), lambda i: (i, 0, 0))
    fine = pl.BlockSpec((r, n), lambda i: (0, 0))
    blk = pl.BlockSpec((r, n), lambda i: (i, 0))
    return pl.pallas_call(
        _dft_combine_kernel,
        grid=(nh,),
        in_specs=[coarse, coarse, fine, fine],
        out_specs=[blk, blk],
        out_shape=[jax.ShapeDtypeStruct((n, n), BF16)] * 2,
        compiler_params=_cparams(1),
        name="dft_tables",
    )(ch.reshape(nh, 1, n), sh.reshape(nh, 1, n), cl, sl)


def _fourier_ch_kernel(x_ref, cd_ref, sd_ref, yc_ref, ys_ref):
    xb = x_ref[...].astype(BF16)
    yc_ref[...] = jnp.dot(xb, cd_ref[...], preferred_element_type=F32).astype(BF16)
    ys_ref[...] = jnp.dot(xb, sd_ref[...], preferred_element_type=F32).astype(BF16)


def fourier_channels(xn, cd, sd, tm=512):
    t = xn.shape[0]
    tm = min(tm, t)
    gd = FOURIER_GROUP_DIM
    tile = pl.BlockSpec((tm, gd), lambda i, g: (i, g))
    tab = pl.BlockSpec((gd, gd), lambda i, g: (0, 0))
    return pl.pallas_call(
        _fourier_ch_kernel,
        grid=(t // tm, N_FOURIER_GROUPS),
        in_specs=[tile, tab, tab],
        out_specs=[tile, tile],
        out_shape=[jax.ShapeDtypeStruct((t, D_MODEL), BF16)] * 2,
        compiler_params=_cparams(2),
        name="fourier_channels",
    )(xn, cd, sd)


def _fourier_seq_kernel(cs_ref, ss_ref, yc_ref, ys_ref, o_ref, acc_ref):
    kk = pl.program_id(3)

    @pl.when(kk == 0)
    def _():
        acc_ref[...] = jnp.zeros_like(acc_ref)

    acc_ref[...] += (jnp.dot(cs_ref[...], yc_ref[...], preferred_element_type=F32)
                     - jnp.dot(ss_ref[...], ys_ref[...], preferred_element_type=F32))

    @pl.when(kk == pl.num_programs(3) - 1)
    def _():
        o_ref[...] = acc_ref[...].astype(o_ref.dtype)


def fourier_seq(cs, ss, yc, ys, base_tok, batch, seq, tm=1024, tn=1024, tk=512):
    nk = seq // tk
    base = base_tok // tk
    a_spec = pl.BlockSpec((tm, tk), lambda b, i, j, k: (i, k))
    y_spec = pl.BlockSpec((tk, tn), lambda b, i, j, k: (base + b * nk + k, j))
    return pl.pallas_call(
        _fourier_seq_kernel,
        grid=(batch, seq // tm, D_MODEL // tn, nk),
        in_specs=[a_spec, a_spec, y_spec, y_spec],
        out_specs=pl.BlockSpec((tm, tn), lambda b, i, j, k: (b * (seq // tm) + i, j)),
        out_shape=jax.ShapeDtypeStruct((batch * seq, D_MODEL), BF16),
        scratch_shapes=[pltpu.VMEM((tm, tn), F32)],
        compiler_params=_cparams(4),
        name="fourier_seq",
    )(cs, ss, yc, ys)


def _top16_rows(s, row_key):
    n = s.shape[1]
    out_iota = lax.broadcasted_iota(I32, (PEER_TOPK, n), 0)
    vals = jnp.zeros((PEER_TOPK, n), F32)
    keys = jnp.zeros((PEER_TOPK, n), F32)
    for rnk in range(PEER_TOPK):
        mx = jnp.max(s, axis=0, keepdims=True)
        kmin = jnp.min(jnp.where(s == mx, row_key, KEY_SENTINEL), axis=0, keepdims=True)
        vals = jnp.where(out_iota == rnk, mx, vals)
        keys = jnp.where(out_iota == rnk, kmin, keys)
        s = jnp.where(row_key == kmin, -jnp.inf, s)
    return vals, keys


def _peer_topk_kernel(q_ref, k1_ref, k2_ref, e_ref, g_ref):
    tt = q_ref.shape[0]
    k1 = k1_ref[...]
    k2 = k2_ref[...]
    fiota = lambda shape: lax.broadcasted_iota(I32, shape, 0).astype(F32)
    key_iota = fiota((PEER_NKEYS, tt))
    nt = (((1,), (1,)), ((), ()))
    blocks = [(a, 16 if a == 0 else 8) for a in range(8)]
    for h in range(PEER_HEADS):
        q1 = q_ref[:, h * 2 * PEER_HALF: h * 2 * PEER_HALF + PEER_HALF]
        q2 = q_ref[:, h * 2 * PEER_HALF + PEER_HALF: (h + 1) * 2 * PEER_HALF]
        s1 = lax.dot_general(k1, q1, nt, precision=HI, preferred_element_type=F32)
        s2 = lax.dot_general(k2, q2, nt, precision=HI, preferred_element_type=F32)
        v1, i1 = _top16_rows(s1, key_iota)
        v2, i2 = _top16_rows(s2, key_iota)
        cands, ids, orders = [], [], []
        for a, rows in blocks:
            bi = fiota((rows, tt))
            ok = (bi + 1.0) * (a + 1) <= PEER_TOPK
            cands.append(jnp.where(ok, v1[a:a + 1, :] + v2[0:rows, :], -jnp.inf))
            ids.append(i1[a:a + 1, :] * PEER_NKEYS + i2[0:rows, :])
            orders.append(a * PEER_TOPK + bi)
        ai = fiota((8, tt)) + 8.0
        cands.append(v1[8:16, :] + v2[0:1, :])
        ids.append(i1[8:16, :] * PEER_NKEYS + i2[0:1, :])
        orders.append(ai * PEER_TOPK)
        cand = jnp.concatenate(cands, axis=0)
        cid = jnp.concatenate(ids, axis=0)
        order = jnp.concatenate(orders, axis=0)
        ts, tpos = _top16_rows(cand, order)
        experts = jnp.zeros((PEER_TOPK, tt), F32)
        out_iota = lax.broadcasted_iota(I32, (PEER_TOPK, tt), 0)
        for rnk in range(PEER_TOPK):
            sel = jnp.max(jnp.where(order == tpos[rnk:rnk + 1, :], cid, -1.0), axis=0, keepdims=True)
            experts = jnp.where(out_iota == rnk, sel, experts)
        ex = jnp.exp(ts - ts[0:1, :])
        gate = ex / jnp.sum(ex, axis=0, keepdims=True)
        e_ref[h * PEER_TOPK:(h + 1) * PEER_TOPK, :] = experts.astype(I32)
        g_ref[h * PEER_TOPK:(h + 1) * PEER_TOPK, :] = gate


def peer_topk(q, k1, k2, tt=256):
    t = q.shape[0]
    tt = min(tt, t)
    ne = PEER_HEADS * PEER_TOPK
    keys = pl.BlockSpec((PEER_NKEYS, PEER_HALF), lambda i: (0, 0))
    out = pl.BlockSpec((ne, tt), lambda i: (0, i))
    return pl.pallas_call(
        _peer_topk_kernel,
        grid=(t // tt,),
        in_specs=[pl.BlockSpec((tt, q.shape[1]), lambda i: (i, 0)), keys, keys],
        out_specs=[out, out],
        out_shape=[jax.ShapeDtypeStruct((ne, t), I32), jax.ShapeDtypeStruct((ne, t), F32)],
        compiler_params=_cparams(1),
        name="peer_topk",
    )(q, k1, k2)


def _gelu_exact(x):
    return 0.5 * x * (1.0 + lax.erf(x * (2.0 ** -0.5)))


def _peer_dense_kernel(i1a_ref, i2a_ref, ga_ref, i1b_ref, i2b_ref, gb_ref, xn_ref, ut_ref, v_ref,
                       out_ref, gate_ref, stage_ref, *, tm, eb):
    i = pl.program_id(0)
    j = pl.program_id(1)
    nk = PEER_NKEYS
    grp = PEER_BUILD_GROUP
    par = i % 2
    key = lax.broadcasted_iota(I32, (nk, nk), 0)
    nt = (((1,), (1,)), ((), ()))

    def build_group(i1_ref, i2_ref, g_ref, t0, slot):
        for k in range(grp):
            t = t0 + k
            a_t = jnp.where(key == i1_ref[pl.ds(t, 1), :], 1.0, 0.0).astype(BF16)
            b_t = jnp.where(key == i2_ref[pl.ds(t, 1), :], g_ref[pl.ds(t, 1), :],
                            0.0).astype(BF16)
            g_t = lax.dot_general(a_t, b_t, nt, preferred_element_type=F32)
            stage_ref[pl.ds(k, nk, stride=grp), :] = g_t
        for r in range(nk):
            gate_ref[slot, pl.ds(pl.multiple_of(r * tm + t0, grp), grp), :] = (
                stage_ref[r * grp:(r + 1) * grp, :].astype(BF16))

    @pl.when(j == 0)
    def _():
        out_ref[...] = jnp.zeros_like(out_ref)

    @pl.when((i == 0) & (j == 0))
    def _():
        def first(tq, carry):
            build_group(i1a_ref, i2a_ref, ga_ref, pl.multiple_of(tq * grp, grp), 0)
            return carry

        lax.fori_loop(0, tm // grp, first, 0)

    build_group(i1b_ref, i2b_ref, gb_ref, pl.multiple_of((j * grp) % tm, grp), 1 - par)

    act = jnp.dot(xn_ref[...], ut_ref[...], preferred_element_type=F32)
    nb = eb // nk
    z = [(_gelu_exact(act[:, b * nk:(b + 1) * nk])
          * gate_ref[par, pl.ds(pl.multiple_of((j * nb + b) * tm, tm), tm), :].astype(F32)
          ).astype(BF16)
         for b in range(nb)]
    out_ref[...] += jnp.dot(jnp.concatenate(z, axis=1), v_ref[...], preferred_element_type=F32)


def peer_dense(xn, expert_t, gate_t, u_tab, v_tab, tm=512, eb=512):
    t, d = xn.shape
    tm = min(tm, t)
    nti = t // tm
    nsteps = N_EXPERTS // eb
    assert tm % PEER_BUILD_GROUP == 0 and tm <= nsteps * PEER_BUILD_GROUP
    ns = expert_t.shape[0]
    i1 = (expert_t // PEER_NKEYS).T
    i2 = (expert_t % PEER_NKEYS).T
    g = gate_t.T
    slot_a = pl.BlockSpec((tm, ns), lambda i, j: (0, 0))
    slot_b = pl.BlockSpec((tm, ns), lambda i, j: (jnp.minimum(i + 1, nti - 1), 0))
    row = pl.BlockSpec((tm, d), lambda i, j: (i, 0), pipeline_mode=pl.Buffered(1))
    return pl.pallas_call(
        functools.partial(_peer_dense_kernel, tm=tm, eb=eb),
        grid=(nti, nsteps),
        in_specs=[slot_a, slot_a, slot_a, slot_b, slot_b, slot_b, row,
                  pl.BlockSpec((d, eb), lambda i, j: (0, j)),
                  pl.BlockSpec((eb, d), lambda i, j: (j, 0))],
        out_specs=row,
        out_shape=jax.ShapeDtypeStruct((t, d), F32),
        scratch_shapes=[pltpu.VMEM((2, PEER_NKEYS * tm, PEER_NKEYS), BF16),
                        pltpu.VMEM((PEER_NKEYS * PEER_BUILD_GROUP, PEER_NKEYS), F32)],
        compiler_params=_cparams(2),
        name="peer_dense",
    )(i1, i2, g, i1, i2, g, xn.astype(BF16), u_tab.astype(BF16).T, v_tab.astype(BF16))


def peer_layer(xn, wq, k1, k2, u_tab, v_tab):
    q = matmul(xn, wq.astype(BF16), tn=1024, name="peer_query")
    expert_t, gate_t = peer_topk(q, k1, k2)
    return peer_dense(xn, expert_t, gate_t, u_tab, v_tab)


def _trunk(x, xn, parts, even_mix_norm, even_w_in, even_gate_bias, even_pool_w, even_pool_scale,
           even_w_out, odd_mix_norm, odd_fourier_w, ffn_norm, peer_wq, peer_k1, peer_k2,
           peer_u, peer_v, final_norm):
    cat = lambda xs: xs[0] if len(xs) == 1 else jnp.concatenate(xs, axis=0)

    n_main = 4 * D_A + D_B
    w_in = even_w_in[0]
    z = matmul(xn, w_in[:, :n_main].astype(BF16), tn=1024, name="in_proj")
    w_gate = jnp.pad(w_in[:, n_main:], ((0, 0), (0, LANES - N_GATES)))
    gates = matmul_hi(xn, w_gate)
    gates_t = gates[:, :N_GATES].T
    bias = even_gate_bias[0].astype(F32)
    bias_row = jnp.pad(bias, (0, LANES - N_GATES)).reshape(1, LANES)
    bias_col = bias.reshape(N_GATES, 1)
    pw = even_pool_w[0].astype(BF16)
    hs, ybs = [], []
    for base, batch, seq in parts:
        hs.append(mlstm(z, gates, gates_t, bias_row, bias_col, base, batch, seq))
        ybs.append(pool(z, pw, even_pool_scale[0], 4 * D_A, base, batch, seq))
    h2 = hs[0] if len(hs) == 1 else jnp.concatenate(hs, axis=1)
    x, xn = outproj(h2, z, cat(ybs), even_w_out[0].astype(BF16), x, ffn_norm[0])
    y = peer_layer(xn, peer_wq[0], peer_k1[0], peer_k2[0], peer_u[0], peer_v[0])

    x, xn = add_rmsnorm(x, y, odd_mix_norm[0])
    cd, sd = dft_tables(FOURIER_GROUP_DIM)
    yc, ys = fourier_channels(xn, cd, sd)
    fs = []
    for base, batch, seq in parts:
        cs, ss = dft_tables(seq)
        fs.append(fourier_seq(cs, ss, yc, ys, base, batch, seq,
                              tm=min(1024, seq), tk=min(512, seq)))
    x, xn = matmul_res_norm(cat(fs), odd_fourier_w[0].astype(BF16), x, ffn_norm[1],
                            name="fourier_out")
    y = peer_layer(xn, peer_wq[1], peer_k1[1], peer_k2[1], peer_u[1], peer_v[1])
    return [add_rmsnorm_part(x, y, final_norm, base, batch * seq) for base, batch, seq in parts]


def kernel(x_prompt, x_sample, even_mix_norm, even_w_in, even_gate_bias, even_pool_w,
           even_pool_scale, even_w_out, odd_mix_norm, odd_fourier_w, ffn_norm, peer_wq,
           peer_k1, peer_k2, peer_u, peer_v, final_norm):
    bp, sp, d = x_prompt.shape
    bs, ss, _ = x_sample.shape
    tp = bp * sp
    x, xn = concat_rmsnorm(x_prompt.reshape(tp, d), x_sample.reshape(bs * ss, d), even_mix_norm[0])
    parts = ((0, bp, sp), (tp, bs, ss))
    yp, ys = _trunk(x, xn, parts, even_mix_norm, even_w_in, even_gate_bias, even_pool_w,
                    even_pool_scale, even_w_out, odd_mix_norm, odd_fourier_w, ffn_norm,
                    peer_wq, peer_k1, peer_k2, peer_u, peer_v, final_norm)
    return yp.reshape(bp, sp, d), ys.reshape(bs, ss, d)
```

```python
import functools
import math

import jax
import jax.numpy as jnp
from jax import lax
from jax.experimental import pallas as pl
from jax.experimental.pallas import tpu as pltpu

F32 = jnp.float32
BF16 = jnp.bfloat16
I32 = jnp.int32
HI = lax.Precision.HIGHEST

D_MODEL = 2048
EPS = 1e-6
N_HEADS = 4
HEAD_DIM = 256
D_A = N_HEADS * HEAD_DIM
D_B = D_MODEL - D_A
N_GATES = 4 * N_HEADS
CHUNK = 128
MLSTM_HEADS_PER_STEP = 2
N_POOL_GROUPS = 4
POOL_TILE = 256
POOL_GROUP_DIM = D_B // N_POOL_GROUPS
N_FOURIER_GROUPS = 4
DFT_FINE_ROWS = 64
FOURIER_GROUP_DIM = D_MODEL // N_FOURIER_GROUPS
PEER_HEADS = 8
PEER_NKEYS = 128
PEER_HALF = 128
PEER_TOPK = 16
N_EXPERTS = PEER_NKEYS * PEER_NKEYS
KEY_SENTINEL = 1e9

LANES = 128
PEER_BUILD_GROUP = 16
VMEM_LIMIT = 56 * 1024 * 1024


def _cparams(n_axes):
    return pltpu.CompilerParams(dimension_semantics=("arbitrary",) * n_axes,
                                vmem_limit_bytes=VMEM_LIMIT)


def _rms(x, g):
    ms = jnp.mean(x * x, axis=-1, keepdims=True)
    return (x * lax.rsqrt(ms + EPS)) * g


def _norm_kernel(a_ref, g_ref, xn_ref):
    xn_ref[...] = _rms(a_ref[...], g_ref[...])


def _addnorm_kernel(a_ref, b_ref, g_ref, x_ref, xn_ref):
    x = a_ref[...] + b_ref[...]
    x_ref[...] = x
    xn_ref[...] = _rms(x, g_ref[...])


def _concat_norm_kernel(a_ref, b_ref, g_ref, x_ref, xn_ref, *, na):
    i = pl.program_id(0)

    @pl.when(i < na)
    def _():
        x_ref[...] = a_ref[...]

    @pl.when(i >= na)
    def _():
        x_ref[...] = b_ref[...]

    xn_ref[...] = _rms(x_ref[...], g_ref[...])


def concat_rmsnorm(a, b, g, tm=256):
    d = a.shape[1]
    na, nb = a.shape[0] // tm, b.shape[0] // tm
    row = pl.BlockSpec((tm, d), lambda i: (i, 0))
    return pl.pallas_call(
        functools.partial(_concat_norm_kernel, na=na),
        grid=(na + nb,),
        in_specs=[pl.BlockSpec((tm, d), lambda i: (jnp.minimum(i, na - 1), 0)),
                  pl.BlockSpec((tm, d), lambda i: (jnp.maximum(i - na, 0), 0)),
                  pl.BlockSpec((1, d), lambda i: (0, 0))],
        out_specs=[row, row],
        out_shape=[jax.ShapeDtypeStruct((a.shape[0] + b.shape[0], d), F32)] * 2,
        compiler_params=_cparams(1),
        name="concat_rmsnorm",
    )(a, b, g.reshape(1, d))


def rmsnorm(a, g, tm=256):
    t, d = a.shape
    tm = min(tm, t)
    row = pl.BlockSpec((tm, d), lambda i: (i, 0))
    return pl.pallas_call(
        _norm_kernel,
        grid=(t // tm,),
        in_specs=[row, pl.BlockSpec((1, d), lambda i: (0, 0))],
        out_specs=row,
        out_shape=jax.ShapeDtypeStruct((t, d), F32),
        compiler_params=_cparams(1),
        name="rmsnorm",
    )(a, g.reshape(1, d))


def add_rmsnorm(a, b, g, tm=256):
    t, d = a.shape
    tm = min(tm, t)
    row = pl.BlockSpec((tm, d), lambda i: (i, 0))
    return pl.pallas_call(
        _addnorm_kernel,
        grid=(t // tm,),
        in_specs=[row, row, pl.BlockSpec((1, d), lambda i: (0, 0))],
        out_specs=[row, row],
        out_shape=[jax.ShapeDtypeStruct((t, d), F32)] * 2,
        compiler_params=_cparams(1),
        name="add_rmsnorm",
    )(a, b, g.reshape(1, d))


def _addnorm_out_kernel(a_ref, b_ref, g_ref, xn_ref):
    xn_ref[...] = _rms(a_ref[...] + b_ref[...], g_ref[...])


def add_rmsnorm_part(a, b, g, base_tok, rows, tm=256):
    d = a.shape[1]
    tm = min(tm, rows)
    src = pl.BlockSpec((tm, d), lambda i: (base_tok // tm + i, 0))
    return pl.pallas_call(
        _addnorm_out_kernel,
        grid=(rows // tm,),
        in_specs=[src, src, pl.BlockSpec((1, d), lambda i: (0, 0))],
        out_specs=pl.BlockSpec((tm, d), lambda i: (i, 0)),
        out_shape=jax.ShapeDtypeStruct((rows, d), F32),
        compiler_params=_cparams(1),
        name="final_norm",
    )(a, b, g.reshape(1, d))


def _mm_kernel(a_ref, w_ref, o_ref):
    o_ref[...] = jnp.dot(a_ref[...].astype(BF16), w_ref[...],
                         preferred_element_type=F32).astype(o_ref.dtype)


def _mm_res_kernel(a_ref, w_ref, r_ref, o_ref):
    acc = jnp.dot(a_ref[...].astype(BF16), w_ref[...], preferred_element_type=F32)
    o_ref[...] = (acc + r_ref[...]).astype(o_ref.dtype)


def _mm_hi_kernel(a_ref, w_ref, o_ref):
    o_ref[...] = jnp.dot(a_ref[...], w_ref[...], precision=HI,
                         preferred_element_type=F32)


def _mm_res_norm_kernel(aa_ref, ab_ref, w_ref, r_ref, g_ref, o_ref, xn_ref, *, na):
    a = jnp.where(pl.program_id(0) < na, aa_ref[...], ab_ref[...])
    x = jnp.dot(a.astype(BF16), w_ref[...], preferred_element_type=F32) + r_ref[...]
    o_ref[...] = x
    xn_ref[...] = _rms(x, g_ref[...])


def matmul_res_norm(a_parts, w, res, g, tm=256, name="matmul_res_norm"):
    k = a_parts[0].shape[1]
    m, n = res.shape
    na = a_parts[0].shape[0] // tm
    row = pl.BlockSpec((tm, n), lambda i: (i, 0))
    return pl.pallas_call(
        functools.partial(_mm_res_norm_kernel, na=na),
        grid=(m // tm,),
        in_specs=[pl.BlockSpec((tm, k), lambda i: (jnp.minimum(i, na - 1), 0)),
                  pl.BlockSpec((tm, k), lambda i: (jnp.maximum(i - na, 0), 0)),
                  pl.BlockSpec((k, n), lambda i: (0, 0)),
                  row,
                  pl.BlockSpec((1, n), lambda i: (0, 0))],
        out_specs=[row, row],
        out_shape=[jax.ShapeDtypeStruct((m, n), F32)] * 2,
        compiler_params=_cparams(1),
        name=name,
    )(a_parts[0], a_parts[1], w, res, g.reshape(1, n))


def matmul(a, w, res=None, tm=512, tn=512, out_dtype=F32, name="matmul"):
    m, k = a.shape
    n = w.shape[1]
    tm = min(tm, m)
    tn = min(tn, n)
    in_specs = [pl.BlockSpec((tm, k), lambda i, j: (i, 0)),
                pl.BlockSpec((k, tn), lambda i, j: (0, j))]
    args = [a, w]
    body = _mm_kernel
    if res is not None:
        in_specs.append(pl.BlockSpec((tm, tn), lambda i, j: (i, j)))
        args.append(res)
        body = _mm_res_kernel
    return pl.pallas_call(
        body,
        grid=(m // tm, n // tn),
        in_specs=in_specs,
        out_specs=pl.BlockSpec((tm, tn), lambda i, j: (i, j)),
        out_shape=jax.ShapeDtypeStruct((m, n), out_dtype),
        compiler_params=_cparams(2),
        name=name,
    )(*args)


def matmul_hi(a, w, tm=512):
    m, k = a.shape
    n = w.shape[1]
    tm = min(tm, m)
    return pl.pallas_call(
        _mm_hi_kernel,
        grid=(m // tm,),
        in_specs=[pl.BlockSpec((tm, k), lambda i: (i, 0)),
                  pl.BlockSpec((k, n), lambda i: (0, 0))],
        out_specs=pl.BlockSpec((tm, n), lambda i: (i, 0)),
        out_shape=jax.ShapeDtypeStruct((m, n), F32),
        compiler_params=_cparams(1),
        name="matmul_hi",
    )(a, w)


def _log_sigmoid(x):
    return jnp.minimum(x, 0.0) - jnp.log(1.0 + jnp.exp(-jnp.abs(x)))


def _mlstm_kernel(q_ref, k_ref, v_ref, gc_ref, gr_ref, bc_ref, br_ref, h_ref,
                  c_ref, n_ref, m_ref):
    d = pl.program_id(0)
    hp = pl.program_id(2)
    c = pl.program_id(3)
    L = CHUNK

    @pl.when(c == 0)
    def _():
        c_ref[...] = jnp.zeros_like(c_ref)
        n_ref[...] = jnp.zeros_like(n_ref)
        m_ref[...] = jnp.zeros_like(m_ref)

    gcol = gc_ref[...] + bc_ref[...]
    grow = gr_ref[...] + br_ref[...]
    lane = lax.broadcasted_iota(I32, gcol.shape, 1)
    sub = lax.broadcasted_iota(I32, grow.shape, 0)
    r = lax.broadcasted_iota(I32, (L, L), 0)
    cc = lax.broadcasted_iota(I32, (L, L), 1)
    sign = 1 - 2 * d
    mask = (cc - r) * sign <= 0
    mask_t = (r - cc) * sign <= 0

    for hh in range(MLSTM_HEADS_PER_STEP):
        cols = slice(hh * HEAD_DIM, (hh + 1) * HEAD_DIM)
        q = q_ref[:, cols] * (HEAD_DIM ** -0.5)
        k = k_ref[:, cols]
        v = v_ref[:, cols]
        li_idx = 2 * d * N_HEADS + hp * MLSTM_HEADS_PER_STEP + hh
        lf_idx = li_idx + N_HEADS
        li_col = jnp.sum(jnp.where(lane == li_idx, gcol, 0.0), axis=1, keepdims=True)
        gf_col = jnp.sum(jnp.where(lane == lf_idx, gcol, 0.0), axis=1, keepdims=True)
        li_row = jnp.sum(jnp.where(sub == li_idx, grow, 0.0), axis=0, keepdims=True)
        gf_row = jnp.sum(jnp.where(sub == lf_idx, grow, 0.0), axis=0, keepdims=True)
        lf_col = _log_sigmoid(gf_col)
        lf_row = _log_sigmoid(gf_row)
        b_col = jnp.sum(jnp.where(mask, lf_row, 0.0), axis=1, keepdims=True)
        b_row = jnp.sum(jnp.where(mask_t, lf_col, 0.0), axis=0, keepdims=True)
        logw = jnp.where(mask, b_col - b_row + li_row, -jnp.inf)

        m_prev = m_ref[hh, :, 0:1]
        inter = b_col + m_prev
        m_t = jnp.maximum(inter, jnp.max(logw, axis=1, keepdims=True))
        w = jnp.exp(logw - m_t)
        a_inter = jnp.exp(inter - m_t)
        qb = q.astype(BF16)
        vb = v.astype(BF16)
        s = lax.dot_general(qb, k.astype(BF16), (((1,), (1,)), ((), ())),
                            preferred_element_type=F32) * w
        cm = c_ref[hh]
        nv = n_ref[hh]
        num = (jnp.dot(s.astype(BF16), vb, preferred_element_type=F32)
               + a_inter * jnp.dot(qb, cm.astype(BF16), preferred_element_type=F32))
        den = (jnp.sum(s, axis=1, keepdims=True)
               + a_inter * jnp.sum(q * nv, axis=1, keepdims=True))
        h_ref[:, cols] = num / jnp.maximum(jnp.abs(den), jnp.exp(-m_t))

        b_last = jnp.sum(lf_row, axis=1, keepdims=True)
        logu = b_last - b_col + li_col
        m_new = jnp.maximum(b_last + m_prev, jnp.max(logu, axis=0, keepdims=True))
        u = jnp.exp(logu - m_new)
        decay = jnp.exp(b_last + m_prev - m_new)
        uk = u * k
        c_ref[hh] = decay * cm + lax.dot_general(
            uk.astype(BF16), vb, (((0,), (0,)), ((), ())), preferred_element_type=F32)
        n_ref[hh] = decay * nv + jnp.sum(uk, axis=0, keepdims=True)
        m_ref[hh] = jnp.broadcast_to(m_new, (1, LANES))


def mlstm(z, gates, gates_t, bias_row, bias_col, base_tok, batch, seq):
    nc = seq // CHUNK
    base = base_tok // CHUNK
    hps = MLSTM_HEADS_PER_STEP
    ngrp = N_HEADS // hps

    def blk(dd, b, c):
        return base + b * nc + c + dd * (nc - 1 - 2 * c)

    def sec(which):
        return pl.BlockSpec((CHUNK, hps * HEAD_DIM),
                            lambda dd, b, h, c: (blk(dd, b, c), which * ngrp + h))

    return pl.pallas_call(
        _mlstm_kernel,
        grid=(2, batch, ngrp, nc),
        in_specs=[sec(0), sec(1), sec(2),
                  pl.BlockSpec((CHUNK, LANES), lambda dd, b, h, c: (blk(dd, b, c), 0)),
                  pl.BlockSpec((N_GATES, CHUNK), lambda dd, b, h, c: (0, blk(dd, b, c))),
                  pl.BlockSpec((1, LANES), lambda dd, b, h, c: (0, 0)),
                  pl.BlockSpec((N_GATES, 1), lambda dd, b, h, c: (0, 0))],
        out_specs=pl.BlockSpec((None, CHUNK, hps * HEAD_DIM),
                               lambda dd, b, h, c: (dd, blk(dd, b, c) - base, h)),
        out_shape=jax.ShapeDtypeStruct((2, batch * seq, D_A), F32),
        scratch_shapes=[pltpu.VMEM((hps, HEAD_DIM, HEAD_DIM), F32),
                        pltpu.VMEM((hps, 1, HEAD_DIM), F32),
                        pltpu.VMEM((hps, 1, LANES), F32)],
        compiler_params=_cparams(4),
        name="mlstm",
    )(z, z, z, gates, gates_t, bias_row, bias_col)


def _pool_kernel(prev_ref, mid_ref, next_ref, pw_ref, sc_ref, o_ref, *, seq):
    i = pl.program_id(1)
    g = pl.program_id(2)
    half = jnp.left_shift(1, g)
    T = POOL_TILE
    p = i * T + lax.broadcasted_iota(I32, (T, T), 0)
    lo = jnp.maximum(p - half, 0)
    hi = jnp.minimum(p + half, seq)
    acc = jnp.zeros((T, POOL_GROUP_DIM), F32)
    for off, ref in ((-T, prev_ref), (0, mid_ref), (T, next_ref)):
        pj = i * T + off + lax.broadcasted_iota(I32, (T, T), 1)
        band = jnp.where((pj >= lo) & (pj < hi), 1.0, 0.0).astype(BF16)
        u = ref[...]
        u_hi = u.astype(BF16)
        u_lo = (u - u_hi.astype(F32)).astype(BF16)
        acc = acc + (jnp.dot(band, u_hi, preferred_element_type=F32)
                     + jnp.dot(band, u_lo, preferred_element_type=F32))
    cnt = (hi - lo)[:, 0:1].astype(F32)
    pooled = acc / cnt - mid_ref[...]
    y = jnp.dot(pooled.astype(BF16), pw_ref[...], preferred_element_type=F32)
    o_ref[...] = y * sc_ref[...]


def pool(z, pool_w, pool_scale, col0, base_tok, batch, seq):
    nt = seq // POOL_TILE
    base = base_tok // POOL_TILE
    cb = col0 // POOL_GROUP_DIM

    def spec(shift):
        return pl.BlockSpec(
            (POOL_TILE, POOL_GROUP_DIM),
            lambda b, i, g: (base + b * nt + jnp.clip(i + shift, 0, nt - 1), cb + g))

    return pl.pallas_call(
        functools.partial(_pool_kernel, seq=seq),
        grid=(batch, nt, N_POOL_GROUPS),
        in_specs=[spec(-1), spec(0), spec(1),
                  pl.BlockSpec((None, POOL_GROUP_DIM, POOL_GROUP_DIM), lambda b, i, g: (g, 0, 0)),
                  pl.BlockSpec((1, POOL_GROUP_DIM), lambda b, i, g: (0, g))],
        out_specs=pl.BlockSpec((POOL_TILE, POOL_GROUP_DIM), lambda b, i, g: (b * nt + i, g)),
        out_shape=jax.ShapeDtypeStruct((batch * seq, D_B), F32),
        compiler_params=_cparams(3),
        name="pool",
    )(z, z, z, pool_w, pool_scale.reshape(1, D_B))


def _outproj_kernel(hfa_ref, hba_ref, hfb_ref, hbb_ref, o_ref, yba_ref, ybb_ref, wa_ref, wb_ref,
                    x_ref, g_ref, out_ref, xn_ref, *, na):
    first = pl.program_id(0) < na
    hsum = jnp.where(first, hfa_ref[...] + hba_ref[...], hfb_ref[...] + hbb_ref[...])
    yb = jnp.where(first, yba_ref[...], ybb_ref[...])
    ha = hsum * jax.nn.sigmoid(o_ref[...])
    acc = jnp.dot(ha.astype(BF16), wa_ref[...], preferred_element_type=F32)
    acc = acc + jnp.dot(yb.astype(BF16), wb_ref[...], preferred_element_type=F32)
    x = acc + x_ref[...]
    out_ref[...] = x
    xn_ref[...] = _rms(x, g_ref[...])


def _part_block(shape, na, second, lead=None):
    row = (lambda i: jnp.maximum(i - na, 0)) if second else (lambda i: jnp.minimum(i, na - 1))
    if lead is None:
        return pl.BlockSpec(shape, lambda i: (row(i), 0))
    return pl.BlockSpec((None,) + shape, lambda i: (lead, row(i), 0))


def outproj(hs, z, ybs, w_out, x, g, tm=256):
    t, d = x.shape
    na = hs[0].shape[1] // tm
    ocb = 3 * D_A // D_A
    row = pl.BlockSpec((tm, d), lambda i: (i, 0))
    return pl.pallas_call(
        functools.partial(_outproj_kernel, na=na),
        grid=(t // tm,),
        in_specs=[_part_block((tm, D_A), na, False, 0), _part_block((tm, D_A), na, False, 1),
                  _part_block((tm, D_A), na, True, 0), _part_block((tm, D_A), na, True, 1),
                  pl.BlockSpec((tm, D_A), lambda i: (i, ocb)),
                  _part_block((tm, D_B), na, False), _part_block((tm, D_B), na, True),
                  pl.BlockSpec((D_A, d), lambda i: (0, 0)),
                  pl.BlockSpec((D_B, d), lambda i: (D_A // D_B, 0)),
                  row,
                  pl.BlockSpec((1, d), lambda i: (0, 0))],
        out_specs=[row, row],
        out_shape=[jax.ShapeDtypeStruct((t, d), F32)] * 2,
        compiler_params=_cparams(1),
        name="outproj",
    )(hs[0], hs[0], hs[1], hs[1], z, ybs[0], ybs[1], w_out, w_out, x, g.reshape(1, d))


def _dft_rows_kernel(c_ref, s_ref, *, n, stride, scale):
    r = lax.broadcasted_iota(I32, c_ref.shape, 0) * stride
    c = lax.broadcasted_iota(I32, c_ref.shape, 1)
    ang = ((r * c) & (n - 1)).astype(F32) * (2.0 * math.pi / n)
    c_ref[...] = jnp.cos(ang) * scale
    s_ref[...] = jnp.sin(ang) * scale


def _dft_rows(n, rows, stride, scale):
    return pl.pallas_call(
        functools.partial(_dft_rows_kernel, n=n, stride=stride, scale=scale),
        out_shape=[jax.ShapeDtypeStruct((rows, n), F32)] * 2,
        compiler_params=_cparams(0),
        name="dft_rows",
    )()


def _dft_combine_kernel(ch_ref, sh_ref, cl_ref, sl_ref, c_ref, s_ref):
    ch, sh = ch_ref[...], sh_ref[...]
    cl, sl = cl_ref[...], sl_ref[...]
    c_ref[...] = (ch * cl - sh * sl).astype(BF16)
    s_ref[...] = (sh * cl + ch * sl).astype(BF16)


def dft_tables(n):
    assert n & (n - 1) == 0
    r = DFT_FINE_ROWS
    nh = n // r
    cl, sl = _dft_rows(n, r, 1, n ** -0.5)
    ch, sh = _dft_rows(n, nh, r, 1.0)
    coarse = pl.BlockSpec((None, 1, n), lambda i: (i, 0, 0))
    fine = pl.BlockSpec((r, n), lambda i: (0, 0))
    blk = pl.BlockSpec((r, n), lambda i: (i, 0))
    return pl.pallas_call(
        _dft_combine_kernel,
        grid=(nh,),
        in_specs=[coarse, coarse, fine, fine],
        out_specs=[blk, blk],
        out_shape=[jax.ShapeDtypeStruct((n, n), BF16)] * 2,
        compiler_params=_cparams(1),
        name="dft_tables",
    )(ch.reshape(nh, 1, n), sh.reshape(nh, 1, n), cl, sl)


def _fourier_ch_kernel(x_ref, cd_ref, sd_ref, yc_ref, ys_ref):
    xb = x_ref[...].astype(BF16)
    yc_ref[...] = jnp.dot(xb, cd_ref[...], preferred_element_type=F32).astype(BF16)
    ys_ref[...] = jnp.dot(xb, sd_ref[...], preferred_element_type=F32).astype(BF16)


def fourier_channels(xn, cd, sd, tm=512):
    t = xn.shape[0]
    tm = min(tm, t)
    gd = FOURIER_GROUP_DIM
    tile = pl.BlockSpec((tm, gd), lambda i, g: (i, g))
    tab = pl.BlockSpec((gd, gd), lambda i, g: (0, 0))
    return pl.pallas_call(
        _fourier_ch_kernel,
        grid=(t // tm, N_FOURIER_GROUPS),
        in_specs=[tile, tab, tab],
        out_specs=[tile, tile],
        out_shape=[jax.ShapeDtypeStruct((t, D_MODEL), BF16)] * 2,
        compiler_params=_cparams(2),
        name="fourier_channels",
    )(xn, cd, sd)


def _fourier_seq_kernel(cs_ref, ss_ref, yc_ref, ys_ref, o_ref, acc_ref):
    kk = pl.program_id(3)

    @pl.when(kk == 0)
    def _():
        acc_ref[...] = jnp.zeros_like(acc_ref)

    acc_ref[...] += (jnp.dot(cs_ref[...], yc_ref[...], preferred_element_type=F32)
                     - jnp.dot(ss_ref[...], ys_ref[...], preferred_element_type=F32))

    @pl.when(kk == pl.num_programs(3) - 1)
    def _():
        o_ref[...] = acc_ref[...].astype(o_ref.dtype)


def fourier_seq(cs, ss, yc, ys, base_tok, batch, seq, tm=1024, tn=1024, tk=512):
    nk = seq // tk
    base = base_tok // tk
    a_spec = pl.BlockSpec((tm, tk), lambda b, i, j, k: (i, k))
    y_spec = pl.BlockSpec((tk, tn), lambda b, i, j, k: (base + b * nk + k, j))
    return pl.pallas_call(
        _fourier_seq_kernel,
        grid=(batch, seq // tm, D_MODEL // tn, nk),
        in_specs=[a_spec, a_spec, y_spec, y_spec],
        out_specs=pl.BlockSpec((tm, tn), lambda b, i, j, k: (b * (seq // tm) + i, j)),
        out_shape=jax.ShapeDtypeStruct((batch * seq, D_MODEL), BF16),
        scratch_shapes=[pltpu.VMEM((tm, tn), F32)],
        compiler_params=_cparams(4),
        name="fourier_seq",
    )(cs, ss, yc, ys)


def _top16_rows(s, row_key):
    n = s.shape[1]
    out_iota = lax.broadcasted_iota(I32, (PEER_TOPK, n), 0)
    vals = jnp.zeros((PEER_TOPK, n), F32)
    keys = jnp.zeros((PEER_TOPK, n), F32)
    for rnk in range(PEER_TOPK):
        mx = jnp.max(s, axis=0, keepdims=True)
        kmin = jnp.min(jnp.where(s == mx, row_key, KEY_SENTINEL), axis=0, keepdims=True)
        vals = jnp.where(out_iota == rnk, mx, vals)
        keys = jnp.where(out_iota == rnk, kmin, keys)
        s = jnp.where(row_key == kmin, -jnp.inf, s)
    return vals, keys


def _peer_topk_kernel(q_ref, k1_ref, k2_ref, e_ref, g_ref):
    tt = q_ref.shape[0]
    k1 = k1_ref[...]
    k2 = k2_ref[...]
    fiota = lambda shape: lax.broadcasted_iota(I32, shape, 0).astype(F32)
    key_iota = fiota((PEER_NKEYS, tt))
    nt = (((1,), (1,)), ((), ()))
    blocks = [(a, 16 if a == 0 else 8) for a in range(8)]
    for h in range(PEER_HEADS):
        q1 = q_ref[:, h * 2 * PEER_HALF: h * 2 * PEER_HALF + PEER_HALF]
        q2 = q_ref[:, h * 2 * PEER_HALF + PEER_HALF: (h + 1) * 2 * PEER_HALF]
        s1 = lax.dot_general(k1, q1, nt, precision=HI, preferred_element_type=F32)
        s2 = lax.dot_general(k2, q2, nt, precision=HI, preferred_element_type=F32)
        v1, i1 = _top16_rows(s1, key_iota)
        v2, i2 = _top16_rows(s2, key_iota)
        cands, ids, orders = [], [], []
        for a, rows in blocks:
            bi = fiota((rows, tt))
            ok = (bi + 1.0) * (a + 1) <= PEER_TOPK
            cands.append(jnp.where(ok, v1[a:a + 1, :] + v2[0:rows, :], -jnp.inf))
            ids.append(i1[a:a + 1, :] * PEER_NKEYS + i2[0:rows, :])
            orders.append(a * PEER_TOPK + bi)
        ai = fiota((8, tt)) + 8.0
        cands.append(v1[8:16, :] + v2[0:1, :])
        ids.append(i1[8:16, :] * PEER_NKEYS + i2[0:1, :])
        orders.append(ai * PEER_TOPK)
        cand = jnp.concatenate(cands, axis=0)
        cid = jnp.concatenate(ids, axis=0)
        order = jnp.concatenate(orders, axis=0)
        ts, tpos = _top16_rows(cand, order)
        experts = jnp.zeros((PEER_TOPK, tt), F32)
        out_iota = lax.broadcasted_iota(I32, (PEER_TOPK, tt), 0)
        for rnk in range(PEER_TOPK):
            sel = jnp.max(jnp.where(order == tpos[rnk:rnk + 1, :], cid, -1.0), axis=0, keepdims=True)
            experts = jnp.where(out_iota == rnk, sel, experts)
        ex = jnp.exp(ts - ts[0:1, :])
        gate = ex / jnp.sum(ex, axis=0, keepdims=True)
        e_ref[h * PEER_TOPK:(h + 1) * PEER_TOPK, :] = experts.astype(I32)
        g_ref[h * PEER_TOPK:(h + 1) * PEER_TOPK, :] = gate


def peer_topk(q, k1, k2, tt=256):
    t = q.shape[0]
    tt = min(tt, t)
    ne = PEER_HEADS * PEER_TOPK
    keys = pl.BlockSpec((PEER_NKEYS, PEER_HALF), lambda i: (0, 0))
    out = pl.BlockSpec((ne, tt), lambda i: (0, i))
    return pl.pallas_call(
        _peer_topk_kernel,
        grid=(t // tt,),
        in_specs=[pl.BlockSpec((tt, q.shape[1]), lambda i: (i, 0)), keys, keys],
        out_specs=[out, out],
        out_shape=[jax.ShapeDtypeStruct((ne, t), I32), jax.ShapeDtypeStruct((ne, t), F32)],
        compiler_params=_cparams(1),
        name="peer_topk",
    )(q, k1, k2)


def _gelu_exact(x):
    return 0.5 * x * (1.0 + lax.erf(x * (2.0 ** -0.5)))


def _peer_dense_kernel(i1a_ref, i2a_ref, ga_ref, i1b_ref, i2b_ref, gb_ref, xn_ref, ut_ref, v_ref,
                       out_ref, gate_ref, stage_ref, *, tm, eb):
    i = pl.program_id(0)
    j = pl.program_id(1)
    nk = PEER_NKEYS
    grp = PEER_BUILD_GROUP
    par = i % 2
    key = lax.broadcasted_iota(I32, (nk, nk), 0)
    nt = (((1,), (1,)), ((), ()))

    def build_group(i1_ref, i2_ref, g_ref, t0, slot):
        for k in range(grp):
            t = t0 + k
            a_t = jnp.where(key == i1_ref[pl.ds(t, 1), :], 1.0, 0.0).astype(BF16)
            b_t = jnp.where(key == i2_ref[pl.ds(t, 1), :], g_ref[pl.ds(t, 1), :],
                            0.0).astype(BF16)
            g_t = lax.dot_general(a_t, b_t, nt, preferred_element_type=F32)
            stage_ref[pl.ds(k, nk, stride=grp), :] = g_t
        for r in range(nk):
            gate_ref[slot, pl.ds(pl.multiple_of(r * tm + t0, grp), grp), :] = (
                stage_ref[r * grp:(r + 1) * grp, :].astype(BF16))

    @pl.when(j == 0)
    def _():
        out_ref[...] = jnp.zeros_like(out_ref)

    @pl.when((i == 0) & (j == 0))
    def _():
        def first(tq, carry):
            build_group(i1a_ref, i2a_ref, ga_ref, pl.multiple_of(tq * grp, grp), 0)
            return carry

        lax.fori_loop(0, tm // grp, first, 0)

    build_group(i1b_ref, i2b_ref, gb_ref, pl.multiple_of((j * grp) % tm, grp), 1 - par)

    act = jnp.dot(xn_ref[...], ut_ref[...], preferred_element_type=F32)
    nb = eb // nk
    z = [(_gelu_exact(act[:, b * nk:(b + 1) * nk])
          * gate_ref[par, pl.ds(pl.multiple_of((j * nb + b) * tm, tm), tm), :].astype(F32)
          ).astype(BF16)
         for b in range(nb)]
    out_ref[...] += jnp.dot(jnp.concatenate(z, axis=1), v_ref[...], preferred_element_type=F32)


def peer_dense(xn, expert_t, gate_t, u_tab, v_tab, tm=512, eb=512):
    t, d = xn.shape
    tm = min(tm, t)
    nti = t // tm
    nsteps = N_EXPERTS // eb
    assert tm % PEER_BUILD_GROUP == 0 and tm <= nsteps * PEER_BUILD_GROUP
    ns = expert_t.shape[0]
    i1 = (expert_t // PEER_NKEYS).T
    i2 = (expert_t % PEER_NKEYS).T
    g = gate_t.T
    slot_a = pl.BlockSpec((tm, ns), lambda i, j: (0, 0))
    slot_b = pl.BlockSpec((tm, ns), lambda i, j: (jnp.minimum(i + 1, nti - 1), 0))
    row = pl.BlockSpec((tm, d), lambda i, j: (i, 0), pipeline_mode=pl.Buffered(1))
    return pl.pallas_call(
        functools.partial(_peer_dense_kernel, tm=tm, eb=eb),
        grid=(nti, nsteps),
        in_specs=[slot_a, slot_a, slot_a, slot_b, slot_b, slot_b, row,
                  pl.BlockSpec((d, eb), lambda i, j: (0, j)),
                  pl.BlockSpec((eb, d), lambda i, j: (j, 0))],
        out_specs=row,
        out_shape=jax.ShapeDtypeStruct((t, d), F32),
        scratch_shapes=[pltpu.VMEM((2, PEER_NKEYS * tm, PEER_NKEYS), BF16),
                        pltpu.VMEM((PEER_NKEYS * PEER_BUILD_GROUP, PEER_NKEYS), F32)],
        compiler_params=_cparams(2),
        name="peer_dense",
    )(i1, i2, g, i1, i2, g, xn.astype(BF16), u_tab.astype(BF16).T, v_tab.astype(BF16))


def peer_layer(xn, wq, k1, k2, u_tab, v_tab):
    q = matmul(xn, wq.astype(BF16), tn=1024, name="peer_query")
    expert_t, gate_t = peer_topk(q, k1, k2)
    return peer_dense(xn, expert_t, gate_t, u_tab, v_tab)


def _trunk(x, xn, parts, even_mix_norm, even_w_in, even_gate_bias, even_pool_w, even_pool_scale,
           even_w_out, odd_mix_norm, odd_fourier_w, ffn_norm, peer_wq, peer_k1, peer_k2,
           peer_u, peer_v, final_norm):
    n_main = 4 * D_A + D_B
    w_in = even_w_in[0]
    z = matmul(xn, w_in[:, :n_main].astype(BF16), tn=1024, name="in_proj")
    w_gate = jnp.pad(w_in[:, n_main:], ((0, 0), (0, LANES - N_GATES)))
    gates = matmul_hi(xn, w_gate)
    gates_t = gates[:, :N_GATES].T
    bias = even_gate_bias[0].astype(F32)
    bias_row = jnp.pad(bias, (0, LANES - N_GATES)).reshape(1, LANES)
    bias_col = bias.reshape(N_GATES, 1)
    pw = even_pool_w[0].astype(BF16)
    hs, ybs = [], []
    for base, batch, seq in parts:
        hs.append(mlstm(z, gates, gates_t, bias_row, bias_col, base, batch, seq))
        ybs.append(pool(z, pw, even_pool_scale[0], 4 * D_A, base, batch, seq))
    x, xn = outproj(hs, z, ybs, even_w_out[0].astype(BF16), x, ffn_norm[0])
    y = peer_layer(xn, peer_wq[0], peer_k1[0], peer_k2[0], peer_u[0], peer_v[0])

    x, xn = add_rmsnorm(x, y, odd_mix_norm[0])
    cd, sd = dft_tables(FOURIER_GROUP_DIM)
    yc, ys = fourier_channels(xn, cd, sd)
    fs = []
    for base, batch, seq in parts:
        cs, ss = dft_tables(seq)
        fs.append(fourier_seq(cs, ss, yc, ys, base, batch, seq,
                              tm=min(1024, seq), tk=min(512, seq)))
    x, xn = matmul_res_norm(fs, odd_fourier_w[0].astype(BF16), x, ffn_norm[1], name="fourier_out")
    y = peer_layer(xn, peer_wq[1], peer_k1[1], peer_k2[1], peer_u[1], peer_v[1])
    return [add_rmsnorm_part(x, y, final_norm, base, batch * seq) for base, batch, seq in parts]


def kernel(x_prompt, x_sample, even_mix_norm, even_w_in, even_gate_bias, even_pool_w,
           even_pool_scale, even_w_out, odd_mix_norm, odd_fourier_w, ffn_norm, peer_wq,
           peer_k1, peer_k2, peer_u, peer_v, final_norm):
    bp, sp, d = x_prompt.shape
    bs, ss, _ = x_sample.shape
    tp = bp * sp
    x, xn = concat_rmsnorm(x_prompt.reshape(tp, d), x_sample.reshape(bs * ss, d), even_mix_norm[0])
    parts = ((0, bp, sp), (tp, bs, ss))
    yp, ys = _trunk(x, xn, parts, even_mix_norm, even_w_in, even_gate_bias, even_pool_w,
                    even_pool_scale, even_w_out, odd_mix_norm, odd_fourier_w, ffn_norm,
                    peer_wq, peer_k1, peer_k2, peer_u, peer_v, final_norm)
    return yp.reshape(bp, sp, d), ys.reshape(bs, ss, d)
```

```python
import functools
import math

import jax
import jax.numpy as jnp
from jax import lax
from jax.experimental import pallas as pl
from jax.experimental.pallas import tpu as pltpu

F32 = jnp.float32
BF16 = jnp.bfloat16
I32 = jnp.int32
HI = lax.Precision.HIGHEST

D_MODEL = 2048
EPS = 1e-6
N_HEADS = 4
HEAD_DIM = 256
D_A = N_HEADS * HEAD_DIM
D_B = D_MODEL - D_A
N_GATES = 4 * N_HEADS
CHUNK = 256
MLSTM_HEADS_PER_STEP = 2
N_POOL_GROUPS = 4
POOL_TILE = 256
POOL_GROUP_DIM = D_B // N_POOL_GROUPS
N_FOURIER_GROUPS = 4
DFT_FINE_ROWS = 64
FOURIER_GROUP_DIM = D_MODEL // N_FOURIER_GROUPS
PEER_HEADS = 8
PEER_NKEYS = 128
PEER_HALF = 128
PEER_TOPK = 16
N_EXPERTS = PEER_NKEYS * PEER_NKEYS
KEY_SENTINEL = 1e9

LANES = 128
PEER_BUILD_GROUP = 16
VMEM_LIMIT = 56 * 1024 * 1024


def _cparams(n_axes):
    return pltpu.CompilerParams(dimension_semantics=("arbitrary",) * n_axes,
                                vmem_limit_bytes=VMEM_LIMIT)


def _rms(x, g):
    ms = jnp.mean(x * x, axis=-1, keepdims=True)
    return (x * lax.rsqrt(ms + EPS)) * g


def _norm_kernel(a_ref, g_ref, xn_ref):
    xn_ref[...] = _rms(a_ref[...], g_ref[...])


def _addnorm_kernel(a_ref, b_ref, g_ref, x_ref, xn_ref):
    x = a_ref[...] + b_ref[...]
    x_ref[...] = x
    xn_ref[...] = _rms(x, g_ref[...])


def _concat_norm_kernel(a_ref, b_ref, g_ref, x_ref, xn_ref, *, na):
    i = pl.program_id(0)

    @pl.when(i < na)
    def _():
        x_ref[...] = a_ref[...]

    @pl.when(i >= na)
    def _():
        x_ref[...] = b_ref[...]

    xn_ref[...] = _rms(x_ref[...], g_ref[...])


def concat_rmsnorm(a, b, g, tm=256):
    d = a.shape[1]
    na, nb = a.shape[0] // tm, b.shape[0] // tm
    row = pl.BlockSpec((tm, d), lambda i: (i, 0))
    return pl.pallas_call(
        functools.partial(_concat_norm_kernel, na=na),
        grid=(na + nb,),
        in_specs=[pl.BlockSpec((tm, d), lambda i: (jnp.minimum(i, na - 1), 0)),
                  pl.BlockSpec((tm, d), lambda i: (jnp.maximum(i - na, 0), 0)),
                  pl.BlockSpec((1, d), lambda i: (0, 0))],
        out_specs=[row, row],
        out_shape=[jax.ShapeDtypeStruct((a.shape[0] + b.shape[0], d), F32)] * 2,
        compiler_params=_cparams(1),
        name="concat_rmsnorm",
    )(a, b, g.reshape(1, d))


def rmsnorm(a, g, tm=256):
    t, d = a.shape
    tm = min(tm, t)
    row = pl.BlockSpec((tm, d), lambda i: (i, 0))
    return pl.pallas_call(
        _norm_kernel,
        grid=(t // tm,),
        in_specs=[row, pl.BlockSpec((1, d), lambda i: (0, 0))],
        out_specs=row,
        out_shape=jax.ShapeDtypeStruct((t, d), F32),
        compiler_params=_cparams(1),
        name="rmsnorm",
    )(a, g.reshape(1, d))


def add_rmsnorm(a, b, g, tm=256):
    t, d = a.shape
    tm = min(tm, t)
    row = pl.BlockSpec((tm, d), lambda i: (i, 0))
    return pl.pallas_call(
        _addnorm_kernel,
        grid=(t // tm,),
        in_specs=[row, row, pl.BlockSpec((1, d), lambda i: (0, 0))],
        out_specs=[row, row],
        out_shape=[jax.ShapeDtypeStruct((t, d), F32)] * 2,
        compiler_params=_cparams(1),
        name="add_rmsnorm",
    )(a, b, g.reshape(1, d))


def _addnorm_out_kernel(a_ref, b_ref, g_ref, xn_ref):
    xn_ref[...] = _rms(a_ref[...] + b_ref[...], g_ref[...])


def add_rmsnorm_part(a, b, g, base_tok, rows, tm=256):
    d = a.shape[1]
    tm = min(tm, rows)
    src = pl.BlockSpec((tm, d), lambda i: (base_tok // tm + i, 0))
    return pl.pallas_call(
        _addnorm_out_kernel,
        grid=(rows // tm,),
        in_specs=[src, src, pl.BlockSpec((1, d), lambda i: (0, 0))],
        out_specs=pl.BlockSpec((tm, d), lambda i: (i, 0)),
        out_shape=jax.ShapeDtypeStruct((rows, d), F32),
        compiler_params=_cparams(1),
        name="final_norm",
    )(a, b, g.reshape(1, d))


def _mm_kernel(a_ref, w_ref, o_ref):
    o_ref[...] = jnp.dot(a_ref[...].astype(BF16), w_ref[...],
                         preferred_element_type=F32).astype(o_ref.dtype)


def _mm_res_kernel(a_ref, w_ref, r_ref, o_ref):
    acc = jnp.dot(a_ref[...].astype(BF16), w_ref[...], preferred_element_type=F32)
    o_ref[...] = (acc + r_ref[...]).astype(o_ref.dtype)


def _mm_hi_kernel(a_ref, w_ref, o_ref):
    o_ref[...] = jnp.dot(a_ref[...], w_ref[...], precision=HI,
                         preferred_element_type=F32)


def _mm_res_norm_kernel(aa_ref, ab_ref, w_ref, r_ref, g_ref, o_ref, xn_ref, *, na):
    a = jnp.where(pl.program_id(0) < na, aa_ref[...], ab_ref[...])
    x = jnp.dot(a.astype(BF16), w_ref[...], preferred_element_type=F32) + r_ref[...]
    o_ref[...] = x
    xn_ref[...] = _rms(x, g_ref[...])


def matmul_res_norm(a_parts, w, res, g, tm=256, name="matmul_res_norm"):
    k = a_parts[0].shape[1]
    m, n = res.shape
    na = a_parts[0].shape[0] // tm
    row = pl.BlockSpec((tm, n), lambda i: (i, 0))
    return pl.pallas_call(
        functools.partial(_mm_res_norm_kernel, na=na),
        grid=(m // tm,),
        in_specs=[pl.BlockSpec((tm, k), lambda i: (jnp.minimum(i, na - 1), 0)),
                  pl.BlockSpec((tm, k), lambda i: (jnp.maximum(i - na, 0), 0)),
                  pl.BlockSpec((k, n), lambda i: (0, 0)),
                  row,
                  pl.BlockSpec((1, n), lambda i: (0, 0))],
        out_specs=[row, row],
        out_shape=[jax.ShapeDtypeStruct((m, n), F32)] * 2,
        compiler_params=_cparams(1),
        name=name,
    )(a_parts[0], a_parts[1], w, res, g.reshape(1, n))


def matmul(a, w, res=None, tm=512, tn=512, out_dtype=F32, name="matmul"):
    m, k = a.shape
    n = w.shape[1]
    tm = min(tm, m)
    tn = min(tn, n)
    in_specs = [pl.BlockSpec((tm, k), lambda i, j: (i, 0)),
                pl.BlockSpec((k, tn), lambda i, j: (0, j))]
    args = [a, w]
    body = _mm_kernel
    if res is not None:
        in_specs.append(pl.BlockSpec((tm, tn), lambda i, j: (i, j)))
        args.append(res)
        body = _mm_res_kernel
    return pl.pallas_call(
        body,
        grid=(m // tm, n // tn),
        in_specs=in_specs,
        out_specs=pl.BlockSpec((tm, tn), lambda i, j: (i, j)),
        out_shape=jax.ShapeDtypeStruct((m, n), out_dtype),
        compiler_params=_cparams(2),
        name=name,
    )(*args)


def matmul_hi(a, w, tm=512):
    m, k = a.shape
    n = w.shape[1]
    tm = min(tm, m)
    return pl.pallas_call(
        _mm_hi_kernel,
        grid=(m // tm,),
        in_specs=[pl.BlockSpec((tm, k), lambda i: (i, 0)),
                  pl.BlockSpec((k, n), lambda i: (0, 0))],
        out_specs=pl.BlockSpec((tm, n), lambda i: (i, 0)),
        out_shape=jax.ShapeDtypeStruct((m, n), F32),
        compiler_params=_cparams(1),
        name="matmul_hi",
    )(a, w)


def _log_sigmoid(x):
    return jnp.minimum(x, 0.0) - jnp.log(1.0 + jnp.exp(-jnp.abs(x)))


def _mlstm_kernel(q_ref, k_ref, v_ref, gc_ref, gr_ref, bc_ref, br_ref, h_ref,
                  c_ref, n_ref, m_ref):
    d = pl.program_id(0)
    hp = pl.program_id(2)
    c = pl.program_id(3)
    L = CHUNK

    @pl.when(c == 0)
    def _():
        c_ref[...] = jnp.zeros_like(c_ref)
        n_ref[...] = jnp.zeros_like(n_ref)
        m_ref[...] = jnp.zeros_like(m_ref)

    gcol = gc_ref[...] + bc_ref[...]
    grow = gr_ref[...] + br_ref[...]
    lane = lax.broadcasted_iota(I32, gcol.shape, 1)
    sub = lax.broadcasted_iota(I32, grow.shape, 0)
    r = lax.broadcasted_iota(I32, (L, L), 0)
    cc = lax.broadcasted_iota(I32, (L, L), 1)
    sign = 1 - 2 * d
    mask = (cc - r) * sign <= 0
    mask_t = (r - cc) * sign <= 0

    for hh in range(MLSTM_HEADS_PER_STEP):
        cols = slice(hh * HEAD_DIM, (hh + 1) * HEAD_DIM)
        q = q_ref[:, cols] * (HEAD_DIM ** -0.5)
        k = k_ref[:, cols]
        v = v_ref[:, cols]
        li_idx = 2 * d * N_HEADS + hp * MLSTM_HEADS_PER_STEP + hh
        lf_idx = li_idx + N_HEADS
        li_col = jnp.sum(jnp.where(lane == li_idx, gcol, 0.0), axis=1, keepdims=True)
        gf_col = jnp.sum(jnp.where(lane == lf_idx, gcol, 0.0), axis=1, keepdims=True)
        li_row = jnp.sum(jnp.where(sub == li_idx, grow, 0.0), axis=0, keepdims=True)
        gf_row = jnp.sum(jnp.where(sub == lf_idx, grow, 0.0), axis=0, keepdims=True)
        lf_col = _log_sigmoid(gf_col)
        lf_row = _log_sigmoid(gf_row)
        b_col = jnp.sum(jnp.where(mask, lf_row, 0.0), axis=1, keepdims=True)
        b_row = jnp.sum(jnp.where(mask_t, lf_col, 0.0), axis=0, keepdims=True)
        logw = jnp.where(mask, b_col - b_row + li_row, -jnp.inf)

        m_prev = m_ref[hh, :, 0:1]
        inter = b_col + m_prev
        m_t = jnp.maximum(inter, jnp.max(logw, axis=1, keepdims=True))
        w = jnp.exp(logw - m_t)
        a_inter = jnp.exp(inter - m_t)
        qb = q.astype(BF16)
        vb = v.astype(BF16)
        s = lax.dot_general(qb, k.astype(BF16), (((1,), (1,)), ((), ())),
                            preferred_element_type=F32) * w
        cm = c_ref[hh]
        nv = n_ref[hh]
        num = (jnp.dot(s.astype(BF16), vb, preferred_element_type=F32)
               + a_inter * jnp.dot(qb, cm.astype(BF16), preferred_element_type=F32))
        den = (jnp.sum(s, axis=1, keepdims=True)
               + a_inter * jnp.sum(q * nv, axis=1, keepdims=True))
        h_ref[:, cols] = num / jnp.maximum(jnp.abs(den), jnp.exp(-m_t))

        b_last = jnp.sum(lf_row, axis=1, keepdims=True)
        logu = b_last - b_col + li_col
        m_new = jnp.maximum(b_last + m_prev, jnp.max(logu, axis=0, keepdims=True))
        u = jnp.exp(logu - m_new)
        decay = jnp.exp(b_last + m_prev - m_new)
        uk = u * k
        c_ref[hh] = decay * cm + lax.dot_general(
            uk.astype(BF16), vb, (((0,), (0,)), ((), ())), preferred_element_type=F32)
        n_ref[hh] = decay * nv + jnp.sum(uk, axis=0, keepdims=True)
        m_ref[hh] = jnp.broadcast_to(m_new, (1, LANES))


def mlstm(z, gates, gates_t, bias_row, bias_col, base_tok, batch, seq):
    nc = seq // CHUNK
    base = base_tok // CHUNK
    hps = MLSTM_HEADS_PER_STEP
    ngrp = N_HEADS // hps

    def blk(dd, b, c):
        return base + b * nc + c + dd * (nc - 1 - 2 * c)

    def sec(which):
        return pl.BlockSpec((CHUNK, hps * HEAD_DIM),
                            lambda dd, b, h, c: (blk(dd, b, c), which * ngrp + h))

    return pl.pallas_call(
        _mlstm_kernel,
        grid=(2, batch, ngrp, nc),
        in_specs=[sec(0), sec(1), sec(2),
                  pl.BlockSpec((CHUNK, LANES), lambda dd, b, h, c: (blk(dd, b, c), 0)),
                  pl.BlockSpec((N_GATES, CHUNK), lambda dd, b, h, c: (0, blk(dd, b, c))),
                  pl.BlockSpec((1, LANES), lambda dd, b, h, c: (0, 0)),
                  pl.BlockSpec((N_GATES, 1), lambda dd, b, h, c: (0, 0))],
        out_specs=pl.BlockSpec((None, CHUNK, hps * HEAD_DIM),
                               lambda dd, b, h, c: (dd, blk(dd, b, c) - base, h)),
        out_shape=jax.ShapeDtypeStruct((2, batch * seq, D_A), F32),
        scratch_shapes=[pltpu.VMEM((hps, HEAD_DIM, HEAD_DIM), F32),
                        pltpu.VMEM((hps, 1, HEAD_DIM), F32),
                        pltpu.VMEM((hps, 1, LANES), F32)],
        compiler_params=_cparams(4),
        name="mlstm",
    )(z, z, z, gates, gates_t, bias_row, bias_col)


def _pool_kernel(prev_ref, mid_ref, next_ref, pw_ref, sc_ref, o_ref, *, seq):
    i = pl.program_id(1)
    g = pl.program_id(2)
    half = jnp.left_shift(1, g)
    T = POOL_TILE
    p = i * T + lax.broadcasted_iota(I32, (T, T), 0)
    lo = jnp.maximum(p - half, 0)
    hi = jnp.minimum(p + half, seq)
    acc = jnp.zeros((T, POOL_GROUP_DIM), F32)
    for off, ref in ((-T, prev_ref), (0, mid_ref), (T, next_ref)):
        pj = i * T + off + lax.broadcasted_iota(I32, (T, T), 1)
        band = jnp.where((pj >= lo) & (pj < hi), 1.0, 0.0).astype(BF16)
        u = ref[...]
        u_hi = u.astype(BF16)
        u_lo = (u - u_hi.astype(F32)).astype(BF16)
        acc = acc + (jnp.dot(band, u_hi, preferred_element_type=F32)
                     + jnp.dot(band, u_lo, preferred_element_type=F32))
    cnt = (hi - lo)[:, 0:1].astype(F32)
    pooled = acc / cnt - mid_ref[...]
    y = jnp.dot(pooled.astype(BF16), pw_ref[...], preferred_element_type=F32)
    o_ref[...] = y * sc_ref[...]


def pool(z, pool_w, pool_scale, col0, base_tok, batch, seq):
    nt = seq // POOL_TILE
    base = base_tok // POOL_TILE
    cb = col0 // POOL_GROUP_DIM

    def spec(shift):
        return pl.BlockSpec(
            (POOL_TILE, POOL_GROUP_DIM),
            lambda b, i, g: (base + b * nt + jnp.clip(i + shift, 0, nt - 1), cb + g))

    return pl.pallas_call(
        functools.partial(_pool_kernel, seq=seq),
        grid=(batch, nt, N_POOL_GROUPS),
        in_specs=[spec(-1), spec(0), spec(1),
                  pl.BlockSpec((None, POOL_GROUP_DIM, POOL_GROUP_DIM), lambda b, i, g: (g, 0, 0)),
                  pl.BlockSpec((1, POOL_GROUP_DIM), lambda b, i, g: (0, g))],
        out_specs=pl.BlockSpec((POOL_TILE, POOL_GROUP_DIM), lambda b, i, g: (b * nt + i, g)),
        out_shape=jax.ShapeDtypeStruct((batch * seq, D_B), F32),
        compiler_params=_cparams(3),
        name="pool",
    )(z, z, z, pool_w, pool_scale.reshape(1, D_B))


def _outproj_kernel(hfa_ref, hba_ref, hfb_ref, hbb_ref, o_ref, yba_ref, ybb_ref, wa_ref, wb_ref,
                    x_ref, g_ref, out_ref, xn_ref, *, na):
    first = pl.program_id(0) < na
    hsum = jnp.where(first, hfa_ref[...] + hba_ref[...], hfb_ref[...] + hbb_ref[...])
    yb = jnp.where(first, yba_ref[...], ybb_ref[...])
    ha = hsum * jax.nn.sigmoid(o_ref[...])
    acc = jnp.dot(ha.astype(BF16), wa_ref[...], preferred_element_type=F32)
    acc = acc + jnp.dot(yb.astype(BF16), wb_ref[...], preferred_element_type=F32)
    x = acc + x_ref[...]
    out_ref[...] = x
    xn_ref[...] = _rms(x, g_ref[...])


def _part_block(shape, na, second, lead=None):
    row = (lambda i: jnp.maximum(i - na, 0)) if second else (lambda i: jnp.minimum(i, na - 1))
    if lead is None:
        return pl.BlockSpec(shape, lambda i: (row(i), 0))
    return pl.BlockSpec((None,) + shape, lambda i: (lead, row(i), 0))


def outproj(hs, z, ybs, w_out, x, g, tm=256):
    t, d = x.shape
    na = hs[0].shape[1] // tm
    ocb = 3 * D_A // D_A
    row = pl.BlockSpec((tm, d), lambda i: (i, 0))
    return pl.pallas_call(
        functools.partial(_outproj_kernel, na=na),
        grid=(t // tm,),
        in_specs=[_part_block((tm, D_A), na, False, 0), _part_block((tm, D_A), na, False, 1),
                  _part_block((tm, D_A), na, True, 0), _part_block((tm, D_A), na, True, 1),
                  pl.BlockSpec((tm, D_A), lambda i: (i, ocb)),
                  _part_block((tm, D_B), na, False), _part_block((tm, D_B), na, True),
                  pl.BlockSpec((D_A, d), lambda i: (0, 0)),
                  pl.BlockSpec((D_B, d), lambda i: (D_A // D_B, 0)),
                  row,
                  pl.BlockSpec((1, d), lambda i: (0, 0))],
        out_specs=[row, row],
        out_shape=[jax.ShapeDtypeStruct((t, d), F32)] * 2,
        compiler_params=_cparams(1),
        name="outproj",
    )(hs[0], hs[0], hs[1], hs[1], z, ybs[0], ybs[1], w_out, w_out, x, g.reshape(1, d))


def _dft_rows_kernel(c_ref, s_ref, *, n, stride, scale):
    r = lax.broadcasted_iota(I32, c_ref.shape, 0) * stride
    c = lax.broadcasted_iota(I32, c_ref.shape, 1)
    ang = ((r * c) & (n - 1)).astype(F32) * (2.0 * math.pi / n)
    c_ref[...] = jnp.cos(ang) * scale
    s_ref[...] = jnp.sin(ang) * scale


def _dft_rows(n, rows, stride, scale):
    return pl.pallas_call(
        functools.partial(_dft_rows_kernel, n=n, stride=stride, scale=scale),
        out_shape=[jax.ShapeDtypeStruct((rows, n), F32)] * 2,
        compiler_params=_cparams(0),
        name="dft_rows",
    )()


def _dft_combine_kernel(ch_ref, sh_ref, cl_ref, sl_ref, c_ref, s_ref):
    ch, sh = ch_ref[...], sh_ref[...]
    cl, sl = cl_ref[...], sl_ref[...]
    c_ref[...] = (ch * cl - sh * sl).astype(BF16)
    s_ref[...] = (sh * cl + ch * sl).astype(BF16)


def dft_tables(n):
    assert n & (n - 1) == 0
    r = DFT_FINE_ROWS
    nh = n // r
    cl, sl = _dft_rows(n, r, 1, n ** -0.5)
    ch, sh = _dft_rows(n, nh, r, 1.0)
    coarse = pl.BlockSpec((None, 1, n), lambda i: (i, 0, 0))
    fine = pl.BlockSpec((r, n), lambda i: (0, 0))
    blk = pl.BlockSpec((r, n), lambda i: (i, 0))
    return pl.pallas_call(
        _dft_combine_kernel,
        grid=(nh,),
        in_specs=[coarse, coarse, fine, fine],
        out_specs=[blk, blk],
        out_shape=[jax.ShapeDtypeStruct((n, n), BF16)] * 2,
        compiler_params=_cparams(1),
        name="dft_tables",
    )(ch.reshape(nh, 1, n), sh.reshape(nh, 1, n), cl, sl)


def _fourier_ch_kernel(x_ref, cd_ref, sd_ref, yc_ref, ys_ref):
    xb = x_ref[...].astype(BF16)
    yc_ref[...] = jnp.dot(xb, cd_ref[...], preferred_element_type=F32).astype(BF16)
    ys_ref[...] = jnp.dot(xb, sd_ref[...], preferred_element_type=F32).astype(BF16)


def fourier_channels(xn, cd, sd, tm=512):
    t = xn.shape[0]
    tm = min(tm, t)
    gd = FOURIER_GROUP_DIM
    tile = pl.BlockSpec((tm, gd), lambda i, g: (i, g))
    tab = pl.BlockSpec((gd, gd), lambda i, g: (0, 0))
    return pl.pallas_call(
        _fourier_ch_kernel,
        grid=(t // tm, N_FOURIER_GROUPS),
        in_specs=[tile, tab, tab],
        out_specs=[tile, tile],
        out_shape=[jax.ShapeDtypeStruct((t, D_MODEL), BF16)] * 2,
        compiler_params=_cparams(2),
        name="fourier_channels",
    )(xn, cd, sd)


def _fourier_seq_kernel(cs_ref, ss_ref, yc_ref, ys_ref, o_ref, acc_ref):
    kk = pl.program_id(3)

    @pl.when(kk == 0)
    def _():
        acc_ref[...] = jnp.zeros_like(acc_ref)

    acc_ref[...] += (jnp.dot(cs_ref[...], yc_ref[...], preferred_element_type=F32)
                     - jnp.dot(ss_ref[...], ys_ref[...], preferred_element_type=F32))

    @pl.when(kk == pl.num_programs(3) - 1)
    def _():
        o_ref[...] = acc_ref[...].astype(o_ref.dtype)


def fourier_seq(cs, ss, yc, ys, base_tok, batch, seq, tm=1024, tn=1024, tk=512):
    nk = seq // tk
    base = base_tok // tk
    a_spec = pl.BlockSpec((tm, tk), lambda b, i, j, k: (i, k))
    y_spec = pl.BlockSpec((tk, tn), lambda b, i, j, k: (base + b * nk + k, j))
    return pl.pallas_call(
        _fourier_seq_kernel,
        grid=(batch, seq // tm, D_MODEL // tn, nk),
        in_specs=[a_spec, a_spec, y_spec, y_spec],
        out_specs=pl.BlockSpec((tm, tn), lambda b, i, j, k: (b * (seq // tm) + i, j)),
        out_shape=jax.ShapeDtypeStruct((batch * seq, D_MODEL), BF16),
        scratch_shapes=[pltpu.VMEM((tm, tn), F32)],
        compiler_params=_cparams(4),
        name="fourier_seq",
    )(cs, ss, yc, ys)


def _top16_rows(s, row_key):
    n = s.shape[1]
    out_iota = lax.broadcasted_iota(I32, (PEER_TOPK, n), 0)
    vals = jnp.zeros((PEER_TOPK, n), F32)
    keys = jnp.zeros((PEER_TOPK, n), F32)
    for rnk in range(PEER_TOPK):
        mx = jnp.max(s, axis=0, keepdims=True)
        kmin = jnp.min(jnp.where(s == mx, row_key, KEY_SENTINEL), axis=0, keepdims=True)
        vals = jnp.where(out_iota == rnk, mx, vals)
        keys = jnp.where(out_iota == rnk, kmin, keys)
        s = jnp.where(row_key == kmin, -jnp.inf, s)
    return vals, keys


def _peer_topk_kernel(q_ref, k1_ref, k2_ref, e_ref, g_ref):
    tt = q_ref.shape[0]
    k1 = k1_ref[...]
    k2 = k2_ref[...]
    fiota = lambda shape: lax.broadcasted_iota(I32, shape, 0).astype(F32)
    key_iota = fiota((PEER_NKEYS, tt))
    nt = (((1,), (1,)), ((), ()))
    blocks = [(a, 16 if a == 0 else 8) for a in range(8)]
    for h in range(PEER_HEADS):
        q1 = q_ref[:, h * 2 * PEER_HALF: h * 2 * PEER_HALF + PEER_HALF]
        q2 = q_ref[:, h * 2 * PEER_HALF + PEER_HALF: (h + 1) * 2 * PEER_HALF]
        s1 = lax.dot_general(k1, q1, nt, precision=HI, preferred_element_type=F32)
        s2 = lax.dot_general(k2, q2, nt, precision=HI, preferred_element_type=F32)
        v1, i1 = _top16_rows(s1, key_iota)
        v2, i2 = _top16_rows(s2, key_iota)
        cands, ids, orders = [], [], []
        for a, rows in blocks:
            bi = fiota((rows, tt))
            ok = (bi + 1.0) * (a + 1) <= PEER_TOPK
            cands.append(jnp.where(ok, v1[a:a + 1, :] + v2[0:rows, :], -jnp.inf))
            ids.append(i1[a:a + 1, :] * PEER_NKEYS + i2[0:rows, :])
            orders.append(a * PEER_TOPK + bi)
        ai = fiota((8, tt)) + 8.0
        cands.append(v1[8:16, :] + v2[0:1, :])
        ids.append(i1[8:16, :] * PEER_NKEYS + i2[0:1, :])
        orders.append(ai * PEER_TOPK)
        cand = jnp.concatenate(cands, axis=0)
        cid = jnp.concatenate(ids, axis=0)
        order = jnp.concatenate(orders, axis=0)
        ts, tpos = _top16_rows(cand, order)
        experts = jnp.zeros((PEER_TOPK, tt), F32)
        out_iota = lax.broadcasted_iota(I32, (PEER_TOPK, tt), 0)
        for rnk in range(PEER_TOPK):
            sel = jnp.max(jnp.where(order == tpos[rnk:rnk + 1, :], cid, -1.0), axis=0, keepdims=True)
            experts = jnp.where(out_iota == rnk, sel, experts)
        ex = jnp.exp(ts - ts[0:1, :])
        gate = ex / jnp.sum(ex, axis=0, keepdims=True)
        e_ref[h * PEER_TOPK:(h + 1) * PEER_TOPK, :] = experts.astype(I32)
        g_ref[h * PEER_TOPK:(h + 1) * PEER_TOPK, :] = gate


def peer_topk(q, k1, k2, tt=256):
    t = q.shape[0]
    tt = min(tt, t)
    ne = PEER_HEADS * PEER_TOPK
    keys = pl.BlockSpec((PEER_NKEYS, PEER_HALF), lambda i: (0, 0))
    out = pl.BlockSpec((ne, tt), lambda i: (0, i))
    return pl.pallas_call(
        _peer_topk_kernel,
        grid=(t // tt,),
        in_specs=[pl.BlockSpec((tt, q.shape[1]), lambda i: (i, 0)), keys, keys],
        out_specs=[out, out],
        out_shape=[jax.ShapeDtypeStruct((ne, t), I32), jax.ShapeDtypeStruct((ne, t), F32)],
        compiler_params=_cparams(1),
        name="peer_topk",
    )(q, k1, k2)


def _gelu_exact(x):
    return 0.5 * x * (1.0 + lax.erf(x * (2.0 ** -0.5)))


def _peer_dense_kernel(i1a_ref, i2a_ref, ga_ref, i1b_ref, i2b_ref, gb_ref, xn_ref, u_ref, v_ref,
                       out_ref, gate_ref, stage_ref, *, tm, eb):
    i = pl.program_id(0)
    j = pl.program_id(1)
    nk = PEER_NKEYS
    grp = PEER_BUILD_GROUP
    par = i % 2
    key = lax.broadcasted_iota(I32, (nk, nk), 0)
    nt = (((1,), (1,)), ((), ()))

    def build_group(i1_ref, i2_ref, g_ref, t0, slot):
        for k in range(grp):
            t = t0 + k
            a_t = jnp.where(key == i1_ref[pl.ds(t, 1), :], 1.0, 0.0).astype(BF16)
            b_t = jnp.where(key == i2_ref[pl.ds(t, 1), :], g_ref[pl.ds(t, 1), :],
                            0.0).astype(BF16)
            g_t = lax.dot_general(a_t, b_t, nt, preferred_element_type=F32)
            stage_ref[pl.ds(k, nk, stride=grp), :] = g_t
        for r in range(nk):
            gate_ref[slot, pl.ds(pl.multiple_of(r * tm + t0, grp), grp), :] = (
                stage_ref[r * grp:(r + 1) * grp, :].astype(BF16))

    @pl.when(j == 0)
    def _():
        out_ref[...] = jnp.zeros_like(out_ref)

    @pl.when((i == 0) & (j == 0))
    def _():
        def first(tq, carry):
            build_group(i1a_ref, i2a_ref, ga_ref, pl.multiple_of(tq * grp, grp), 0)
            return carry

        lax.fori_loop(0, tm // grp, first, 0)

    build_group(i1b_ref, i2b_ref, gb_ref, pl.multiple_of((j * grp) % tm, grp), 1 - par)

    act = lax.dot_general(xn_ref[...], u_ref[...], nt, preferred_element_type=F32)
    nb = eb // nk
    z = [(_gelu_exact(act[:, b * nk:(b + 1) * nk])
          * gate_ref[par, pl.ds(pl.multiple_of((j * nb + b) * tm, tm), tm), :].astype(F32)
          ).astype(BF16)
         for b in range(nb)]
    out_ref[...] += jnp.dot(jnp.concatenate(z, axis=1), v_ref[...], preferred_element_type=F32)


def peer_dense(xn, expert_t, gate_t, u_tab, v_tab, tm=512, eb=512):
    t, d = xn.shape
    tm = min(tm, t)
    nti = t // tm
    nsteps = N_EXPERTS // eb
    assert tm % PEER_BUILD_GROUP == 0 and tm <= nsteps * PEER_BUILD_GROUP
    ns = expert_t.shape[0]
    i1 = (expert_t // PEER_NKEYS).T
    i2 = (expert_t % PEER_NKEYS).T
    g = gate_t.T
    slot_a = pl.BlockSpec((tm, ns), lambda i, j: (0, 0))
    slot_b = pl.BlockSpec((tm, ns), lambda i, j: (jnp.minimum(i + 1, nti - 1), 0))
    row = pl.BlockSpec((tm, d), lambda i, j: (i, 0), pipeline_mode=pl.Buffered(1))
    return pl.pallas_call(
        functools.partial(_peer_dense_kernel, tm=tm, eb=eb),
        grid=(nti, nsteps),
        in_specs=[slot_a, slot_a, slot_a, slot_b, slot_b, slot_b, row,
                  pl.BlockSpec((eb, d), lambda i, j: (j, 0)),
                  pl.BlockSpec((eb, d), lambda i, j: (j, 0))],
        out_specs=row,
        out_shape=jax.ShapeDtypeStruct((t, d), F32),
        scratch_shapes=[pltpu.VMEM((2, PEER_NKEYS * tm, PEER_NKEYS), BF16),
                        pltpu.VMEM((PEER_NKEYS * PEER_BUILD_GROUP, PEER_NKEYS), F32)],
        compiler_params=_cparams(2),
        name="peer_dense",
    )(i1, i2, g, i1, i2, g, xn.astype(BF16), u_tab.astype(BF16), v_tab.astype(BF16))


def peer_layer(xn, wq, k1, k2, u_tab, v_tab):
    q = matmul(xn, wq.astype(BF16), tn=1024, name="peer_query")
    expert_t, gate_t = peer_topk(q, k1, k2)
    return peer_dense(xn, expert_t, gate_t, u_tab, v_tab)


def _trunk(x, xn, parts, even_mix_norm, even_w_in, even_gate_bias, even_pool_w, even_pool_scale,
           even_w_out, odd_mix_norm, odd_fourier_w, ffn_norm, peer_wq, peer_k1, peer_k2,
           peer_u, peer_v, final_norm):
    n_main = 4 * D_A + D_B
    w_in = even_w_in[0]
    z = matmul(xn, w_in[:, :n_main].astype(BF16), tn=1024, name="in_proj")
    w_gate = jnp.pad(w_in[:, n_main:], ((0, 0), (0, LANES - N_GATES)))
    gates = matmul_hi(xn, w_gate)
    gates_t = gates[:, :N_GATES].T
    bias = even_gate_bias[0].astype(F32)
    bias_row = jnp.pad(bias, (0, LANES - N_GATES)).reshape(1, LANES)
    bias_col = bias.reshape(N_GATES, 1)
    pw = even_pool_w[0].astype(BF16)
    hs, ybs = [], []
    for base, batch, seq in parts:
        hs.append(mlstm(z, gates, gates_t, bias_row, bias_col, base, batch, seq))
        ybs.append(pool(z, pw, even_pool_scale[0], 4 * D_A, base, batch, seq))
    x, xn = outproj(hs, z, ybs, even_w_out[0].astype(BF16), x, ffn_norm[0])
    y = peer_layer(xn, peer_wq[0], peer_k1[0], peer_k2[0], peer_u[0], peer_v[0])

    x, xn = add_rmsnorm(x, y, odd_mix_norm[0])
    cd, sd = dft_tables(FOURIER_GROUP_DIM)
    yc, ys = fourier_channels(xn, cd, sd)
    fs = []
    for base, batch, seq in parts:
        cs, ss = dft_tables(seq)
        fs.append(fourier_seq(cs, ss, yc, ys, base, batch, seq,
                              tm=min(1024, seq), tk=min(512, seq)))
    x, xn = matmul_res_norm(fs, odd_fourier_w[0].astype(BF16), x, ffn_norm[1], name="fourier_out")
    y = peer_layer(xn, peer_wq[1], peer_k1[1], peer_k2[1], peer_u[1], peer_v[1])
    return [add_rmsnorm_part(x, y, final_norm, base, batch * seq) for base, batch, seq in parts]


def kernel(x_prompt, x_sample, even_mix_norm, even_w_in, even_gate_bias, even_pool_w,
           even_pool_scale, even_w_out, odd_mix_norm, odd_fourier_w, ffn_norm, peer_wq,
           peer_k1, peer_k2, peer_u, peer_v, final_norm):
    bp, sp, d = x_prompt.shape
    bs, ss, _ = x_sample.shape
    tp = bp * sp
    x, xn = concat_rmsnorm(x_prompt.reshape(tp, d), x_sample.reshape(bs * ss, d), even_mix_norm[0])
    parts = ((0, bp, sp), (tp, bs, ss))
    yp, ys = _trunk(x, xn, parts, even_mix_norm, even_w_in, even_gate_bias, even_pool_w,
                    even_pool_scale, even_w_out, odd_mix_norm, odd_fourier_w, ffn_norm,
                    peer_wq, peer_k1, peer_k2, peer_u, peer_v, final_norm)
    return yp.reshape(bp, sp, d), ys.reshape(bs, ss, d)
```

```python
import functools
import math

import jax
import jax.numpy as jnp
from jax import lax
from jax.experimental import pallas as pl
from jax.experimental.pallas import tpu as pltpu

F32 = jnp.float32
BF16 = jnp.bfloat16
I32 = jnp.int32
HI = lax.Precision.HIGHEST

D_MODEL = 2048
EPS = 1e-6
N_HEADS = 4
HEAD_DIM = 256
D_A = N_HEADS * HEAD_DIM
D_B = D_MODEL - D_A
N_GATES = 4 * N_HEADS
CHUNK = 256
MLSTM_HEADS_PER_STEP = 2
N_POOL_GROUPS = 4
POOL_TILE = 256
POOL_GROUP_DIM = D_B // N_POOL_GROUPS
N_FOURIER_GROUPS = 4
DFT_FINE_ROWS = 64
FOURIER_GROUP_DIM = D_MODEL // N_FOURIER_GROUPS
PEER_HEADS = 8
PEER_NKEYS = 128
PEER_HALF = 128
PEER_TOPK = 16
N_EXPERTS = PEER_NKEYS * PEER_NKEYS
KEY_SENTINEL = 1e9

LANES = 128
PEER_BUILD_GROUP = 16
VMEM_LIMIT = 56 * 1024 * 1024


def _cparams(n_axes):
    return pltpu.CompilerParams(dimension_semantics=("arbitrary",) * n_axes,
                                vmem_limit_bytes=VMEM_LIMIT)


def _rms(x, g):
    ms = jnp.mean(x * x, axis=-1, keepdims=True)
    return (x * lax.rsqrt(ms + EPS)) * g


def _norm_kernel(a_ref, g_ref, xn_ref):
    xn_ref[...] = _rms(a_ref[...], g_ref[...])


def _addnorm_kernel(a_ref, b_ref, g_ref, x_ref, xn_ref):
    x = a_ref[...] + b_ref[...]
    x_ref[...] = x
    xn_ref[...] = _rms(x, g_ref[...]).astype(xn_ref.dtype)


def _concat_norm_kernel(a_ref, b_ref, g_ref, x_ref, xn_ref, *, na):
    i = pl.program_id(0)

    @pl.when(i < na)
    def _():
        x_ref[...] = a_ref[...]

    @pl.when(i >= na)
    def _():
        x_ref[...] = b_ref[...]

    xn_ref[...] = _rms(x_ref[...], g_ref[...])


def concat_rmsnorm(a, b, g, tm=256):
    d = a.shape[1]
    na, nb = a.shape[0] // tm, b.shape[0] // tm
    row = pl.BlockSpec((tm, d), lambda i: (i, 0))
    return pl.pallas_call(
        functools.partial(_concat_norm_kernel, na=na),
        grid=(na + nb,),
        in_specs=[pl.BlockSpec((tm, d), lambda i: (jnp.minimum(i, na - 1), 0)),
                  pl.BlockSpec((tm, d), lambda i: (jnp.maximum(i - na, 0), 0)),
                  pl.BlockSpec((1, d), lambda i: (0, 0))],
        out_specs=[row, row],
        out_shape=[jax.ShapeDtypeStruct((a.shape[0] + b.shape[0], d), F32)] * 2,
        compiler_params=_cparams(1),
        name="concat_rmsnorm",
    )(a, b, g.reshape(1, d))


def rmsnorm(a, g, tm=256):
    t, d = a.shape
    tm = min(tm, t)
    row = pl.BlockSpec((tm, d), lambda i: (i, 0))
    return pl.pallas_call(
        _norm_kernel,
        grid=(t // tm,),
        in_specs=[row, pl.BlockSpec((1, d), lambda i: (0, 0))],
        out_specs=row,
        out_shape=jax.ShapeDtypeStruct((t, d), F32),
        compiler_params=_cparams(1),
        name="rmsnorm",
    )(a, g.reshape(1, d))


def add_rmsnorm(a, b, g, tm=256):
    t, d = a.shape
    tm = min(tm, t)
    row = pl.BlockSpec((tm, d), lambda i: (i, 0))
    return pl.pallas_call(
        _addnorm_kernel,
        grid=(t // tm,),
        in_specs=[row, row, pl.BlockSpec((1, d), lambda i: (0, 0))],
        out_specs=[row, row],
        out_shape=[jax.ShapeDtypeStruct((t, d), F32), jax.ShapeDtypeStruct((t, d), BF16)],
        compiler_params=_cparams(1),
        name="add_rmsnorm",
    )(a, b, g.reshape(1, d))


def _addnorm_out_kernel(a_ref, b_ref, g_ref, xn_ref):
    xn_ref[...] = _rms(a_ref[...] + b_ref[...], g_ref[...])


def add_rmsnorm_part(a, b, g, base_tok, rows, tm=256):
    d = a.shape[1]
    tm = min(tm, rows)
    src = pl.BlockSpec((tm, d), lambda i: (base_tok // tm + i, 0))
    return pl.pallas_call(
        _addnorm_out_kernel,
        grid=(rows // tm,),
        in_specs=[src, src, pl.BlockSpec((1, d), lambda i: (0, 0))],
        out_specs=pl.BlockSpec((tm, d), lambda i: (i, 0)),
        out_shape=jax.ShapeDtypeStruct((rows, d), F32),
        compiler_params=_cparams(1),
        name="final_norm",
    )(a, b, g.reshape(1, d))


def _mm_kernel(a_ref, w_ref, o_ref):
    o_ref[...] = jnp.dot(a_ref[...].astype(BF16), w_ref[...],
                         preferred_element_type=F32).astype(o_ref.dtype)


def _mm_res_kernel(a_ref, w_ref, r_ref, o_ref):
    acc = jnp.dot(a_ref[...].astype(BF16), w_ref[...], preferred_element_type=F32)
    o_ref[...] = (acc + r_ref[...]).astype(o_ref.dtype)


def _mm_hi_kernel(a_ref, w_ref, o_ref):
    o_ref[...] = jnp.dot(a_ref[...], w_ref[...], precision=HI,
                         preferred_element_type=F32)


def _mm_res_norm_kernel(aa_ref, ab_ref, w_ref, r_ref, g_ref, o_ref, xn_ref, *, na):
    a = jnp.where(pl.program_id(0) < na, aa_ref[...], ab_ref[...])
    x = jnp.dot(a.astype(BF16), w_ref[...], preferred_element_type=F32) + r_ref[...]
    o_ref[...] = x
    xn_ref[...] = _rms(x, g_ref[...]).astype(xn_ref.dtype)


def matmul_res_norm(a_parts, w, res, g, tm=256, name="matmul_res_norm"):
    k = a_parts[0].shape[1]
    m, n = res.shape
    na = a_parts[0].shape[0] // tm
    row = pl.BlockSpec((tm, n), lambda i: (i, 0))
    return pl.pallas_call(
        functools.partial(_mm_res_norm_kernel, na=na),
        grid=(m // tm,),
        in_specs=[pl.BlockSpec((tm, k), lambda i: (jnp.minimum(i, na - 1), 0)),
                  pl.BlockSpec((tm, k), lambda i: (jnp.maximum(i - na, 0), 0)),
                  pl.BlockSpec((k, n), lambda i: (0, 0)),
                  row,
                  pl.BlockSpec((1, n), lambda i: (0, 0))],
        out_specs=[row, row],
        out_shape=[jax.ShapeDtypeStruct((m, n), F32), jax.ShapeDtypeStruct((m, n), BF16)],
        compiler_params=_cparams(1),
        name=name,
    )(a_parts[0], a_parts[1], w, res, g.reshape(1, n))


def matmul(a, w, res=None, tm=512, tn=512, out_dtype=F32, name="matmul"):
    m, k = a.shape
    n = w.shape[1]
    tm = min(tm, m)
    tn = min(tn, n)
    in_specs = [pl.BlockSpec((tm, k), lambda i, j: (i, 0)),
                pl.BlockSpec((k, tn), lambda i, j: (0, j))]
    args = [a, w]
    body = _mm_kernel
    if res is not None:
        in_specs.append(pl.BlockSpec((tm, tn), lambda i, j: (i, j)))
        args.append(res)
        body = _mm_res_kernel
    return pl.pallas_call(
        body,
        grid=(m // tm, n // tn),
        in_specs=in_specs,
        out_specs=pl.BlockSpec((tm, tn), lambda i, j: (i, j)),
        out_shape=jax.ShapeDtypeStruct((m, n), out_dtype),
        compiler_params=_cparams(2),
        name=name,
    )(*args)


def matmul_hi(a, w, tm=512):
    m, k = a.shape
    n = w.shape[1]
    tm = min(tm, m)
    return pl.pallas_call(
        _mm_hi_kernel,
        grid=(m // tm,),
        in_specs=[pl.BlockSpec((tm, k), lambda i: (i, 0)),
                  pl.BlockSpec((k, n), lambda i: (0, 0))],
        out_specs=pl.BlockSpec((tm, n), lambda i: (i, 0)),
        out_shape=jax.ShapeDtypeStruct((m, n), F32),
        compiler_params=_cparams(1),
        name="matmul_hi",
    )(a, w)


def _log_sigmoid(x):
    return jnp.minimum(x, 0.0) - jnp.log(1.0 + jnp.exp(-jnp.abs(x)))


def _mlstm_kernel(q_ref, k_ref, v_ref, gc_ref, gr_ref, bc_ref, br_ref, h_ref,
                  c_ref, n_ref, m_ref):
    d = pl.program_id(0)
    hp = pl.program_id(2)
    c = pl.program_id(3)
    L = CHUNK

    @pl.when(c == 0)
    def _():
        c_ref[...] = jnp.zeros_like(c_ref)
        n_ref[...] = jnp.zeros_like(n_ref)
        m_ref[...] = jnp.zeros_like(m_ref)

    gcol = gc_ref[...] + bc_ref[...]
    grow = gr_ref[...] + br_ref[...]
    lane = lax.broadcasted_iota(I32, gcol.shape, 1)
    sub = lax.broadcasted_iota(I32, grow.shape, 0)
    r = lax.broadcasted_iota(I32, (L, L), 0)
    cc = lax.broadcasted_iota(I32, (L, L), 1)
    sign = 1 - 2 * d
    mask = (cc - r) * sign <= 0
    mask_t = (r - cc) * sign <= 0

    for hh in range(MLSTM_HEADS_PER_STEP):
        cols = slice(hh * HEAD_DIM, (hh + 1) * HEAD_DIM)
        q = q_ref[:, cols] * (HEAD_DIM ** -0.5)
        k = k_ref[:, cols]
        v = v_ref[:, cols]
        li_idx = 2 * d * N_HEADS + hp * MLSTM_HEADS_PER_STEP + hh
        lf_idx = li_idx + N_HEADS
        li_col = jnp.sum(jnp.where(lane == li_idx, gcol, 0.0), axis=1, keepdims=True)
        gf_col = jnp.sum(jnp.where(lane == lf_idx, gcol, 0.0), axis=1, keepdims=True)
        li_row = jnp.sum(jnp.where(sub == li_idx, grow, 0.0), axis=0, keepdims=True)
        gf_row = jnp.sum(jnp.where(sub == lf_idx, grow, 0.0), axis=0, keepdims=True)
        lf_col = _log_sigmoid(gf_col)
        lf_row = _log_sigmoid(gf_row)
        b_col = jnp.sum(jnp.where(mask, lf_row, 0.0), axis=1, keepdims=True)
        b_row = jnp.sum(jnp.where(mask_t, lf_col, 0.0), axis=0, keepdims=True)
        logw = jnp.where(mask, b_col - b_row + li_row, -jnp.inf)

        m_prev = m_ref[hh, :, 0:1]
        inter = b_col + m_prev
        m_t = jnp.maximum(inter, jnp.max(logw, axis=1, keepdims=True))
        w = jnp.exp(logw - m_t)
        a_inter = jnp.exp(inter - m_t)
        qb = q.astype(BF16)
        vb = v.astype(BF16)
        s = lax.dot_general(qb, k.astype(BF16), (((1,), (1,)), ((), ())),
                            preferred_element_type=F32) * w
        cm = c_ref[hh]
        nv = n_ref[hh]
        num = (jnp.dot(s.astype(BF16), vb, preferred_element_type=F32)
               + a_inter * jnp.dot(qb, cm.astype(BF16), preferred_element_type=F32))
        den = (jnp.sum(s, axis=1, keepdims=True)
               + a_inter * jnp.sum(q * nv, axis=1, keepdims=True))
        h_ref[:, cols] = num / jnp.maximum(jnp.abs(den), jnp.exp(-m_t))

        b_last = jnp.sum(lf_row, axis=1, keepdims=True)
        logu = b_last - b_col + li_col
        m_new = jnp.maximum(b_last + m_prev, jnp.max(logu, axis=0, keepdims=True))
        u = jnp.exp(logu - m_new)
        decay = jnp.exp(b_last + m_prev - m_new)
        uk = u * k
        c_ref[hh] = decay * cm + lax.dot_general(
            uk.astype(BF16), vb, (((0,), (0,)), ((), ())), preferred_element_type=F32)
        n_ref[hh] = decay * nv + jnp.sum(uk, axis=0, keepdims=True)
        m_ref[hh] = jnp.broadcast_to(m_new, (1, LANES))


def mlstm(z, gates, gates_t, bias_row, bias_col, base_tok, batch, seq):
    nc = seq // CHUNK
    base = base_tok // CHUNK
    hps = MLSTM_HEADS_PER_STEP
    ngrp = N_HEADS // hps

    def blk(dd, b, c):
        return base + b * nc + c + dd * (nc - 1 - 2 * c)

    def sec(which):
        return pl.BlockSpec((CHUNK, hps * HEAD_DIM),
                            lambda dd, b, h, c: (blk(dd, b, c), which * ngrp + h))

    return pl.pallas_call(
        _mlstm_kernel,
        grid=(2, batch, ngrp, nc),
        in_specs=[sec(0), sec(1), sec(2),
                  pl.BlockSpec((CHUNK, LANES), lambda dd, b, h, c: (blk(dd, b, c), 0)),
                  pl.BlockSpec((N_GATES, CHUNK), lambda dd, b, h, c: (0, blk(dd, b, c))),
                  pl.BlockSpec((1, LANES), lambda dd, b, h, c: (0, 0)),
                  pl.BlockSpec((N_GATES, 1), lambda dd, b, h, c: (0, 0))],
        out_specs=pl.BlockSpec((None, CHUNK, hps * HEAD_DIM),
                               lambda dd, b, h, c: (dd, blk(dd, b, c) - base, h)),
        out_shape=jax.ShapeDtypeStruct((2, batch * seq, D_A), F32),
        scratch_shapes=[pltpu.VMEM((hps, HEAD_DIM, HEAD_DIM), F32),
                        pltpu.VMEM((hps, 1, HEAD_DIM), F32),
                        pltpu.VMEM((hps, 1, LANES), F32)],
        compiler_params=_cparams(4),
        name="mlstm",
    )(z, z, z, gates, gates_t, bias_row, bias_col)


def _pool_kernel(prev_ref, mid_ref, next_ref, pw_ref, sc_ref, o_ref, *, seq):
    i = pl.program_id(1)
    g = pl.program_id(2)
    half = jnp.left_shift(1, g)
    T = POOL_TILE
    p = i * T + lax.broadcasted_iota(I32, (T, T), 0)
    lo = jnp.maximum(p - half, 0)
    hi = jnp.minimum(p + half, seq)
    acc = jnp.zeros((T, POOL_GROUP_DIM), F32)
    for off, ref in ((-T, prev_ref), (0, mid_ref), (T, next_ref)):
        pj = i * T + off + lax.broadcasted_iota(I32, (T, T), 1)
        band = jnp.where((pj >= lo) & (pj < hi), 1.0, 0.0).astype(BF16)
        u = ref[...]
        u_hi = u.astype(BF16)
        u_lo = (u - u_hi.astype(F32)).astype(BF16)
        acc = acc + (jnp.dot(band, u_hi, preferred_element_type=F32)
                     + jnp.dot(band, u_lo, preferred_element_type=F32))
    cnt = (hi - lo)[:, 0:1].astype(F32)
    pooled = acc / cnt - mid_ref[...]
    y = jnp.dot(pooled.astype(BF16), pw_ref[...], preferred_element_type=F32)
    o_ref[...] = y * sc_ref[...]


def pool(z, pool_w, pool_scale, col0, base_tok, batch, seq):
    nt = seq // POOL_TILE
    base = base_tok // POOL_TILE
    cb = col0 // POOL_GROUP_DIM

    def spec(shift):
        return pl.BlockSpec(
            (POOL_TILE, POOL_GROUP_DIM),
            lambda b, i, g: (base + b * nt + jnp.clip(i + shift, 0, nt - 1), cb + g))

    return pl.pallas_call(
        functools.partial(_pool_kernel, seq=seq),
        grid=(batch, nt, N_POOL_GROUPS),
        in_specs=[spec(-1), spec(0), spec(1),
                  pl.BlockSpec((None, POOL_GROUP_DIM, POOL_GROUP_DIM), lambda b, i, g: (g, 0, 0)),
                  pl.BlockSpec((1, POOL_GROUP_DIM), lambda b, i, g: (0, g))],
        out_specs=pl.BlockSpec((POOL_TILE, POOL_GROUP_DIM), lambda b, i, g: (b * nt + i, g)),
        out_shape=jax.ShapeDtypeStruct((batch * seq, D_B), F32),
        compiler_params=_cparams(3),
        name="pool",
    )(z, z, z, pool_w, pool_scale.reshape(1, D_B))


def _outproj_kernel(hfa_ref, hba_ref, hfb_ref, hbb_ref, o_ref, yba_ref, ybb_ref, wa_ref, wb_ref,
                    x_ref, g_ref, out_ref, xn_ref, *, na):
    first = pl.program_id(0) < na
    hsum = jnp.where(first, hfa_ref[...] + hba_ref[...], hfb_ref[...] + hbb_ref[...])
    yb = jnp.where(first, yba_ref[...], ybb_ref[...])
    ha = hsum * jax.nn.sigmoid(o_ref[...])
    acc = jnp.dot(ha.astype(BF16), wa_ref[...], preferred_element_type=F32)
    acc = acc + jnp.dot(yb.astype(BF16), wb_ref[...], preferred_element_type=F32)
    x = acc + x_ref[...]
    out_ref[...] = x
    xn_ref[...] = _rms(x, g_ref[...]).astype(xn_ref.dtype)


def _part_block(shape, na, second, lead=None):
    row = (lambda i: jnp.maximum(i - na, 0)) if second else (lambda i: jnp.minimum(i, na - 1))
    if lead is None:
        return pl.BlockSpec(shape, lambda i: (row(i), 0))
    return pl.BlockSpec((None,) + shape, lambda i: (lead, row(i), 0))


def outproj(hs, z, ybs, w_out, x, g, tm=256):
    t, d = x.shape
    na = hs[0].shape[1] // tm
    ocb = 3 * D_A // D_A
    row = pl.BlockSpec((tm, d), lambda i: (i, 0))
    return pl.pallas_call(
        functools.partial(_outproj_kernel, na=na),
        grid=(t // tm,),
        in_specs=[_part_block((tm, D_A), na, False, 0), _part_block((tm, D_A), na, False, 1),
                  _part_block((tm, D_A), na, True, 0), _part_block((tm, D_A), na, True, 1),
                  pl.BlockSpec((tm, D_A), lambda i: (i, ocb)),
                  _part_block((tm, D_B), na, False), _part_block((tm, D_B), na, True),
                  pl.BlockSpec((D_A, d), lambda i: (0, 0)),
                  pl.BlockSpec((D_B, d), lambda i: (D_A // D_B, 0)),
                  row,
                  pl.BlockSpec((1, d), lambda i: (0, 0))],
        out_specs=[row, row],
        out_shape=[jax.ShapeDtypeStruct((t, d), F32), jax.ShapeDtypeStruct((t, d), BF16)],
        compiler_params=_cparams(1),
        name="outproj",
    )(hs[0], hs[0], hs[1], hs[1], z, ybs[0], ybs[1], w_out, w_out, x, g.reshape(1, d))


def _dft_rows_kernel(c_ref, s_ref, *, n, stride, scale):
    r = lax.broadcasted_iota(I32, c_ref.shape, 0) * stride
    c = lax.broadcasted_iota(I32, c_ref.shape, 1)
    ang = ((r * c) & (n - 1)).astype(F32) * (2.0 * math.pi / n)
    c_ref[...] = jnp.cos(ang) * scale
    s_ref[...] = jnp.sin(ang) * scale


def _dft_rows(n, rows, stride, scale):
    return pl.pallas_call(
        functools.partial(_dft_rows_kernel, n=n, stride=stride, scale=scale),
        out_shape=[jax.ShapeDtypeStruct((rows, n), F32)] * 2,
        compiler_params=_cparams(0),
        name="dft_rows",
    )()


def _dft_combine_kernel(ch_ref, sh_ref, cl_ref, sl_ref, c_ref, s_ref):
    ch, sh = ch_ref[...], sh_ref[...]
    cl, sl = cl_ref[...], sl_ref[...]
    c_ref[...] = (ch * cl - sh * sl).astype(BF16)
    s_ref[...] = (sh * cl + ch * sl).astype(BF16)


def dft_tables(n):
    assert n & (n - 1) == 0
    r = DFT_FINE_ROWS
    nh = n // r
    cl, sl = _dft_rows(n, r, 1, n ** -0.5)
    ch, sh = _dft_rows(n, nh, r, 1.0)
    coarse = pl.BlockSpec((None, 1, n), lambda i: (i, 0, 0))
    fine = pl.BlockSpec((r, n), lambda i: (0, 0))
    blk = pl.BlockSpec((r, n), lambda i: (i, 0))
    return pl.pallas_call(
        _dft_combine_kernel,
        grid=(nh,),
        in_specs=[coarse, coarse, fine, fine],
        out_specs=[blk, blk],
        out_shape=[jax.ShapeDtypeStruct((n, n), BF16)] * 2,
        compiler_params=_cparams(1),
        name="dft_tables",
    )(ch.reshape(nh, 1, n), sh.reshape(nh, 1, n), cl, sl)


def _fourier_ch_kernel(x_ref, cd_ref, sd_ref, yc_ref, ys_ref):
    xb = x_ref[...].astype(BF16)
    yc_ref[...] = jnp.dot(xb, cd_ref[...], preferred_element_type=F32).astype(BF16)
    ys_ref[...] = jnp.dot(xb, sd_ref[...], preferred_element_type=F32).astype(BF16)


def fourier_channels(xn, cd, sd, tm=512):
    t = xn.shape[0]
    tm = min(tm, t)
    gd = FOURIER_GROUP_DIM
    tile = pl.BlockSpec((tm, gd), lambda i, g: (i, g))
    tab = pl.BlockSpec((gd, gd), lambda i, g: (0, 0))
    return pl.pallas_call(
        _fourier_ch_kernel,
        grid=(t // tm, N_FOURIER_GROUPS),
        in_specs=[tile, tab, tab],
        out_specs=[tile, tile],
        out_shape=[jax.ShapeDtypeStruct((t, D_MODEL), BF16)] * 2,
        compiler_params=_cparams(2),
        name="fourier_channels",
    )(xn, cd, sd)


def _fourier_seq_kernel(cs_ref, ss_ref, yc_ref, ys_ref, o_ref, acc_ref):
    kk = pl.program_id(3)

    @pl.when(kk == 0)
    def _():
        acc_ref[...] = jnp.zeros_like(acc_ref)

    acc_ref[...] += (jnp.dot(cs_ref[...], yc_ref[...], preferred_element_type=F32)
                     - jnp.dot(ss_ref[...], ys_ref[...], preferred_element_type=F32))

    @pl.when(kk == pl.num_programs(3) - 1)
    def _():
        o_ref[...] = acc_ref[...].astype(o_ref.dtype)


def fourier_seq(cs, ss, yc, ys, base_tok, batch, seq, tm=1024, tn=1024, tk=512):
    nk = seq // tk
    base = base_tok // tk
    a_spec = pl.BlockSpec((tm, tk), lambda b, i, j, k: (i, k))
    y_spec = pl.BlockSpec((tk, tn), lambda b, i, j, k: (base + b * nk + k, j))
    return pl.pallas_call(
        _fourier_seq_kernel,
        grid=(batch, seq // tm, D_MODEL // tn, nk),
        in_specs=[a_spec, a_spec, y_spec, y_spec],
        out_specs=pl.BlockSpec((tm, tn), lambda b, i, j, k: (b * (seq // tm) + i, j)),
        out_shape=jax.ShapeDtypeStruct((batch * seq, D_MODEL), BF16),
        scratch_shapes=[pltpu.VMEM((tm, tn), F32)],
        compiler_params=_cparams(4),
        name="fourier_seq",
    )(cs, ss, yc, ys)


def _top16_rows(s, row_key):
    n = s.shape[1]
    out_iota = lax.broadcasted_iota(I32, (PEER_TOPK, n), 0)
    vals = jnp.zeros((PEER_TOPK, n), F32)
    keys = jnp.zeros((PEER_TOPK, n), F32)
    for rnk in range(PEER_TOPK):
        mx = jnp.max(s, axis=0, keepdims=True)
        kmin = jnp.min(jnp.where(s == mx, row_key, KEY_SENTINEL), axis=0, keepdims=True)
        vals = jnp.where(out_iota == rnk, mx, vals)
        keys = jnp.where(out_iota == rnk, kmin, keys)
        s = jnp.where(row_key == kmin, -jnp.inf, s)
    return vals, keys


def _peer_topk_kernel(q_ref, k1_ref, k2_ref, e_ref, g_ref):
    tt = q_ref.shape[0]
    k1 = k1_ref[...]
    k2 = k2_ref[...]
    fiota = lambda shape: lax.broadcasted_iota(I32, shape, 0).astype(F32)
    key_iota = fiota((PEER_NKEYS, tt))
    nt = (((1,), (1,)), ((), ()))
    blocks = [(a, 16 if a == 0 else 8) for a in range(8)]
    for h in range(PEER_HEADS):
        q1 = q_ref[:, h * 2 * PEER_HALF: h * 2 * PEER_HALF + PEER_HALF]
        q2 = q_ref[:, h * 2 * PEER_HALF + PEER_HALF: (h + 1) * 2 * PEER_HALF]
        s1 = lax.dot_general(k1, q1, nt, precision=HI, preferred_element_type=F32)
        s2 = lax.dot_general(k2, q2, nt, precision=HI, preferred_element_type=F32)
        v1, i1 = _top16_rows(s1, key_iota)
        v2, i2 = _top16_rows(s2, key_iota)
        cands, ids, orders = [], [], []
        for a, rows in blocks:
            bi = fiota((rows, tt))
            ok = (bi + 1.0) * (a + 1) <= PEER_TOPK
            cands.append(jnp.where(ok, v1[a:a + 1, :] + v2[0:rows, :], -jnp.inf))
            ids.append(i1[a:a + 1, :] * PEER_NKEYS + i2[0:rows, :])
            orders.append(a * PEER_TOPK + bi)
        ai = fiota((8, tt)) + 8.0
        cands.append(v1[8:16, :] + v2[0:1, :])
        ids.append(i1[8:16, :] * PEER_NKEYS + i2[0:1, :])
        orders.append(ai * PEER_TOPK)
        cand = jnp.concatenate(cands, axis=0)
        ckey = jnp.concatenate(orders, axis=0) * float(N_EXPERTS) + jnp.concatenate(ids, axis=0)
        ts, tkey = _top16_rows(cand, ckey)
        experts = tkey - jnp.floor(tkey * (1.0 / N_EXPERTS)) * float(N_EXPERTS)
        ex = jnp.exp(ts - ts[0:1, :])
        gate = ex / jnp.sum(ex, axis=0, keepdims=True)
        e_ref[h * PEER_TOPK:(h + 1) * PEER_TOPK, :] = experts.astype(I32)
        g_ref[h * PEER_TOPK:(h + 1) * PEER_TOPK, :] = gate


def peer_topk(q, k1, k2, tt=256):
    t = q.shape[0]
    tt = min(tt, t)
    ne = PEER_HEADS * PEER_TOPK
    keys = pl.BlockSpec((PEER_NKEYS, PEER_HALF), lambda i: (0, 0))
    out = pl.BlockSpec((ne, tt), lambda i: (0, i))
    return pl.pallas_call(
        _peer_topk_kernel,
        grid=(t // tt,),
        in_specs=[pl.BlockSpec((tt, q.shape[1]), lambda i: (i, 0)), keys, keys],
        out_specs=[out, out],
        out_shape=[jax.ShapeDtypeStruct((ne, t), I32), jax.ShapeDtypeStruct((ne, t), F32)],
        compiler_params=_cparams(1),
        name="peer_topk",
    )(q, k1, k2)


def _gelu_exact(x):
    return 0.5 * x * (1.0 + lax.erf(x * (2.0 ** -0.5)))


def _peer_dense_kernel(i1a_ref, i2a_ref, ga_ref, i1b_ref, i2b_ref, gb_ref, xn_ref, u_ref, v_ref,
                       out_ref, gate_ref, stage_ref, *, tm, eb):
    i = pl.program_id(0)
    j = pl.program_id(1)
    nk = PEER_NKEYS
    grp = PEER_BUILD_GROUP
    par = i % 2
    key = lax.broadcasted_iota(I32, (nk, nk), 0)
    nt = (((1,), (1,)), ((), ()))

    def build_group(i1_ref, i2_ref, g_ref, t0, slot):
        for k in range(grp):
            t = t0 + k
            a_t = jnp.where(key == i1_ref[pl.ds(t, 1), :], 1.0, 0.0).astype(BF16)
            b_t = jnp.where(key == i2_ref[pl.ds(t, 1), :], g_ref[pl.ds(t, 1), :],
                            0.0).astype(BF16)
            g_t = lax.dot_general(a_t, b_t, nt, preferred_element_type=F32)
            stage_ref[pl.ds(k, nk, stride=grp), :] = g_t
        for r in range(nk):
            gate_ref[slot, pl.ds(pl.multiple_of(r * tm + t0, grp), grp), :] = (
                stage_ref[r * grp:(r + 1) * grp, :].astype(BF16))

    @pl.when(j == 0)
    def _():
        out_ref[...] = jnp.zeros_like(out_ref)

    @pl.when((i == 0) & (j == 0))
    def _():
        def first(tq, carry):
            build_group(i1a_ref, i2a_ref, ga_ref, pl.multiple_of(tq * grp, grp), 0)
            return carry

        lax.fori_loop(0, tm // grp, first, 0)

    build_group(i1b_ref, i2b_ref, gb_ref, pl.multiple_of((j * grp) % tm, grp), 1 - par)

    act = lax.dot_general(xn_ref[...], u_ref[...], nt, preferred_element_type=F32)
    nb = eb // nk
    z = [(_gelu_exact(act[:, b * nk:(b + 1) * nk])
          * gate_ref[par, pl.ds(pl.multiple_of((j * nb + b) * tm, tm), tm), :].astype(F32)
          ).astype(BF16)
         for b in range(nb)]
    out_ref[...] += jnp.dot(jnp.concatenate(z, axis=1), v_ref[...], preferred_element_type=F32)


def peer_dense(xn, expert_t, gate_t, u_tab, v_tab, tm=512, eb=512):
    t, d = xn.shape
    tm = min(tm, t)
    nti = t // tm
    nsteps = N_EXPERTS // eb
    assert tm % PEER_BUILD_GROUP == 0 and tm <= nsteps * PEER_BUILD_GROUP
    ns = expert_t.shape[0]
    i1 = (expert_t // PEER_NKEYS).T
    i2 = (expert_t % PEER_NKEYS).T
    g = gate_t.T
    slot_a = pl.BlockSpec((tm, ns), lambda i, j: (0, 0))
    slot_b = pl.BlockSpec((tm, ns), lambda i, j: (jnp.minimum(i + 1, nti - 1), 0))
    row = pl.BlockSpec((tm, d), lambda i, j: (i, 0), pipeline_mode=pl.Buffered(1))
    return pl.pallas_call(
        functools.partial(_peer_dense_kernel, tm=tm, eb=eb),
        grid=(nti, nsteps),
        in_specs=[slot_a, slot_a, slot_a, slot_b, slot_b, slot_b, row,
                  pl.BlockSpec((eb, d), lambda i, j: (j, 0)),
                  pl.BlockSpec((eb, d), lambda i, j: (j, 0))],
        out_specs=row,
        out_shape=jax.ShapeDtypeStruct((t, d), F32),
        scratch_shapes=[pltpu.VMEM((2, PEER_NKEYS * tm, PEER_NKEYS), BF16),
                        pltpu.VMEM((PEER_NKEYS * PEER_BUILD_GROUP, PEER_NKEYS), F32)],
        compiler_params=_cparams(2),
        name="peer_dense",
    )(i1, i2, g, i1, i2, g, xn.astype(BF16), u_tab.astype(BF16), v_tab.astype(BF16))


def peer_layer(xn, wq, k1, k2, u_tab, v_tab):
    q = matmul(xn, wq.astype(BF16), tn=1024, name="peer_query")
    expert_t, gate_t = peer_topk(q, k1, k2)
    return peer_dense(xn, expert_t, gate_t, u_tab, v_tab)


def _trunk(x, xn, parts, even_mix_norm, even_w_in, even_gate_bias, even_pool_w, even_pool_scale,
           even_w_out, odd_mix_norm, odd_fourier_w, ffn_norm, peer_wq, peer_k1, peer_k2,
           peer_u, peer_v, final_norm):
    n_main = 4 * D_A + D_B
    w_in = even_w_in[0]
    z = matmul(xn, w_in[:, :n_main].astype(BF16), tn=1024, name="in_proj")
    w_gate = jnp.pad(w_in[:, n_main:], ((0, 0), (0, LANES - N_GATES)))
    gates = matmul_hi(xn, w_gate)
    gates_t = gates[:, :N_GATES].T
    bias = even_gate_bias[0].astype(F32)
    bias_row = jnp.pad(bias, (0, LANES - N_GATES)).reshape(1, LANES)
    bias_col = bias.reshape(N_GATES, 1)
    pw = even_pool_w[0].astype(BF16)
    hs, ybs = [], []
    for base, batch, seq in parts:
        hs.append(mlstm(z, gates, gates_t, bias_row, bias_col, base, batch, seq))
        ybs.append(pool(z, pw, even_pool_scale[0], 4 * D_A, base, batch, seq))
    x, xn = outproj(hs, z, ybs, even_w_out[0].astype(BF16), x, ffn_norm[0])
    y = peer_layer(xn, peer_wq[0], peer_k1[0], peer_k2[0], peer_u[0], peer_v[0])

    x, xn = add_rmsnorm(x, y, odd_mix_norm[0])
    cd, sd = dft_tables(FOURIER_GROUP_DIM)
    yc, ys = fourier_channels(xn, cd, sd)
    fs = []
    for base, batch, seq in parts:
        cs, ss = dft_tables(seq)
        fs.append(fourier_seq(cs, ss, yc, ys, base, batch, seq,
                              tm=min(1024, seq), tk=min(512, seq)))
    x, xn = matmul_res_norm(fs, odd_fourier_w[0].astype(BF16), x, ffn_norm[1], name="fourier_out")
    y = peer_layer(xn, peer_wq[1], peer_k1[1], peer_k2[1], peer_u[1], peer_v[1])
    return [add_rmsnorm_part(x, y, final_norm, base, batch * seq) for base, batch, seq in parts]


def kernel(x_prompt, x_sample, even_mix_norm, even_w_in, even_gate_bias, even_pool_w,
           even_pool_scale, even_w_out, odd_mix_norm, odd_fourier_w, ffn_norm, peer_wq,
           peer_k1, peer_k2, peer_u, peer_v, final_norm):
    bp, sp, d = x_prompt.shape
    bs, ss, _ = x_sample.shape
    tp = bp * sp
    x, xn = concat_rmsnorm(x_prompt.reshape(tp, d), x_sample.reshape(bs * ss, d), even_mix_norm[0])
    parts = ((0, bp, sp), (tp, bs, ss))
    yp, ys = _trunk(x, xn, parts, even_mix_norm, even_w_in, even_gate_bias, even_pool_w,
                    even_pool_scale, even_w_out, odd_mix_norm, odd_fourier_w, ffn_norm,
                    peer_wq, peer_k1, peer_k2, peer_u, peer_v, final_norm)
    return yp.reshape(bp, sp, d), ys.reshape(bs, ss, d)
```

```python
import functools
import math

import jax
import jax.numpy as jnp
from jax import lax
from jax.experimental import pallas as pl
from jax.experimental.pallas import tpu as pltpu

F32 = jnp.float32
BF16 = jnp.bfloat16
I32 = jnp.int32
HI = lax.Precision.HIGHEST

D_MODEL = 2048
EPS = 1e-6
N_HEADS = 4
HEAD_DIM = 256
D_A = N_HEADS * HEAD_DIM
D_B = D_MODEL - D_A
N_GATES = 4 * N_HEADS
CHUNK = 512
MLSTM_HEADS_PER_STEP = 2
N_POOL_GROUPS = 4
POOL_TILE = 256
POOL_GROUP_DIM = D_B // N_POOL_GROUPS
N_FOURIER_GROUPS = 4
DFT_FINE_ROWS = 64
FOURIER_GROUP_DIM = D_MODEL // N_FOURIER_GROUPS
PEER_HEADS = 8
PEER_NKEYS = 128
PEER_HALF = 128
PEER_TOPK = 16
N_EXPERTS = PEER_NKEYS * PEER_NKEYS
KEY_SENTINEL = 1e9

LANES = 128
PEER_BUILD_GROUP = 16
VMEM_LIMIT = 56 * 1024 * 1024


def _cparams(n_axes):
    return pltpu.CompilerParams(dimension_semantics=("arbitrary",) * n_axes,
                                vmem_limit_bytes=VMEM_LIMIT)


def _rms(x, g):
    ms = jnp.mean(x * x, axis=-1, keepdims=True)
    return (x * lax.rsqrt(ms + EPS)) * g


def _norm_kernel(a_ref, g_ref, xn_ref):
    xn_ref[...] = _rms(a_ref[...], g_ref[...])


def _addnorm_kernel(a_ref, b_ref, g_ref, x_ref, xn_ref):
    x = a_ref[...] + b_ref[...]
    x_ref[...] = x
    xn_ref[...] = _rms(x, g_ref[...]).astype(xn_ref.dtype)


def _concat_norm_kernel(a_ref, b_ref, g_ref, x_ref, xn_ref, *, na):
    i = pl.program_id(0)

    @pl.when(i < na)
    def _():
        x_ref[...] = a_ref[...]

    @pl.when(i >= na)
    def _():
        x_ref[...] = b_ref[...]

    xn_ref[...] = _rms(x_ref[...], g_ref[...])


def concat_rmsnorm(a, b, g, tm=256):
    d = a.shape[1]
    na, nb = a.shape[0] // tm, b.shape[0] // tm
    row = pl.BlockSpec((tm, d), lambda i: (i, 0))
    return pl.pallas_call(
        functools.partial(_concat_norm_kernel, na=na),
        grid=(na + nb,),
        in_specs=[pl.BlockSpec((tm, d), lambda i: (jnp.minimum(i, na - 1), 0)),
                  pl.BlockSpec((tm, d), lambda i: (jnp.maximum(i - na, 0), 0)),
                  pl.BlockSpec((1, d), lambda i: (0, 0))],
        out_specs=[row, row],
        out_shape=[jax.ShapeDtypeStruct((a.shape[0] + b.shape[0], d), F32)] * 2,
        compiler_params=_cparams(1),
        name="concat_rmsnorm",
    )(a, b, g.reshape(1, d))


def rmsnorm(a, g, tm=256):
    t, d = a.shape
    tm = min(tm, t)
    row = pl.BlockSpec((tm, d), lambda i: (i, 0))
    return pl.pallas_call(
        _norm_kernel,
        grid=(t // tm,),
        in_specs=[row, pl.BlockSpec((1, d), lambda i: (0, 0))],
        out_specs=row,
        out_shape=jax.ShapeDtypeStruct((t, d), F32),
        compiler_params=_cparams(1),
        name="rmsnorm",
    )(a, g.reshape(1, d))


def add_rmsnorm(a, b, g, tm=256):
    t, d = a.shape
    tm = min(tm, t)
    row = pl.BlockSpec((tm, d), lambda i: (i, 0))
    return pl.pallas_call(
        _addnorm_kernel,
        grid=(t // tm,),
        in_specs=[row, row, pl.BlockSpec((1, d), lambda i: (0, 0))],
        out_specs=[row, row],
        out_shape=[jax.ShapeDtypeStruct((t, d), F32), jax.ShapeDtypeStruct((t, d), BF16)],
        compiler_params=_cparams(1),
        name="add_rmsnorm",
    )(a, b, g.reshape(1, d))


def _addnorm_out_kernel(a_ref, b_ref, g_ref, xn_ref):
    xn_ref[...] = _rms(a_ref[...] + b_ref[...], g_ref[...])


def add_rmsnorm_part(a, b, g, base_tok, rows, tm=256):
    d = a.shape[1]
    tm = min(tm, rows)
    src = pl.BlockSpec((tm, d), lambda i: (base_tok // tm + i, 0))
    return pl.pallas_call(
        _addnorm_out_kernel,
        grid=(rows // tm,),
        in_specs=[src, src, pl.BlockSpec((1, d), lambda i: (0, 0))],
        out_specs=pl.BlockSpec((tm, d), lambda i: (i, 0)),
        out_shape=jax.ShapeDtypeStruct((rows, d), F32),
        compiler_params=_cparams(1),
        name="final_norm",
    )(a, b, g.reshape(1, d))


def _mm_kernel(a_ref, w_ref, o_ref):
    o_ref[...] = jnp.dot(a_ref[...].astype(BF16), w_ref[...],
                         preferred_element_type=F32).astype(o_ref.dtype)


def _mm_res_kernel(a_ref, w_ref, r_ref, o_ref):
    acc = jnp.dot(a_ref[...].astype(BF16), w_ref[...], preferred_element_type=F32)
    o_ref[...] = (acc + r_ref[...]).astype(o_ref.dtype)


def _mm_hi_kernel(a_ref, w_ref, o_ref):
    o_ref[...] = jnp.dot(a_ref[...], w_ref[...], precision=HI,
                         preferred_element_type=F32)


def _mm_res_norm_kernel(aa_ref, ab_ref, w_ref, r_ref, g_ref, o_ref, xn_ref, *, na):
    a = jnp.where(pl.program_id(0) < na, aa_ref[...], ab_ref[...])
    x = jnp.dot(a.astype(BF16), w_ref[...], preferred_element_type=F32) + r_ref[...]
    o_ref[...] = x
    xn_ref[...] = _rms(x, g_ref[...]).astype(xn_ref.dtype)


def matmul_res_norm(a_parts, w, res, g, tm=256, name="matmul_res_norm"):
    k = a_parts[0].shape[1]
    m, n = res.shape
    na = a_parts[0].shape[0] // tm
    row = pl.BlockSpec((tm, n), lambda i: (i, 0))
    return pl.pallas_call(
        functools.partial(_mm_res_norm_kernel, na=na),
        grid=(m // tm,),
        in_specs=[pl.BlockSpec((tm, k), lambda i: (jnp.minimum(i, na - 1), 0)),
                  pl.BlockSpec((tm, k), lambda i: (jnp.maximum(i - na, 0), 0)),
                  pl.BlockSpec((k, n), lambda i: (0, 0)),
                  row,
                  pl.BlockSpec((1, n), lambda i: (0, 0))],
        out_specs=[row, row],
        out_shape=[jax.ShapeDtypeStruct((m, n), F32), jax.ShapeDtypeStruct((m, n), BF16)],
        compiler_params=_cparams(1),
        name=name,
    )(a_parts[0], a_parts[1], w, res, g.reshape(1, n))


def matmul(a, w, res=None, tm=512, tn=512, out_dtype=F32, name="matmul"):
    m, k = a.shape
    n = w.shape[1]
    tm = min(tm, m)
    tn = min(tn, n)
    in_specs = [pl.BlockSpec((tm, k), lambda i, j: (i, 0)),
                pl.BlockSpec((k, tn), lambda i, j: (0, j))]
    args = [a, w]
    body = _mm_kernel
    if res is not None:
        in_specs.append(pl.BlockSpec((tm, tn), lambda i, j: (i, j)))
        args.append(res)
        body = _mm_res_kernel
    return pl.pallas_call(
        body,
        grid=(m // tm, n // tn),
        in_specs=in_specs,
        out_specs=pl.BlockSpec((tm, tn), lambda i, j: (i, j)),
        out_shape=jax.ShapeDtypeStruct((m, n), out_dtype),
        compiler_params=_cparams(2),
        name=name,
    )(*args)


def matmul_hi(a, w, tm=512):
    m, k = a.shape
    n = w.shape[1]
    tm = min(tm, m)
    return pl.pallas_call(
        _mm_hi_kernel,
        grid=(m // tm,),
        in_specs=[pl.BlockSpec((tm, k), lambda i: (i, 0)),
                  pl.BlockSpec((k, n), lambda i: (0, 0))],
        out_specs=pl.BlockSpec((tm, n), lambda i: (i, 0)),
        out_shape=jax.ShapeDtypeStruct((m, n), F32),
        compiler_params=_cparams(1),
        name="matmul_hi",
    )(a, w)


def _log_sigmoid(x):
    return jnp.minimum(x, 0.0) - jnp.log(1.0 + jnp.exp(-jnp.abs(x)))


def _mlstm_kernel(q_ref, k_ref, v_ref, gc_ref, gr_ref, bc_ref, br_ref, h_ref,
                  c_ref, n_ref, m_ref):
    d = pl.program_id(0)
    hp = pl.program_id(2)
    c = pl.program_id(3)
    L = CHUNK

    @pl.when(c == 0)
    def _():
        c_ref[...] = jnp.zeros_like(c_ref)
        n_ref[...] = jnp.zeros_like(n_ref)
        m_ref[...] = jnp.zeros_like(m_ref)

    gcol = gc_ref[...] + bc_ref[...]
    grow = gr_ref[...] + br_ref[...]
    lane = lax.broadcasted_iota(I32, gcol.shape, 1)
    sub = lax.broadcasted_iota(I32, grow.shape, 0)
    r = lax.broadcasted_iota(I32, (L, L), 0)
    cc = lax.broadcasted_iota(I32, (L, L), 1)
    sign = 1 - 2 * d
    mask = (cc - r) * sign <= 0
    mask_t = (r - cc) * sign <= 0

    for hh in range(MLSTM_HEADS_PER_STEP):
        cols = slice(hh * HEAD_DIM, (hh + 1) * HEAD_DIM)
        q = q_ref[:, cols] * (HEAD_DIM ** -0.5)
        k = k_ref[:, cols]
        v = v_ref[:, cols]
        li_idx = 2 * d * N_HEADS + hp * MLSTM_HEADS_PER_STEP + hh
        lf_idx = li_idx + N_HEADS
        li_col = jnp.sum(jnp.where(lane == li_idx, gcol, 0.0), axis=1, keepdims=True)
        gf_col = jnp.sum(jnp.where(lane == lf_idx, gcol, 0.0), axis=1, keepdims=True)
        li_row = jnp.sum(jnp.where(sub == li_idx, grow, 0.0), axis=0, keepdims=True)
        gf_row = jnp.sum(jnp.where(sub == lf_idx, grow, 0.0), axis=0, keepdims=True)
        lf_col = _log_sigmoid(gf_col)
        lf_row = _log_sigmoid(gf_row)
        b_col = jnp.sum(jnp.where(mask, lf_row, 0.0), axis=1, keepdims=True)
        b_row = jnp.sum(jnp.where(mask_t, lf_col, 0.0), axis=0, keepdims=True)
        logw = jnp.where(mask, b_col - b_row + li_row, -jnp.inf)

        m_prev = m_ref[hh, :, 0:1]
        inter = b_col + m_prev
        m_t = jnp.maximum(inter, jnp.max(logw, axis=1, keepdims=True))
        w = jnp.exp(logw - m_t)
        a_inter = jnp.exp(inter - m_t)
        qb = q.astype(BF16)
        vb = v.astype(BF16)
        s = lax.dot_general(qb, k.astype(BF16), (((1,), (1,)), ((), ())),
                            preferred_element_type=F32) * w
        cm = c_ref[hh]
        nv = n_ref[hh]
        num = (jnp.dot(s.astype(BF16), vb, preferred_element_type=F32)
               + a_inter * jnp.dot(qb, cm.astype(BF16), preferred_element_type=F32))
        den = (jnp.sum(s, axis=1, keepdims=True)
               + a_inter * jnp.sum(q * nv, axis=1, keepdims=True))
        h_ref[:, cols] = num / jnp.maximum(jnp.abs(den), jnp.exp(-m_t))

        b_last = jnp.sum(lf_row, axis=1, keepdims=True)
        logu = b_last - b_col + li_col
        m_new = jnp.maximum(b_last + m_prev, jnp.max(logu, axis=0, keepdims=True))
        u = jnp.exp(logu - m_new)
        decay = jnp.exp(b_last + m_prev - m_new)
        uk = u * k
        c_ref[hh] = decay * cm + lax.dot_general(
            uk.astype(BF16), vb, (((0,), (0,)), ((), ())), preferred_element_type=F32)
        n_ref[hh] = decay * nv + jnp.sum(uk, axis=0, keepdims=True)
        m_ref[hh] = jnp.broadcast_to(m_new, (1, LANES))


def mlstm(z, gates, gates_t, bias_row, bias_col, base_tok, batch, seq):
    nc = seq // CHUNK
    base = base_tok // CHUNK
    hps = MLSTM_HEADS_PER_STEP
    ngrp = N_HEADS // hps

    def blk(dd, b, c):
        return base + b * nc + c + dd * (nc - 1 - 2 * c)

    def sec(which):
        return pl.BlockSpec((CHUNK, hps * HEAD_DIM),
                            lambda dd, b, h, c: (blk(dd, b, c), which * ngrp + h))

    return pl.pallas_call(
        _mlstm_kernel,
        grid=(2, batch, ngrp, nc),
        in_specs=[sec(0), sec(1), sec(2),
                  pl.BlockSpec((CHUNK, LANES), lambda dd, b, h, c: (blk(dd, b, c), 0)),
                  pl.BlockSpec((N_GATES, CHUNK), lambda dd, b, h, c: (0, blk(dd, b, c))),
                  pl.BlockSpec((1, LANES), lambda dd, b, h, c: (0, 0)),
                  pl.BlockSpec((N_GATES, 1), lambda dd, b, h, c: (0, 0))],
        out_specs=pl.BlockSpec((None, CHUNK, hps * HEAD_DIM),
                               lambda dd, b, h, c: (dd, blk(dd, b, c) - base, h)),
        out_shape=jax.ShapeDtypeStruct((2, batch * seq, D_A), F32),
        scratch_shapes=[pltpu.VMEM((hps, HEAD_DIM, HEAD_DIM), F32),
                        pltpu.VMEM((hps, 1, HEAD_DIM), F32),
                        pltpu.VMEM((hps, 1, LANES), F32)],
        compiler_params=_cparams(4),
        name="mlstm",
    )(z, z, z, gates, gates_t, bias_row, bias_col)


def _pool_kernel(prev_ref, mid_ref, next_ref, pw_ref, sc_ref, o_ref, *, seq):
    i = pl.program_id(1)
    g = pl.program_id(2)
    half = jnp.left_shift(1, g)
    T = POOL_TILE
    p = i * T + lax.broadcasted_iota(I32, (T, T), 0)
    lo = jnp.maximum(p - half, 0)
    hi = jnp.minimum(p + half, seq)
    acc = jnp.zeros((T, POOL_GROUP_DIM), F32)
    for off, ref in ((-T, prev_ref), (0, mid_ref), (T, next_ref)):
        pj = i * T + off + lax.broadcasted_iota(I32, (T, T), 1)
        band = jnp.where((pj >= lo) & (pj < hi), 1.0, 0.0).astype(BF16)
        u = ref[...]
        u_hi = u.astype(BF16)
        u_lo = (u - u_hi.astype(F32)).astype(BF16)
        acc = acc + (jnp.dot(band, u_hi, preferred_element_type=F32)
                     + jnp.dot(band, u_lo, preferred_element_type=F32))
    cnt = (hi - lo)[:, 0:1].astype(F32)
    pooled = acc / cnt - mid_ref[...]
    y = jnp.dot(pooled.astype(BF16), pw_ref[...], preferred_element_type=F32)
    o_ref[...] = y * sc_ref[...]


def pool(z, pool_w, pool_scale, col0, base_tok, batch, seq):
    nt = seq // POOL_TILE
    base = base_tok // POOL_TILE
    cb = col0 // POOL_GROUP_DIM

    def spec(shift):
        return pl.BlockSpec(
            (POOL_TILE, POOL_GROUP_DIM),
            lambda b, i, g: (base + b * nt + jnp.clip(i + shift, 0, nt - 1), cb + g))

    return pl.pallas_call(
        functools.partial(_pool_kernel, seq=seq),
        grid=(batch, nt, N_POOL_GROUPS),
        in_specs=[spec(-1), spec(0), spec(1),
                  pl.BlockSpec((None, POOL_GROUP_DIM, POOL_GROUP_DIM), lambda b, i, g: (g, 0, 0)),
                  pl.BlockSpec((1, POOL_GROUP_DIM), lambda b, i, g: (0, g))],
        out_specs=pl.BlockSpec((POOL_TILE, POOL_GROUP_DIM), lambda b, i, g: (b * nt + i, g)),
        out_shape=jax.ShapeDtypeStruct((batch * seq, D_B), F32),
        compiler_params=_cparams(3),
        name="pool",
    )(z, z, z, pool_w, pool_scale.reshape(1, D_B))


def _outproj_kernel(hfa_ref, hba_ref, hfb_ref, hbb_ref, o_ref, yba_ref, ybb_ref, wa_ref, wb_ref,
                    x_ref, g_ref, out_ref, xn_ref, *, na):
    first = pl.program_id(0) < na
    hsum = jnp.where(first, hfa_ref[...] + hba_ref[...], hfb_ref[...] + hbb_ref[...])
    yb = jnp.where(first, yba_ref[...], ybb_ref[...])
    ha = hsum * jax.nn.sigmoid(o_ref[...])
    acc = jnp.dot(ha.astype(BF16), wa_ref[...], preferred_element_type=F32)
    acc = acc + jnp.dot(yb.astype(BF16), wb_ref[...], preferred_element_type=F32)
    x = acc + x_ref[...]
    out_ref[...] = x
    xn_ref[...] = _rms(x, g_ref[...]).astype(xn_ref.dtype)


def _part_block(shape, na, second, lead=None):
    row = (lambda i: jnp.maximum(i - na, 0)) if second else (lambda i: jnp.minimum(i, na - 1))
    if lead is None:
        return pl.BlockSpec(shape, lambda i: (row(i), 0))
    return pl.BlockSpec((None,) + shape, lambda i: (lead, row(i), 0))


def outproj(hs, z, ybs, w_out, x, g, tm=256):
    t, d = x.shape
    na = hs[0].shape[1] // tm
    ocb = 3 * D_A // D_A
    row = pl.BlockSpec((tm, d), lambda i: (i, 0))
    return pl.pallas_call(
        functools.partial(_outproj_kernel, na=na),
        grid=(t // tm,),
        in_specs=[_part_block((tm, D_A), na, False, 0), _part_block((tm, D_A), na, False, 1),
                  _part_block((tm, D_A), na, True, 0), _part_block((tm, D_A), na, True, 1),
                  pl.BlockSpec((tm, D_A), lambda i: (i, ocb)),
                  _part_block((tm, D_B), na, False), _part_block((tm, D_B), na, True),
                  pl.BlockSpec((D_A, d), lambda i: (0, 0)),
                  pl.BlockSpec((D_B, d), lambda i: (D_A // D_B, 0)),
                  row,
                  pl.BlockSpec((1, d), lambda i: (0, 0))],
        out_specs=[row, row],
        out_shape=[jax.ShapeDtypeStruct((t, d), F32), jax.ShapeDtypeStruct((t, d), BF16)],
        compiler_params=_cparams(1),
        name="outproj",
    )(hs[0], hs[0], hs[1], hs[1], z, ybs[0], ybs[1], w_out, w_out, x, g.reshape(1, d))


def _dft_rows_kernel(c_ref, s_ref, *, n, stride, scale):
    r = lax.broadcasted_iota(I32, c_ref.shape, 0) * stride
    c = lax.broadcasted_iota(I32, c_ref.shape, 1)
    ang = ((r * c) & (n - 1)).astype(F32) * (2.0 * math.pi / n)
    c_ref[...] = jnp.cos(ang) * scale
    s_ref[...] = jnp.sin(ang) * scale


def _dft_rows(n, rows, stride, scale):
    return pl.pallas_call(
        functools.partial(_dft_rows_kernel, n=n, stride=stride, scale=scale),
        out_shape=[jax.ShapeDtypeStruct((rows, n), F32)] * 2,
        compiler_params=_cparams(0),
        name="dft_rows",
    )()


def _dft_combine_kernel(ch_ref, sh_ref, cl_ref, sl_ref, c_ref, s_ref):
    ch, sh = ch_ref[...], sh_ref[...]
    cl, sl = cl_ref[...], sl_ref[...]
    c_ref[...] = (ch * cl - sh * sl).astype(BF16)
    s_ref[...] = (sh * cl + ch * sl).astype(BF16)


def dft_tables(n):
    assert n & (n - 1) == 0
    r = DFT_FINE_ROWS
    nh = n // r
    cl, sl = _dft_rows(n, r, 1, n ** -0.5)
    ch, sh = _dft_rows(n, nh, r, 1.0)
    coarse = pl.BlockSpec((None, 1, n), lambda i: (i, 0, 0))
    fine = pl.BlockSpec((r, n), lambda i: (0, 0))
    blk = pl.BlockSpec((r, n), lambda i: (i, 0))
    return pl.pallas_call(
        _dft_combine_kernel,
        grid=(nh,),
        in_specs=[coarse, coarse, fine, fine],
        out_specs=[blk, blk],
        out_shape=[jax.ShapeDtypeStruct((n, n), BF16)] * 2,
        compiler_params=_cparams(1),
        name="dft_tables",
    )(ch.reshape(nh, 1, n), sh.reshape(nh, 1, n), cl, sl)


def _fourier_ch_kernel(x_ref, cd_ref, sd_ref, yc_ref, ys_ref):
    xb = x_ref[...].astype(BF16)
    yc_ref[...] = jnp.dot(xb, cd_ref[...], preferred_element_type=F32).astype(BF16)
    ys_ref[...] = jnp.dot(xb, sd_ref[...], preferred_element_type=F32).astype(BF16)


def fourier_channels(xn, cd, sd, tm=512):
    t = xn.shape[0]
    tm = min(tm, t)
    gd = FOURIER_GROUP_DIM
    tile = pl.BlockSpec((tm, gd), lambda i, g: (i, g))
    tab = pl.BlockSpec((gd, gd), lambda i, g: (0, 0))
    return pl.pallas_call(
        _fourier_ch_kernel,
        grid=(t // tm, N_FOURIER_GROUPS),
        in_specs=[tile, tab, tab],
        out_specs=[tile, tile],
        out_shape=[jax.ShapeDtypeStruct((t, D_MODEL), BF16)] * 2,
        compiler_params=_cparams(2),
        name="fourier_channels",
    )(xn, cd, sd)


def _fourier_seq_kernel(cs_ref, ss_ref, yc_ref, ys_ref, o_ref, acc_ref):
    kk = pl.program_id(3)

    @pl.when(kk == 0)
    def _():
        acc_ref[...] = jnp.zeros_like(acc_ref)

    acc_ref[...] += (jnp.dot(cs_ref[...], yc_ref[...], preferred_element_type=F32)
                     - jnp.dot(ss_ref[...], ys_ref[...], preferred_element_type=F32))

    @pl.when(kk == pl.num_programs(3) - 1)
    def _():
        o_ref[...] = acc_ref[...].astype(o_ref.dtype)


def fourier_seq(cs, ss, yc, ys, base_tok, batch, seq, tm=1024, tn=1024, tk=512):
    nk = seq // tk
    base = base_tok // tk
    a_spec = pl.BlockSpec((tm, tk), lambda b, i, j, k: (i, k))
    y_spec = pl.BlockSpec((tk, tn), lambda b, i, j, k: (base + b * nk + k, j))
    return pl.pallas_call(
        _fourier_seq_kernel,
        grid=(batch, seq // tm, D_MODEL // tn, nk),
        in_specs=[a_spec, a_spec, y_spec, y_spec],
        out_specs=pl.BlockSpec((tm, tn), lambda b, i, j, k: (b * (seq // tm) + i, j)),
        out_shape=jax.ShapeDtypeStruct((batch * seq, D_MODEL), BF16),
        scratch_shapes=[pltpu.VMEM((tm, tn), F32)],
        compiler_params=_cparams(4),
        name="fourier_seq",
    )(cs, ss, yc, ys)


def _top16_rows(s, row_key):
    n = s.shape[1]
    out_iota = lax.broadcasted_iota(I32, (PEER_TOPK, n), 0)
    vals = jnp.zeros((PEER_TOPK, n), F32)
    keys = jnp.zeros((PEER_TOPK, n), F32)
    for rnk in range(PEER_TOPK):
        mx = jnp.max(s, axis=0, keepdims=True)
        kmin = jnp.min(jnp.where(s == mx, row_key, KEY_SENTINEL), axis=0, keepdims=True)
        vals = jnp.where(out_iota == rnk, mx, vals)
        keys = jnp.where(out_iota == rnk, kmin, keys)
        s = jnp.where(row_key == kmin, -jnp.inf, s)
    return vals, keys


def _peer_topk_kernel(q_ref, k1_ref, k2_ref, e_ref, g_ref):
    tt = q_ref.shape[0]
    k1 = k1_ref[...]
    k2 = k2_ref[...]
    fiota = lambda shape: lax.broadcasted_iota(I32, shape, 0).astype(F32)
    key_iota = fiota((PEER_NKEYS, tt))
    nt = (((1,), (1,)), ((), ()))
    blocks = [(a, 16 if a == 0 else 8) for a in range(8)]
    for h in range(PEER_HEADS):
        q1 = q_ref[:, h * 2 * PEER_HALF: h * 2 * PEER_HALF + PEER_HALF]
        q2 = q_ref[:, h * 2 * PEER_HALF + PEER_HALF: (h + 1) * 2 * PEER_HALF]
        s1 = lax.dot_general(k1, q1, nt, precision=HI, preferred_element_type=F32)
        s2 = lax.dot_general(k2, q2, nt, precision=HI, preferred_element_type=F32)
        v1, i1 = _top16_rows(s1, key_iota)
        v2, i2 = _top16_rows(s2, key_iota)
        cands, ids, orders = [], [], []
        for a, rows in blocks:
            bi = fiota((rows, tt))
            ok = (bi + 1.0) * (a + 1) <= PEER_TOPK
            cands.append(jnp.where(ok, v1[a:a + 1, :] + v2[0:rows, :], -jnp.inf))
            ids.append(i1[a:a + 1, :] * PEER_NKEYS + i2[0:rows, :])
            orders.append(a * PEER_TOPK + bi)
        ai = fiota((8, tt)) + 8.0
        cands.append(v1[8:16, :] + v2[0:1, :])
        ids.append(i1[8:16, :] * PEER_NKEYS + i2[0:1, :])
        orders.append(ai * PEER_TOPK)
        cand = jnp.concatenate(cands, axis=0)
        ckey = jnp.concatenate(orders, axis=0) * float(N_EXPERTS) + jnp.concatenate(ids, axis=0)
        ts, tkey = _top16_rows(cand, ckey)
        experts = tkey - jnp.floor(tkey * (1.0 / N_EXPERTS)) * float(N_EXPERTS)
        ex = jnp.exp(ts - ts[0:1, :])
        gate = ex / jnp.sum(ex, axis=0, keepdims=True)
        e_ref[h * PEER_TOPK:(h + 1) * PEER_TOPK, :] = experts.astype(I32)
        g_ref[h * PEER_TOPK:(h + 1) * PEER_TOPK, :] = gate


def peer_topk(q, k1, k2, tt=256):
    t = q.shape[0]
    tt = min(tt, t)
    ne = PEER_HEADS * PEER_TOPK
    keys = pl.BlockSpec((PEER_NKEYS, PEER_HALF), lambda i: (0, 0))
    out = pl.BlockSpec((ne, tt), lambda i: (0, i))
    return pl.pallas_call(
        _peer_topk_kernel,
        grid=(t // tt,),
        in_specs=[pl.BlockSpec((tt, q.shape[1]), lambda i: (i, 0)), keys, keys],
        out_specs=[out, out],
        out_shape=[jax.ShapeDtypeStruct((ne, t), I32), jax.ShapeDtypeStruct((ne, t), F32)],
        compiler_params=_cparams(1),
        name="peer_topk",
    )(q, k1, k2)


def _gelu_exact(x):
    return 0.5 * x * (1.0 + lax.erf(x * (2.0 ** -0.5)))


def _peer_dense_kernel(i1a_ref, i2a_ref, ga_ref, i1b_ref, i2b_ref, gb_ref, xn_ref, u_ref, v_ref,
                       out_ref, gate_ref, stage_ref, *, tm, eb):
    i = pl.program_id(0)
    j = pl.program_id(1)
    nk = PEER_NKEYS
    grp = PEER_BUILD_GROUP
    par = i % 2
    key = lax.broadcasted_iota(I32, (nk, nk), 0)
    nt = (((1,), (1,)), ((), ()))

    def build_group(i1_ref, i2_ref, g_ref, t0, slot):
        for k in range(grp):
            t = t0 + k
            a_t = jnp.where(key == i1_ref[pl.ds(t, 1), :], 1.0, 0.0).astype(BF16)
            b_t = jnp.where(key == i2_ref[pl.ds(t, 1), :], g_ref[pl.ds(t, 1), :],
                            0.0).astype(BF16)
            g_t = lax.dot_general(a_t, b_t, nt, preferred_element_type=F32)
            stage_ref[pl.ds(k, nk, stride=grp), :] = g_t
        for r in range(nk):
            gate_ref[slot, pl.ds(pl.multiple_of(r * tm + t0, grp), grp), :] = (
                stage_ref[r * grp:(r + 1) * grp, :].astype(BF16))

    @pl.when(j == 0)
    def _():
        out_ref[...] = jnp.zeros_like(out_ref)

    @pl.when((i == 0) & (j == 0))
    def _():
        def first(tq, carry):
            build_group(i1a_ref, i2a_ref, ga_ref, pl.multiple_of(tq * grp, grp), 0)
            return carry

        lax.fori_loop(0, tm // grp, first, 0)

    build_group(i1b_ref, i2b_ref, gb_ref, pl.multiple_of((j * grp) % tm, grp), 1 - par)

    act = lax.dot_general(xn_ref[...], u_ref[...], nt, preferred_element_type=F32)
    nb = eb // nk
    z = [(_gelu_exact(act[:, b * nk:(b + 1) * nk])
          * gate_ref[par, pl.ds(pl.multiple_of((j * nb + b) * tm, tm), tm), :].astype(F32)
          ).astype(BF16)
         for b in range(nb)]
    out_ref[...] += jnp.dot(jnp.concatenate(z, axis=1), v_ref[...], preferred_element_type=F32)


def peer_dense(xn, expert_t, gate_t, u_tab, v_tab, tm=512, eb=512):
    t, d = xn.shape
    tm = min(tm, t)
    nti = t // tm
    nsteps = N_EXPERTS // eb
    assert tm % PEER_BUILD_GROUP == 0 and tm <= nsteps * PEER_BUILD_GROUP
    ns = expert_t.shape[0]
    i1 = (expert_t // PEER_NKEYS).T
    i2 = (expert_t % PEER_NKEYS).T
    g = gate_t.T
    slot_a = pl.BlockSpec((tm, ns), lambda i, j: (0, 0))
    slot_b = pl.BlockSpec((tm, ns), lambda i, j: (jnp.minimum(i + 1, nti - 1), 0))
    row = pl.BlockSpec((tm, d), lambda i, j: (i, 0), pipeline_mode=pl.Buffered(1))
    return pl.pallas_call(
        functools.partial(_peer_dense_kernel, tm=tm, eb=eb),
        grid=(nti, nsteps),
        in_specs=[slot_a, slot_a, slot_a, slot_b, slot_b, slot_b, row,
                  pl.BlockSpec((eb, d), lambda i, j: (j, 0)),
                  pl.BlockSpec((eb, d), lambda i, j: (j, 0))],
        out_specs=row,
        out_shape=jax.ShapeDtypeStruct((t, d), F32),
        scratch_shapes=[pltpu.VMEM((2, PEER_NKEYS * tm, PEER_NKEYS), BF16),
                        pltpu.VMEM((PEER_NKEYS * PEER_BUILD_GROUP, PEER_NKEYS), F32)],
        compiler_params=_cparams(2),
        name="peer_dense",
    )(i1, i2, g, i1, i2, g, xn.astype(BF16), u_tab.astype(BF16), v_tab.astype(BF16))


def peer_layer(xn, wq, k1, k2, u_tab, v_tab):
    q = matmul(xn, wq.astype(BF16), tn=1024, name="peer_query")
    expert_t, gate_t = peer_topk(q, k1, k2)
    return peer_dense(xn, expert_t, gate_t, u_tab, v_tab)


def _trunk(x, xn, parts, even_mix_norm, even_w_in, even_gate_bias, even_pool_w, even_pool_scale,
           even_w_out, odd_mix_norm, odd_fourier_w, ffn_norm, peer_wq, peer_k1, peer_k2,
           peer_u, peer_v, final_norm):
    n_main = 4 * D_A + D_B
    w_in = even_w_in[0]
    z = matmul(xn, w_in[:, :n_main].astype(BF16), tn=1024, name="in_proj")
    w_gate = jnp.pad(w_in[:, n_main:], ((0, 0), (0, LANES - N_GATES)))
    gates = matmul_hi(xn, w_gate)
    gates_t = gates[:, :N_GATES].T
    bias = even_gate_bias[0].astype(F32)
    bias_row = jnp.pad(bias, (0, LANES - N_GATES)).reshape(1, LANES)
    bias_col = bias.reshape(N_GATES, 1)
    pw = even_pool_w[0].astype(BF16)
    hs, ybs = [], []
    for base, batch, seq in parts:
        hs.append(mlstm(z, gates, gates_t, bias_row, bias_col, base, batch, seq))
        ybs.append(pool(z, pw, even_pool_scale[0], 4 * D_A, base, batch, seq))
    x, xn = outproj(hs, z, ybs, even_w_out[0].astype(BF16), x, ffn_norm[0])
    y = peer_layer(xn, peer_wq[0], peer_k1[0], peer_k2[0], peer_u[0], peer_v[0])

    x, xn = add_rmsnorm(x, y, odd_mix_norm[0])
    cd, sd = dft_tables(FOURIER_GROUP_DIM)
    yc, ys = fourier_channels(xn, cd, sd)
    fs = []
    for base, batch, seq in parts:
        cs, ss = dft_tables(seq)
        fs.append(fourier_seq(cs, ss, yc, ys, base, batch, seq,
                              tm=min(1024, seq), tk=min(512, seq)))
    x, xn = matmul_res_norm(fs, odd_fourier_w[0].astype(BF16), x, ffn_norm[1], name="fourier_out")
    y = peer_layer(xn, peer_wq[1], peer_k1[1], peer_k2[1], peer_u[1], peer_v[1])
    return [add_rmsnorm_part(x, y, final_norm, base, batch * seq) for base, batch, seq in parts]


def kernel(x_prompt, x_sample, even_mix_norm, even_w_in, even_gate_bias, even_pool_w,
           even_pool_scale, even_w_out, odd_mix_norm, odd_fourier_w, ffn_norm, peer_wq,
           peer_k1, peer_k2, peer_u, peer_v, final_norm):
    bp, sp, d = x_prompt.shape
    bs, ss, _ = x_sample.shape
    tp = bp * sp
    x, xn = concat_rmsnorm(x_prompt.reshape(tp, d), x_sample.reshape(bs * ss, d), even_mix_norm[0])
    parts = ((0, bp, sp), (tp, bs, ss))
    yp, ys = _trunk(x, xn, parts, even_mix_norm, even_w_in, even_gate_bias, even_pool_w,
                    even_pool_scale, even_w_out, odd_mix_norm, odd_fourier_w, ffn_norm,
                    peer_wq, peer_k1, peer_k2, peer_u, peer_v, final_norm)
    return yp.reshape(bp, sp, d), ys.reshape(bs, ss, d)
```

```python
import functools
import math

import jax
import jax.numpy as jnp
from jax import lax
from jax.experimental import pallas as pl
from jax.experimental.pallas import tpu as pltpu

F32 = jnp.float32
BF16 = jnp.bfloat16
I32 = jnp.int32
HI = lax.Precision.HIGHEST

D_MODEL = 2048
EPS = 1e-6
N_HEADS = 4
HEAD_DIM = 256
D_A = N_HEADS * HEAD_DIM
D_B = D_MODEL - D_A
N_GATES = 4 * N_HEADS
CHUNK = 512
MLSTM_HEADS_PER_STEP = 2
N_POOL_GROUPS = 4
POOL_TILE = 256
POOL_GROUP_DIM = D_B // N_POOL_GROUPS
N_FOURIER_GROUPS = 4
DFT_FINE_ROWS = 64
FOURIER_GROUP_DIM = D_MODEL // N_FOURIER_GROUPS
PEER_HEADS = 8
PEER_NKEYS = 128
PEER_HALF = 128
PEER_TOPK = 16
N_EXPERTS = PEER_NKEYS * PEER_NKEYS
KEY_SENTINEL = 1e9

LANES = 128
PEER_BUILD_GROUP = 16
VMEM_LIMIT = 56 * 1024 * 1024


def _cparams(n_axes):
    return pltpu.CompilerParams(dimension_semantics=("arbitrary",) * n_axes,
                                vmem_limit_bytes=VMEM_LIMIT)


def _rms(x, g):
    ms = jnp.mean(x * x, axis=-1, keepdims=True)
    return (x * lax.rsqrt(ms + EPS)) * g


def _norm_kernel(a_ref, g_ref, xn_ref):
    xn_ref[...] = _rms(a_ref[...], g_ref[...])


def _addnorm_kernel(a_ref, b_ref, g_ref, x_ref, xn_ref):
    x = a_ref[...] + b_ref[...]
    x_ref[...] = x
    xn_ref[...] = _rms(x, g_ref[...]).astype(xn_ref.dtype)


def _concat_norm_kernel(a_ref, b_ref, g_ref, x_ref, xn_ref, *, na):
    i = pl.program_id(0)

    @pl.when(i < na)
    def _():
        x_ref[...] = a_ref[...]

    @pl.when(i >= na)
    def _():
        x_ref[...] = b_ref[...]

    xn_ref[...] = _rms(x_ref[...], g_ref[...])


def concat_rmsnorm(a, b, g, tm=256):
    d = a.shape[1]
    na, nb = a.shape[0] // tm, b.shape[0] // tm
    row = pl.BlockSpec((tm, d), lambda i: (i, 0))
    return pl.pallas_call(
        functools.partial(_concat_norm_kernel, na=na),
        grid=(na + nb,),
        in_specs=[pl.BlockSpec((tm, d), lambda i: (jnp.minimum(i, na - 1), 0)),
                  pl.BlockSpec((tm, d), lambda i: (jnp.maximum(i - na, 0), 0)),
                  pl.BlockSpec((1, d), lambda i: (0, 0))],
        out_specs=[row, row],
        out_shape=[jax.ShapeDtypeStruct((a.shape[0] + b.shape[0], d), F32)] * 2,
        compiler_params=_cparams(1),
        name="concat_rmsnorm",
    )(a, b, g.reshape(1, d))


def rmsnorm(a, g, tm=256):
    t, d = a.shape
    tm = min(tm, t)
    row = pl.BlockSpec((tm, d), lambda i: (i, 0))
    return pl.pallas_call(
        _norm_kernel,
        grid=(t // tm,),
        in_specs=[row, pl.BlockSpec((1, d), lambda i: (0, 0))],
        out_specs=row,
        out_shape=jax.ShapeDtypeStruct((t, d), F32),
        compiler_params=_cparams(1),
        name="rmsnorm",
    )(a, g.reshape(1, d))


def add_rmsnorm(a, b, g, tm=256):
    t, d = a.shape
    tm = min(tm, t)
    row = pl.BlockSpec((tm, d), lambda i: (i, 0))
    return pl.pallas_call(
        _addnorm_kernel,
        grid=(t // tm,),
        in_specs=[row, row, pl.BlockSpec((1, d), lambda i: (0, 0))],
        out_specs=[row, row],
        out_shape=[jax.ShapeDtypeStruct((t, d), F32), jax.ShapeDtypeStruct((t, d), BF16)],
        compiler_params=_cparams(1),
        name="add_rmsnorm",
    )(a, b, g.reshape(1, d))


def _addnorm_out_kernel(a_ref, b_ref, g_ref, xn_ref):
    xn_ref[...] = _rms(a_ref[...] + b_ref[...], g_ref[...])


def add_rmsnorm_part(a, b, g, base_tok, rows, tm=256):
    d = a.shape[1]
    tm = min(tm, rows)
    src = pl.BlockSpec((tm, d), lambda i: (base_tok // tm + i, 0))
    return pl.pallas_call(
        _addnorm_out_kernel,
        grid=(rows // tm,),
        in_specs=[src, src, pl.BlockSpec((1, d), lambda i: (0, 0))],
        out_specs=pl.BlockSpec((tm, d), lambda i: (i, 0)),
        out_shape=jax.ShapeDtypeStruct((rows, d), F32),
        compiler_params=_cparams(1),
        name="final_norm",
    )(a, b, g.reshape(1, d))


def _mm_kernel(a_ref, w_ref, o_ref):
    o_ref[...] = jnp.dot(a_ref[...].astype(BF16), w_ref[...],
                         preferred_element_type=F32).astype(o_ref.dtype)


def _mm_res_kernel(a_ref, w_ref, r_ref, o_ref):
    acc = jnp.dot(a_ref[...].astype(BF16), w_ref[...], preferred_element_type=F32)
    o_ref[...] = (acc + r_ref[...]).astype(o_ref.dtype)


def _mm_hi_kernel(a_ref, w_ref, o_ref):
    o_ref[...] = jnp.dot(a_ref[...], w_ref[...], precision=HI,
                         preferred_element_type=F32)


def _mm_res_norm_kernel(aa_ref, ab_ref, w_ref, r_ref, g_ref, o_ref, xn_ref, *, na):
    a = jnp.where(pl.program_id(0) < na, aa_ref[...], ab_ref[...])
    x = jnp.dot(a.astype(BF16), w_ref[...], preferred_element_type=F32) + r_ref[...]
    o_ref[...] = x
    xn_ref[...] = _rms(x, g_ref[...]).astype(xn_ref.dtype)


def matmul_res_norm(a_parts, w, res, g, tm=256, name="matmul_res_norm"):
    k = a_parts[0].shape[1]
    m, n = res.shape
    na = a_parts[0].shape[0] // tm
    row = pl.BlockSpec((tm, n), lambda i: (i, 0))
    return pl.pallas_call(
        functools.partial(_mm_res_norm_kernel, na=na),
        grid=(m // tm,),
        in_specs=[pl.BlockSpec((tm, k), lambda i: (jnp.minimum(i, na - 1), 0)),
                  pl.BlockSpec((tm, k), lambda i: (jnp.maximum(i - na, 0), 0)),
                  pl.BlockSpec((k, n), lambda i: (0, 0)),
                  row,
                  pl.BlockSpec((1, n), lambda i: (0, 0))],
        out_specs=[row, row],
        out_shape=[jax.ShapeDtypeStruct((m, n), F32), jax.ShapeDtypeStruct((m, n), BF16)],
        compiler_params=_cparams(1),
        name=name,
    )(a_parts[0], a_parts[1], w, res, g.reshape(1, n))


def matmul(a, w, res=None, tm=512, tn=512, out_dtype=F32, name="matmul"):
    m, k = a.shape
    n = w.shape[1]
    tm = min(tm, m)
    tn = min(tn, n)
    in_specs = [pl.BlockSpec((tm, k), lambda i, j: (i, 0)),
                pl.BlockSpec((k, tn), lambda i, j: (0, j))]
    args = [a, w]
    body = _mm_kernel
    if res is not None:
        in_specs.append(pl.BlockSpec((tm, tn), lambda i, j: (i, j)))
        args.append(res)
        body = _mm_res_kernel
    return pl.pallas_call(
        body,
        grid=(m // tm, n // tn),
        in_specs=in_specs,
        out_specs=pl.BlockSpec((tm, tn), lambda i, j: (i, j)),
        out_shape=jax.ShapeDtypeStruct((m, n), out_dtype),
        compiler_params=_cparams(2),
        name=name,
    )(*args)


def matmul_hi(a, w, tm=512):
    m, k = a.shape
    n = w.shape[1]
    tm = min(tm, m)
    return pl.pallas_call(
        _mm_hi_kernel,
        grid=(m // tm,),
        in_specs=[pl.BlockSpec((tm, k), lambda i: (i, 0)),
                  pl.BlockSpec((k, n), lambda i: (0, 0))],
        out_specs=pl.BlockSpec((tm, n), lambda i: (i, 0)),
        out_shape=jax.ShapeDtypeStruct((m, n), F32),
        compiler_params=_cparams(1),
        name="matmul_hi",
    )(a, w)


def _log_sigmoid(x):
    return jnp.minimum(x, 0.0) - jnp.log(1.0 + jnp.exp(-jnp.abs(x)))


def _mlstm_kernel(q_ref, k_ref, v_ref, gc_ref, gr_ref, bc_ref, br_ref, h_ref,
                  c_ref, n_ref, m_ref):
    d = pl.program_id(0)
    hp = pl.program_id(2)
    c = pl.program_id(3)
    L = CHUNK

    @pl.when(c == 0)
    def _():
        c_ref[...] = jnp.zeros_like(c_ref)
        n_ref[...] = jnp.zeros_like(n_ref)
        m_ref[...] = jnp.zeros_like(m_ref)

    gcol = gc_ref[...] + bc_ref[...]
    grow = gr_ref[...] + br_ref[...]
    lane = lax.broadcasted_iota(I32, gcol.shape, 1)
    sub = lax.broadcasted_iota(I32, grow.shape, 0)
    r = lax.broadcasted_iota(I32, (L, L), 0)
    cc = lax.broadcasted_iota(I32, (L, L), 1)
    sign = 1 - 2 * d
    mask = (cc - r) * sign <= 0
    mask_t = (r - cc) * sign <= 0

    for hh in range(MLSTM_HEADS_PER_STEP):
        cols = slice(hh * HEAD_DIM, (hh + 1) * HEAD_DIM)
        q = q_ref[:, cols] * (HEAD_DIM ** -0.5)
        k = k_ref[:, cols]
        v = v_ref[:, cols]
        li_idx = 2 * d * N_HEADS + hp * MLSTM_HEADS_PER_STEP + hh
        lf_idx = li_idx + N_HEADS
        li_col = jnp.sum(jnp.where(lane == li_idx, gcol, 0.0), axis=1, keepdims=True)
        gf_col = jnp.sum(jnp.where(lane == lf_idx, gcol, 0.0), axis=1, keepdims=True)
        li_row = jnp.sum(jnp.where(sub == li_idx, grow, 0.0), axis=0, keepdims=True)
        gf_row = jnp.sum(jnp.where(sub == lf_idx, grow, 0.0), axis=0, keepdims=True)
        lf_col = _log_sigmoid(gf_col)
        lf_row = _log_sigmoid(gf_row)
        b_col = jnp.sum(jnp.where(mask, lf_row, 0.0), axis=1, keepdims=True)
        b_row = jnp.sum(jnp.where(mask_t, lf_col, 0.0), axis=0, keepdims=True)
        logw = jnp.where(mask, b_col - b_row + li_row, -jnp.inf)

        m_prev = m_ref[hh, :, 0:1]
        inter = b_col + m_prev
        m_t = jnp.maximum(inter, jnp.max(logw, axis=1, keepdims=True))
        w = jnp.exp(logw - m_t)
        a_inter = jnp.exp(inter - m_t)
        qb = q.astype(BF16)
        vb = v.astype(BF16)
        s = lax.dot_general(qb, k.astype(BF16), (((1,), (1,)), ((), ())),
                            preferred_element_type=F32) * w
        cm = c_ref[hh]
        nv = n_ref[hh]
        num = (jnp.dot(s.astype(BF16), vb, preferred_element_type=F32)
               + a_inter * jnp.dot(qb, cm.astype(BF16), preferred_element_type=F32))
        den = (jnp.sum(s, axis=1, keepdims=True)
               + a_inter * jnp.sum(q * nv, axis=1, keepdims=True))
        h_ref[:, cols] = num / jnp.maximum(jnp.abs(den), jnp.exp(-m_t))

        b_last = jnp.sum(lf_row, axis=1, keepdims=True)
        logu = b_last - b_col + li_col
        m_new = jnp.maximum(b_last + m_prev, jnp.max(logu, axis=0, keepdims=True))
        u = jnp.exp(logu - m_new)
        decay = jnp.exp(b_last + m_prev - m_new)
        uk = u * k
        c_ref[hh] = decay * cm + lax.dot_general(
            uk.astype(BF16), vb, (((0,), (0,)), ((), ())), preferred_element_type=F32)
        n_ref[hh] = decay * nv + jnp.sum(uk, axis=0, keepdims=True)
        m_ref[hh] = jnp.broadcast_to(m_new, (1, LANES))


def mlstm(z, gates, gates_t, bias_row, bias_col, base_tok, batch, seq):
    nc = seq // CHUNK
    base = base_tok // CHUNK
    hps = MLSTM_HEADS_PER_STEP
    ngrp = N_HEADS // hps

    def blk(dd, b, c):
        return base + b * nc + c + dd * (nc - 1 - 2 * c)

    def sec(which):
        return pl.BlockSpec((CHUNK, hps * HEAD_DIM),
                            lambda dd, b, h, c: (blk(dd, b, c), which * ngrp + h))

    return pl.pallas_call(
        _mlstm_kernel,
        grid=(2, batch, ngrp, nc),
        in_specs=[sec(0), sec(1), sec(2),
                  pl.BlockSpec((CHUNK, LANES), lambda dd, b, h, c: (blk(dd, b, c), 0)),
                  pl.BlockSpec((N_GATES, CHUNK), lambda dd, b, h, c: (0, blk(dd, b, c))),
                  pl.BlockSpec((1, LANES), lambda dd, b, h, c: (0, 0)),
                  pl.BlockSpec((N_GATES, 1), lambda dd, b, h, c: (0, 0))],
        out_specs=pl.BlockSpec((None, CHUNK, hps * HEAD_DIM),
                               lambda dd, b, h, c: (dd, blk(dd, b, c) - base, h)),
        out_shape=jax.ShapeDtypeStruct((2, batch * seq, D_A), F32),
        scratch_shapes=[pltpu.VMEM((hps, HEAD_DIM, HEAD_DIM), F32),
                        pltpu.VMEM((hps, 1, HEAD_DIM), F32),
                        pltpu.VMEM((hps, 1, LANES), F32)],
        compiler_params=_cparams(4),
        name="mlstm",
    )(z, z, z, gates, gates_t, bias_row, bias_col)


def _pool_kernel(prev_ref, mid_ref, next_ref, pw_ref, sc_ref, o_ref, *, seq):
    i = pl.program_id(1)
    g = pl.program_id(2)
    half = jnp.left_shift(1, g)
    T = POOL_TILE
    p = i * T + lax.broadcasted_iota(I32, (T, T), 0)
    lo = jnp.maximum(p - half, 0)
    hi = jnp.minimum(p + half, seq)
    acc = jnp.zeros((T, POOL_GROUP_DIM), F32)
    for off, ref in ((-T, prev_ref), (0, mid_ref), (T, next_ref)):
        pj = i * T + off + lax.broadcasted_iota(I32, (T, T), 1)
        band = jnp.where((pj >= lo) & (pj < hi), 1.0, 0.0).astype(BF16)
        u = ref[...]
        u_hi = u.astype(BF16)
        u_lo = (u - u_hi.astype(F32)).astype(BF16)
        acc = acc + (jnp.dot(band, u_hi, preferred_element_type=F32)
                     + jnp.dot(band, u_lo, preferred_element_type=F32))
    cnt = (hi - lo)[:, 0:1].astype(F32)
    pooled = acc / cnt - mid_ref[...]
    y = jnp.dot(pooled.astype(BF16), pw_ref[...], preferred_element_type=F32)
    o_ref[...] = y * sc_ref[...]


def pool(z, pool_w, pool_scale, col0, base_tok, batch, seq):
    nt = seq // POOL_TILE
    base = base_tok // POOL_TILE
    cb = col0 // POOL_GROUP_DIM

    def spec(shift):
        return pl.BlockSpec(
            (POOL_TILE, POOL_GROUP_DIM),
            lambda b, i, g: (base + b * nt + jnp.clip(i + shift, 0, nt - 1), cb + g))

    return pl.pallas_call(
        functools.partial(_pool_kernel, seq=seq),
        grid=(batch, nt, N_POOL_GROUPS),
        in_specs=[spec(-1), spec(0), spec(1),
                  pl.BlockSpec((None, POOL_GROUP_DIM, POOL_GROUP_DIM), lambda b, i, g: (g, 0, 0)),
                  pl.BlockSpec((1, POOL_GROUP_DIM), lambda b, i, g: (0, g))],
        out_specs=pl.BlockSpec((POOL_TILE, POOL_GROUP_DIM), lambda b, i, g: (b * nt + i, g)),
        out_shape=jax.ShapeDtypeStruct((batch * seq, D_B), F32),
        compiler_params=_cparams(3),
        name="pool",
    )(z, z, z, pool_w, pool_scale.reshape(1, D_B))


def _outproj_kernel(hfa_ref, hba_ref, hfb_ref, hbb_ref, o_ref, yba_ref, ybb_ref, wa_ref, wb_ref,
                    x_ref, g_ref, out_ref, xn_ref, *, na):
    first = pl.program_id(0) < na
    hsum = jnp.where(first, hfa_ref[...] + hba_ref[...], hfb_ref[...] + hbb_ref[...])
    yb = jnp.where(first, yba_ref[...], ybb_ref[...])
    ha = hsum * jax.nn.sigmoid(o_ref[...])
    acc = jnp.dot(ha.astype(BF16), wa_ref[...], preferred_element_type=F32)
    acc = acc + jnp.dot(yb.astype(BF16), wb_ref[...], preferred_element_type=F32)
    x = acc + x_ref[...]
    out_ref[...] = x
    xn_ref[...] = _rms(x, g_ref[...]).astype(xn_ref.dtype)


def _part_block(shape, na, second, lead=None):
    row = (lambda i: jnp.maximum(i - na, 0)) if second else (lambda i: jnp.minimum(i, na - 1))
    if lead is None:
        return pl.BlockSpec(shape, lambda i: (row(i), 0))
    return pl.BlockSpec((None,) + shape, lambda i: (lead, row(i), 0))


def outproj(hs, z, ybs, w_out, x, g, tm=256):
    t, d = x.shape
    na = hs[0].shape[1] // tm
    ocb = 3 * D_A // D_A
    row = pl.BlockSpec((tm, d), lambda i: (i, 0))
    return pl.pallas_call(
        functools.partial(_outproj_kernel, na=na),
        grid=(t // tm,),
        in_specs=[_part_block((tm, D_A), na, False, 0), _part_block((tm, D_A), na, False, 1),
                  _part_block((tm, D_A), na, True, 0), _part_block((tm, D_A), na, True, 1),
                  pl.BlockSpec((tm, D_A), lambda i: (i, ocb)),
                  _part_block((tm, D_B), na, False), _part_block((tm, D_B), na, True),
                  pl.BlockSpec((D_A, d), lambda i: (0, 0)),
                  pl.BlockSpec((D_B, d), lambda i: (D_A // D_B, 0)),
                  row,
                  pl.BlockSpec((1, d), lambda i: (0, 0))],
        out_specs=[row, row],
        out_shape=[jax.ShapeDtypeStruct((t, d), F32), jax.ShapeDtypeStruct((t, d), BF16)],
        compiler_params=_cparams(1),
        name="outproj",
    )(hs[0], hs[0], hs[1], hs[1], z, ybs[0], ybs[1], w_out, w_out, x, g.reshape(1, d))


def _dft_rows_kernel(c_ref, s_ref, *, n, stride, scale):
    r = lax.broadcasted_iota(I32, c_ref.shape, 0) * stride
    c = lax.broadcasted_iota(I32, c_ref.shape, 1)
    ang = ((r * c) & (n - 1)).astype(F32) * (2.0 * math.pi / n)
    c_ref[...] = jnp.cos(ang) * scale
    s_ref[...] = jnp.sin(ang) * scale


def _dft_rows(n, rows, stride, scale):
    return pl.pallas_call(
        functools.partial(_dft_rows_kernel, n=n, stride=stride, scale=scale),
        out_shape=[jax.ShapeDtypeStruct((rows, n), F32)] * 2,
        compiler_params=_cparams(0),
        name="dft_rows",
    )()


def _dft_combine_kernel(ch_ref, sh_ref, cl_ref, sl_ref, c_ref, s_ref):
    ch, sh = ch_ref[...], sh_ref[...]
    cl, sl = cl_ref[...], sl_ref[...]
    c_ref[...] = (ch * cl - sh * sl).astype(BF16)
    s_ref[...] = (sh * cl + ch * sl).astype(BF16)


def dft_tables(n):
    assert n & (n - 1) == 0
    r = DFT_FINE_ROWS
    nh = n // r
    cl, sl = _dft_rows(n, r, 1, n ** -0.5)
    ch, sh = _dft_rows(n, nh, r, 1.0)
    coarse = pl.BlockSpec((None, 1, n), lambda i: (i, 0, 0))
    fine = pl.BlockSpec((r, n), lambda i: (0, 0))
    blk = pl.BlockSpec((r, n), lambda i: (i, 0))
    return pl.pallas_call(
        _dft_combine_kernel,
        grid=(nh,),
        in_specs=[coarse, coarse, fine, fine],
        out_specs=[blk, blk],
        out_shape=[jax.ShapeDtypeStruct((n, n), BF16)] * 2,
        compiler_params=_cparams(1),
        name="dft_tables",
    )(ch.reshape(nh, 1, n), sh.reshape(nh, 1, n), cl, sl)


def _fourier_ch_kernel(x_ref, cd_ref, sd_ref, yc_ref, ys_ref):
    xb = x_ref[...].astype(BF16)
    yc_ref[...] = jnp.dot(xb, cd_ref[...], preferred_element_type=F32).astype(BF16)
    ys_ref[...] = jnp.dot(xb, sd_ref[...], preferred_element_type=F32).astype(BF16)


def fourier_channels(xn, cd, sd, tm=512):
    t = xn.shape[0]
    tm = min(tm, t)
    gd = FOURIER_GROUP_DIM
    tile = pl.BlockSpec((tm, gd), lambda i, g: (i, g))
    tab = pl.BlockSpec((gd, gd), lambda i, g: (0, 0))
    return pl.pallas_call(
        _fourier_ch_kernel,
        grid=(t // tm, N_FOURIER_GROUPS),
        in_specs=[tile, tab, tab],
        out_specs=[tile, tile],
        out_shape=[jax.ShapeDtypeStruct((t, D_MODEL), BF16)] * 2,
        compiler_params=_cparams(2),
        name="fourier_channels",
    )(xn, cd, sd)


def _fourier_seq_kernel(cs_ref, ss_ref, yc_ref, ys_ref, o_ref, acc_ref):
    kk = pl.program_id(3)

    @pl.when(kk == 0)
    def _():
        acc_ref[...] = jnp.zeros_like(acc_ref)

    acc_ref[...] += (jnp.dot(cs_ref[...], yc_ref[...], preferred_element_type=F32)
                     - jnp.dot(ss_ref[...], ys_ref[...], preferred_element_type=F32))

    @pl.when(kk == pl.num_programs(3) - 1)
    def _():
        o_ref[...] = acc_ref[...].astype(o_ref.dtype)


def fourier_seq(cs, ss, yc, ys, base_tok, batch, seq, tm=1024, tn=1024, tk=512):
    nk = seq // tk
    base = base_tok // tk
    a_spec = pl.BlockSpec((tm, tk), lambda b, i, j, k: (i, k))
    y_spec = pl.BlockSpec((tk, tn), lambda b, i, j, k: (base + b * nk + k, j))
    return pl.pallas_call(
        _fourier_seq_kernel,
        grid=(batch, seq // tm, D_MODEL // tn, nk),
        in_specs=[a_spec, a_spec, y_spec, y_spec],
        out_specs=pl.BlockSpec((tm, tn), lambda b, i, j, k: (b * (seq // tm) + i, j)),
        out_shape=jax.ShapeDtypeStruct((batch * seq, D_MODEL), BF16),
        scratch_shapes=[pltpu.VMEM((tm, tn), F32)],
        compiler_params=_cparams(4),
        name="fourier_seq",
    )(cs, ss, yc, ys)


def _top16_rows(s, row_key):
    n = s.shape[1]
    out_iota = lax.broadcasted_iota(I32, (PEER_TOPK, n), 0)
    vals = jnp.zeros((PEER_TOPK, n), F32)
    keys = jnp.zeros((PEER_TOPK, n), F32)
    for rnk in range(PEER_TOPK):
        mx = jnp.max(s, axis=0, keepdims=True)
        kmin = jnp.min(jnp.where(s == mx, row_key, KEY_SENTINEL), axis=0, keepdims=True)
        vals = jnp.where(out_iota == rnk, mx, vals)
        keys = jnp.where(out_iota == rnk, kmin, keys)
        s = jnp.where(row_key == kmin, -jnp.inf, s)
    return vals, keys


def _peer_topk_kernel(xn_ref, wq_ref, k1_ref, k2_ref, e_ref, g_ref):
    tt = xn_ref.shape[0]
    xb = xn_ref[...].astype(BF16)
    k1 = k1_ref[...]
    k2 = k2_ref[...]
    fiota = lambda shape: lax.broadcasted_iota(I32, shape, 0).astype(F32)
    key_iota = fiota((PEER_NKEYS, tt))
    nt = (((1,), (1,)), ((), ()))
    blocks = [(a, 16 if a == 0 else 8) for a in range(8)]
    for h in range(PEER_HEADS):
        qh = jnp.dot(xb, wq_ref[:, h * 2 * PEER_HALF:(h + 1) * 2 * PEER_HALF],
                     preferred_element_type=F32)
        q1 = qh[:, :PEER_HALF]
        q2 = qh[:, PEER_HALF:]
        s1 = lax.dot_general(k1, q1, nt, precision=HI, preferred_element_type=F32)
        s2 = lax.dot_general(k2, q2, nt, precision=HI, preferred_element_type=F32)
        v1, i1 = _top16_rows(s1, key_iota)
        v2, i2 = _top16_rows(s2, key_iota)
        cands, ids, orders = [], [], []
        for a, rows in blocks:
            bi = fiota((rows, tt))
            ok = (bi + 1.0) * (a + 1) <= PEER_TOPK
            cands.append(jnp.where(ok, v1[a:a + 1, :] + v2[0:rows, :], -jnp.inf))
            ids.append(i1[a:a + 1, :] * PEER_NKEYS + i2[0:rows, :])
            orders.append(a * PEER_TOPK + bi)
        ai = fiota((8, tt)) + 8.0
        cands.append(v1[8:16, :] + v2[0:1, :])
        ids.append(i1[8:16, :] * PEER_NKEYS + i2[0:1, :])
        orders.append(ai * PEER_TOPK)
        cand = jnp.concatenate(cands, axis=0)
        ckey = jnp.concatenate(orders, axis=0) * float(N_EXPERTS) + jnp.concatenate(ids, axis=0)
        ts, tkey = _top16_rows(cand, ckey)
        experts = tkey - jnp.floor(tkey * (1.0 / N_EXPERTS)) * float(N_EXPERTS)
        ex = jnp.exp(ts - ts[0:1, :])
        gate = ex / jnp.sum(ex, axis=0, keepdims=True)
        e_ref[h * PEER_TOPK:(h + 1) * PEER_TOPK, :] = experts.astype(I32)
        g_ref[h * PEER_TOPK:(h + 1) * PEER_TOPK, :] = gate


def peer_topk(xn, wq, k1, k2, tt=256):
    t, d = xn.shape
    tt = min(tt, t)
    ne = PEER_HEADS * PEER_TOPK
    keys = pl.BlockSpec((PEER_NKEYS, PEER_HALF), lambda i: (0, 0))
    out = pl.BlockSpec((ne, tt), lambda i: (0, i))
    return pl.pallas_call(
        _peer_topk_kernel,
        grid=(t // tt,),
        in_specs=[pl.BlockSpec((tt, d), lambda i: (i, 0)),
                  pl.BlockSpec(wq.shape, lambda i: (0, 0)),
                  keys, keys],
        out_specs=[out, out],
        out_shape=[jax.ShapeDtypeStruct((ne, t), I32), jax.ShapeDtypeStruct((ne, t), F32)],
        compiler_params=_cparams(1),
        name="peer_topk",
    )(xn, wq, k1, k2)


def _gelu_exact(x):
    return 0.5 * x * (1.0 + lax.erf(x * (2.0 ** -0.5)))


def _peer_dense_kernel(i1a_ref, i2a_ref, ga_ref, i1b_ref, i2b_ref, gb_ref, xn_ref, u_ref, v_ref,
                       out_ref, gate_ref, stage_ref, *, tm, eb):
    i = pl.program_id(0)
    j = pl.program_id(1)
    nk = PEER_NKEYS
    grp = PEER_BUILD_GROUP
    par = i % 2
    key = lax.broadcasted_iota(I32, (nk, nk), 0)
    nt = (((1,), (1,)), ((), ()))

    def build_group(i1_ref, i2_ref, g_ref, t0, slot):
        for k in range(grp):
            t = t0 + k
            a_t = jnp.where(key == i1_ref[pl.ds(t, 1), :], 1.0, 0.0).astype(BF16)
            b_t = jnp.where(key == i2_ref[pl.ds(t, 1), :], g_ref[pl.ds(t, 1), :],
                            0.0).astype(BF16)
            g_t = lax.dot_general(a_t, b_t, nt, preferred_element_type=F32)
            stage_ref[pl.ds(k, nk, stride=grp), :] = g_t
        for r in range(nk):
            gate_ref[slot, pl.ds(pl.multiple_of(r * tm + t0, grp), grp), :] = (
                stage_ref[r * grp:(r + 1) * grp, :].astype(BF16))

    @pl.when(j == 0)
    def _():
        out_ref[...] = jnp.zeros_like(out_ref)

    @pl.when((i == 0) & (j == 0))
    def _():
        def first(tq, carry):
            build_group(i1a_ref, i2a_ref, ga_ref, pl.multiple_of(tq * grp, grp), 0)
            return carry

        lax.fori_loop(0, tm // grp, first, 0)

    build_group(i1b_ref, i2b_ref, gb_ref, pl.multiple_of((j * grp) % tm, grp), 1 - par)

    act = lax.dot_general(xn_ref[...], u_ref[...], nt, preferred_element_type=F32)
    nb = eb // nk
    z = [(_gelu_exact(act[:, b * nk:(b + 1) * nk])
          * gate_ref[par, pl.ds(pl.multiple_of((j * nb + b) * tm, tm), tm), :].astype(F32)
          ).astype(BF16)
         for b in range(nb)]
    out_ref[...] += jnp.dot(jnp.concatenate(z, axis=1), v_ref[...], preferred_element_type=F32)


def peer_dense(xn, expert_t, gate_t, u_tab, v_tab, tm=512, eb=512):
    t, d = xn.shape
    tm = min(tm, t)
    nti = t // tm
    nsteps = N_EXPERTS // eb
    assert tm % PEER_BUILD_GROUP == 0 and tm <= nsteps * PEER_BUILD_GROUP
    ns = expert_t.shape[0]
    i1 = (expert_t // PEER_NKEYS).T
    i2 = (expert_t % PEER_NKEYS).T
    g = gate_t.T
    slot_a = pl.BlockSpec((tm, ns), lambda i, j: (0, 0))
    slot_b = pl.BlockSpec((tm, ns), lambda i, j: (jnp.minimum(i + 1, nti - 1), 0))
    row = pl.BlockSpec((tm, d), lambda i, j: (i, 0), pipeline_mode=pl.Buffered(1))
    return pl.pallas_call(
        functools.partial(_peer_dense_kernel, tm=tm, eb=eb),
        grid=(nti, nsteps),
        in_specs=[slot_a, slot_a, slot_a, slot_b, slot_b, slot_b, row,
                  pl.BlockSpec((eb, d), lambda i, j: (j, 0)),
                  pl.BlockSpec((eb, d), lambda i, j: (j, 0))],
        out_specs=row,
        out_shape=jax.ShapeDtypeStruct((t, d), F32),
        scratch_shapes=[pltpu.VMEM((2, PEER_NKEYS * tm, PEER_NKEYS), BF16),
                        pltpu.VMEM((PEER_NKEYS * PEER_BUILD_GROUP, PEER_NKEYS), F32)],
        compiler_params=_cparams(2),
        name="peer_dense",
    )(i1, i2, g, i1, i2, g, xn.astype(BF16), u_tab.astype(BF16), v_tab.astype(BF16))


def peer_layer(xn, wq, k1, k2, u_tab, v_tab):
    expert_t, gate_t = peer_topk(xn, wq.astype(BF16), k1, k2)
    return peer_dense(xn, expert_t, gate_t, u_tab, v_tab)


def _trunk(x, xn, parts, even_mix_norm, even_w_in, even_gate_bias, even_pool_w, even_pool_scale,
           even_w_out, odd_mix_norm, odd_fourier_w, ffn_norm, peer_wq, peer_k1, peer_k2,
           peer_u, peer_v, final_norm):
    n_main = 4 * D_A + D_B
    w_in = even_w_in[0]
    z = matmul(xn, w_in[:, :n_main].astype(BF16), tn=1024, name="in_proj")
    w_gate = jnp.pad(w_in[:, n_main:], ((0, 0), (0, LANES - N_GATES)))
    gates = matmul_hi(xn, w_gate)
    gates_t = gates[:, :N_GATES].T
    bias = even_gate_bias[0].astype(F32)
    bias_row = jnp.pad(bias, (0, LANES - N_GATES)).reshape(1, LANES)
    bias_col = bias.reshape(N_GATES, 1)
    pw = even_pool_w[0].astype(BF16)
    hs, ybs = [], []
    for base, batch, seq in parts:
        hs.append(mlstm(z, gates, gates_t, bias_row, bias_col, base, batch, seq))
        ybs.append(pool(z, pw, even_pool_scale[0], 4 * D_A, base, batch, seq))
    x, xn = outproj(hs, z, ybs, even_w_out[0].astype(BF16), x, ffn_norm[0])
    y = peer_layer(xn, peer_wq[0], peer_k1[0], peer_k2[0], peer_u[0], peer_v[0])

    x, xn = add_rmsnorm(x, y, odd_mix_norm[0])
    cd, sd = dft_tables(FOURIER_GROUP_DIM)
    yc, ys = fourier_channels(xn, cd, sd)
    fs = []
    for base, batch, seq in parts:
        cs, ss = dft_tables(seq)
        fs.append(fourier_seq(cs, ss, yc, ys, base, batch, seq,
                              tm=min(1024, seq), tk=min(512, seq)))
    x, xn = matmul_res_norm(fs, odd_fourier_w[0].astype(BF16), x, ffn_norm[1], name="fourier_out")
    y = peer_layer(xn, peer_wq[1], peer_k1[1], peer_k2[1], peer_u[1], peer_v[1])
    return [add_rmsnorm_part(x, y, final_norm, base, batch * seq) for base, batch, seq in parts]


def kernel(x_prompt, x_sample, even_mix_norm, even_w_in, even_gate_bias, even_pool_w,
           even_pool_scale, even_w_out, odd_mix_norm, odd_fourier_w, ffn_norm, peer_wq,
           peer_k1, peer_k2, peer_u, peer_v, final_norm):
    bp, sp, d = x_prompt.shape
    bs, ss, _ = x_sample.shape
    tp = bp * sp
    x, xn = concat_rmsnorm(x_prompt.reshape(tp, d), x_sample.reshape(bs * ss, d), even_mix_norm[0])
    parts = ((0, bp, sp), (tp, bs, ss))
    yp, ys = _trunk(x, xn, parts, even_mix_norm, even_w_in, even_gate_bias, even_pool_w,
                    even_pool_scale, even_w_out, odd_mix_norm, odd_fourier_w, ffn_norm,
                    peer_wq, peer_k1, peer_k2, peer_u, peer_v, final_norm)
    return yp.reshape(bp, sp, d), ys.reshape(bs, ss, d)
```
